```python
import jax, jax.numpy as jnp
from jax import lax
import numpy as np

D_MODEL = 2048
BATCH = 32
SEQ = 256
DEPTH = 1
DEC_BATCH = 4
DEC_SEQ = 1024
PAST_LEN = 512

GRID_W = 64
N_HEADS = 16
Q_LORA = 512
KV_LORA = 256
D_NOPE = 128
D_ROPE = 64
D_V = 128
ROPE_FREQS = D_ROPE // 4
ROPE_BASE = 10000.0
Q_BLOCK = 128
GM_WIDTH = 2048
GM_GROUPS = 8
GM_GC = GM_WIDTH // GM_GROUPS
CHUNK = 128
N_EXPERTS = 16
D_FF_EXPERT = D_MODEL // 2
EC_FACTOR = 2
IN_SPLITS = (Q_LORA, KV_LORA, D_ROPE, GM_WIDTH, GM_WIDTH, D_MODEL, D_MODEL)
IN_WIDTH = sum(IN_SPLITS)
EPS = 1e-6

kernel_name = "hybrid_mla_gmlp_ec_diffusion_step"


def rms(x, g):
    xf = x.astype(jnp.float32)
    y = xf * lax.rsqrt(jnp.mean(xf * xf, axis=-1, keepdims=True) + EPS)
    return (y * g.astype(jnp.float32)).astype(x.dtype)


def rope_angles(n):
    t = jnp.arange(n, dtype=jnp.int32)
    row = (t // GRID_W).astype(jnp.float32)
    col = (t % GRID_W).astype(jnp.float32)
    inv = 1.0 / (ROPE_BASE ** (jnp.arange(ROPE_FREQS, dtype=jnp.float32) / ROPE_FREQS))
    return jnp.stack([row[:, None] * inv, col[:, None] * inv], axis=1)


def apply_rope2d(x, ang):
    xr = x.reshape(*x.shape[:-1], 2, 2, ROPE_FREQS)
    x1, x2 = xr[..., 0, :], xr[..., 1, :]
    cos = jnp.cos(ang).astype(x.dtype)
    sin = jnp.sin(ang).astype(x.dtype)
    out = jnp.stack([x1 * cos - x2 * sin, x2 * cos + x1 * sin], axis=-2)
    return out.reshape(x.shape)


def attend(q_nope, q_rope, k_nope, k_rope, v):
    b, n, h, _ = q_nope.shape
    nb = n // Q_BLOCK
    scale = (D_NOPE + D_ROPE) ** -0.5
    qn = q_nope.reshape(b, nb, Q_BLOCK, h, D_NOPE).transpose(1, 0, 2, 3, 4)
    qr = q_rope.reshape(b, nb, Q_BLOCK, h, D_ROPE).transpose(1, 0, 2, 3, 4)

    def one_block(args):
        qn_b, qr_b = args
        s = (jnp.einsum('bqhd,bkhd->bhqk', qn_b, k_nope)
             + jnp.einsum('bqhr,bkr->bhqk', qr_b, k_rope)).astype(jnp.float32) * scale
        p = jax.nn.softmax(s, axis=-1).astype(v.dtype)
        return jnp.einsum('bhqk,bkhd->bqhd', p, v)

    o = lax.map(one_block, (qn, qr))
    return o.transpose(1, 0, 2, 3, 4).reshape(b, n, h * D_V)


def spatial_gating(gu, gv, g_sgu, w_s, b_s):
    b, n, _ = gu.shape
    v = rms(gv, g_sgu).reshape(b, n // CHUNK, CHUNK, GM_GROUPS, GM_GC)
    s = jnp.einsum('gpq,bnqgc->bnpgc', w_s, v) + b_s.T[:, :, None]
    return gu * s.reshape(b, n, GM_WIDTH)


def ec_moe(h, w_router, w_e1, w_e3, w_e2):
    b, n, d = h.shape
    cap = EC_FACTOR * n // N_EXPERTS
    aff = jax.nn.softmax((h @ w_router).astype(jnp.float32), axis=-1)
    gate, idx = lax.top_k(jnp.swapaxes(aff, 1, 2), cap)
    xg = jax.vmap(lambda hb, ib: hb[ib])(h, idx)
    a = jnp.einsum('becd,edf->becf', xg, w_e1)
    g3 = jnp.einsum('becd,edf->becf', xg, w_e3)
    y = jnp.einsum('becf,efd->becd', jax.nn.silu(a) * g3, w_e2) * gate[..., None].astype(h.dtype)
    return jax.vmap(lambda yb, ib: jnp.zeros((n, d), yb.dtype).at[ib.reshape(-1)].add(yb.reshape(-1, d)))(y, idx)


def trunk_layer(x, mod, p, ang, ctx):
    b, n, _ = x.shape
    sh1, sc1, gate1, sh2, sc2, gate2 = jnp.split(mod, 6, axis=-1)
    h = rms(x, p['g_attn']) * (1 + sc1) + sh1
    cuts = list(np.cumsum(IN_SPLITS)[:-1])
    q_a, kv_c, k_r, gu, gv, ga, gb = jnp.split(h @ p['w_in'], cuts, axis=-1)
    q = (rms(q_a, p['g_qa']) @ p['w_qb']).reshape(b, n, N_HEADS, D_NOPE + D_ROPE)
    q_nope, q_rope = q[..., :D_NOPE], q[..., D_NOPE:]
    ckv = rms(kv_c, p['g_kv'])
    if ang is not None:
        q_rope = apply_rope2d(q_rope, ang[:, None])
        k_r = apply_rope2d(k_r, ang)
    k_all, kr_all = ckv, k_r
    if ctx is not None:
        k_all = jnp.concatenate([ckv, ctx[0]], axis=1)
        kr_all = jnp.concatenate([k_r, ctx[1]], axis=1)
    k_nope = jnp.einsum('bkr,rhd->bkhd', k_all, p['w_uk'])
    v = jnp.einsum('bkr,rhd->bkhd', k_all, p['w_uv'])
    o_a = attend(q_nope, q_rope, k_nope, kr_all, v) @ p['w_o_mla']
    o_b = spatial_gating(jax.nn.gelu(gu), jax.nn.gelu(gv), p['g_sgu'], p['w_s'], p['b_s']) @ p['w_o_gm']
    merged = jax.nn.sigmoid(ga) * o_a + jax.nn.sigmoid(gb) * o_b
    x = x + gate1 * (merged @ p['w_out'])
    h2 = rms(x, p['g_ffn']) * (1 + sc2) + sh2
    x = x + gate2 * ec_moe(h2, p['w_router'], p['w_e1'], p['w_e3'], p['w_e2'])
    return x, ckv, k_r


def setup_inputs(seed: int = 0) -> dict:
    key = jax.random.key(seed)
    ks = jax.random.split(key, 32)
    f = jnp.float32
    nrm = lambda k, shape, s: jax.random.normal(k, shape, f) * s
    L, D, H = DEPTH, D_MODEL, N_HEADS
    return {
        'x_prompt': nrm(ks[0], (BATCH, SEQ, D), 1.0),
        'x_sample': nrm(ks[1], (DEC_BATCH, DEC_SEQ, D), 1.0),
        'c': nrm(ks[2], (DEC_BATCH, D), 1.0),
        'cache_ckv': nrm(ks[3], (DEC_BATCH, L, PAST_LEN, KV_LORA), 1.0),
        'cache_krope': nrm(ks[4], (DEC_BATCH, L, PAST_LEN, D_ROPE), 1.0),
        'c_ctx': nrm(ks[5], (D,), 1.0),
        'g_attn': 1.0 + nrm(ks[6], (L, D), 0.02),
        'g_ffn': 1.0 + nrm(ks[7], (L, D), 0.02),
        'w_ada': nrm(ks[8], (L, D, 6 * D), 0.5 * D ** -0.5),
        'b_ada': nrm(ks[9], (L, 6 * D), 0.02),
        'w_in': nrm(ks[10], (L, D, IN_WIDTH), D ** -0.5),
        'g_qa': 1.0 + nrm(ks[11], (L, Q_LORA), 0.02),
        'w_qb': nrm(ks[12], (L, Q_LORA, H * (D_NOPE + D_ROPE)), Q_LORA ** -0.5),
        'g_kv': 1.0 + nrm(ks[13], (L, KV_LORA), 0.02),
        'w_uk': nrm(ks[14], (L, KV_LORA, H, D_NOPE), KV_LORA ** -0.5),
        'w_uv': nrm(ks[15], (L, KV_LORA, H, D_V), KV_LORA ** -0.5),
        'w_o_mla': nrm(ks[16], (L, H * D_V, D), (H * D_V) ** -0.5),
        'g_sgu': 1.0 + nrm(ks[17], (L, GM_WIDTH), 0.02),
        'w_s': nrm(ks[18], (L, GM_GROUPS, CHUNK, CHUNK), CHUNK ** -0.5),
        'b_s': 1.0 + nrm(ks[19], (L, GM_GROUPS, CHUNK), 0.02),
        'w_o_gm': nrm(ks[20], (L, GM_WIDTH, D), GM_WIDTH ** -0.5),
        'w_out': nrm(ks[21], (L, D, D), D ** -0.5),
        'w_router': nrm(ks[22], (L, D, N_EXPERTS), D ** -0.5),
        'w_e1': nrm(ks[23], (L, N_EXPERTS, D, D_FF_EXPERT), D ** -0.5),
        'w_e3': nrm(ks[24], (L, N_EXPERTS, D, D_FF_EXPERT), D ** -0.5),
        'w_e2': nrm(ks[25], (L, N_EXPERTS, D_FF_EXPERT, D), D_FF_EXPERT ** -0.5),
        'g_final': 1.0 + nrm(ks[26], (D,), 0.02),
    }


def reference(x_prompt, x_sample, c, cache_ckv, cache_krope, c_ctx,
              g_attn, g_ffn, w_ada, b_ada, w_in, g_qa, w_qb, g_kv, w_uk, w_uv, w_o_mla,
              g_sgu, w_s, b_s, w_o_gm, w_out, w_router, w_e1, w_e3, w_e2, g_final):
    ang = rope_angles(x_sample.shape[1])
    xc, xl = x_prompt, x_sample
    ckv_list, kr_list = [], []
    for l in range(DEPTH):
        p = dict(g_attn=g_attn[l], g_ffn=g_ffn[l], w_in=w_in[l], g_qa=g_qa[l], w_qb=w_qb[l],
                 g_kv=g_kv[l], w_uk=w_uk[l], w_uv=w_uv[l], w_o_mla=w_o_mla[l], g_sgu=g_sgu[l],
                 w_s=w_s[l], b_s=b_s[l], w_o_gm=w_o_gm[l], w_out=w_out[l], w_router=w_router[l],
                 w_e1=w_e1[l], w_e3=w_e3[l], w_e2=w_e2[l])
        mod_ctx = jax.nn.silu(c_ctx) @ w_ada[l] + b_ada[l]
        mod_lat = (jax.nn.silu(c) @ w_ada[l] + b_ada[l])[:, None, :]
        xc, ckv_c, kr_c = trunk_layer(xc, mod_ctx, p, None, None)
        ckv_list.append(ckv_c)
        kr_list.append(kr_c)
        xl, _, _ = trunk_layer(xl, mod_lat, p, ang, (cache_ckv[:, l], cache_krope[:, l]))
    y_prompt = rms(xc, g_final)
    y_sample = rms(xl, g_final)
    new_ckv = jnp.stack(ckv_list, axis=1)
    new_krope = jnp.stack(kr_list, axis=1)
    return (y_prompt, y_sample, new_ckv, new_krope)
```

```python
import functools

import numpy as np
import jax
import jax.numpy as jnp
from jax import lax
from jax.experimental import pallas as pl
from jax.experimental.pallas import tpu as pltpu

F32 = jnp.float32
BF16 = jnp.bfloat16

N_HEADS = 16
D_NOPE = 128
D_ROPE = 64
D_V = 128
HEAD_SLOT = 256
ROPE_FREQS = D_ROPE // 4
ROPE_BASE = 10000.0
GRID_W = 64
CHUNK = 128
GM_GROUPS = 8
N_EXPERTS = 16
EC_FACTOR = 2
EPS = 1e-6
MOD_ROWS = 8
LOGIT_LANES = 128

VMEM_LIMIT_V7X = 56 * 1024 * 1024


def _params(n_axes):
    return pltpu.CompilerParams(
        dimension_semantics=("arbitrary",) * n_axes, vmem_limit_bytes=VMEM_LIMIT_V7X)


def _const_spec(shape):
    nd = len(shape)
    return pl.BlockSpec(shape, lambda *_: (0,) * nd, pipeline_mode=pl.Buffered(1))


def _sigmoid(x):
    return 1.0 / (1.0 + jnp.exp(-x))


def _gelu_tanh(x):
    c = np.float32(np.sqrt(2.0 / np.pi))
    return x * (0.5 * (1.0 + jnp.tanh(c * (x + np.float32(0.044715) * (x * x * x)))))


def _rms_rows(x, g):
    return x * lax.rsqrt(jnp.mean(x * x, axis=-1, keepdims=True) + EPS) * g


def _bdot(a, b):
    return jnp.dot(a, b, preferred_element_type=F32)


def _bdot_t(a, bt):
    return lax.dot_general(a, bt, (((1,), (1,)), ((), ())), preferred_element_type=F32)


def _mod_kernel(c_ref, w_ref, b_ref, o_ref):
    c = c_ref[...]
    s = (c * _sigmoid(c)).astype(BF16)
    o_ref[...] = _bdot(s, w_ref[...].astype(BF16)) + b_ref[...]


def _modulation(c_rows, w_ada, b_ada, bn=1024):
    d, n = w_ada.shape
    return pl.pallas_call(
        _mod_kernel,
        grid=(n // bn,),
        in_specs=[
            _const_spec((MOD_ROWS, d)),
            pl.BlockSpec((d, bn), lambda j: (0, j)),
            pl.BlockSpec((1, bn), lambda j: (0, j)),
        ],
        out_specs=pl.BlockSpec((MOD_ROWS, bn), lambda j: (0, j)),
        out_shape=jax.ShapeDtypeStruct((MOD_ROWS, n), F32),
        compiler_params=_params(1),
        name="adaln_mod",
    )(c_rows, w_ada, b_ada)


def _cast_kernel(x_ref, o_ref, *, front_blocks):
    i = pl.program_id(0)

    @pl.when(i < front_blocks)
    def _():
        o_ref[...] = jnp.zeros(o_ref.shape, o_ref.dtype)

    @pl.when(i >= front_blocks)
    def _():
        o_ref[...] = x_ref[...].astype(o_ref.dtype)


def _cast_bf16(x, *, rows, front_blocks=0):
    r, c = x.shape
    assert r % rows == 0 and rows % 16 == 0
    nb = r // rows + front_blocks
    return pl.pallas_call(
        functools.partial(_cast_kernel, front_blocks=front_blocks),
        grid=(nb,),
        in_specs=[pl.BlockSpec((rows, c), lambda i: (jnp.maximum(i - front_blocks, 0), 0))],
        out_specs=pl.BlockSpec((rows, c), lambda i: (i, 0)),
        out_shape=jax.ShapeDtypeStruct((nb * rows, c), BF16),
        compiler_params=_params(1),
        name="cast_bf16",
    )(x)


def _inproj_kernel(xc_ref, xl_ref, g_ref, sc_ref, sh_ref, ws_ref, gqa_ref, gkv_ref, cos_ref, sin_ref, wb_ref,
                   big_ref, qa_ref, ckv_ref, kr_ref, h_scr, *, n_ctx_tiles, q_lora, kv_lora):
    i = pl.program_id(0)
    j = pl.program_id(1)

    def prologue(x_ref, rotary):
        h = _rms_rows(x_ref[...], g_ref[...]) * (1.0 + sc_ref[...]) + sh_ref[...]
        hb = h.astype(BF16)
        h_scr[...] = hb
        small = _bdot_t(hb, ws_ref[...])
        qa_ref[...] = _rms_rows(small[:, :q_lora], gqa_ref[...]).astype(BF16)
        ckv_ref[...] = _rms_rows(small[:, q_lora:q_lora + kv_lora], gkv_ref[...])
        o = q_lora + kv_lora
        kr = small[:, o:o + 128]
        if rotary:
            kr_sw = small[:, o + 128:o + 256]
            kr = kr * cos_ref[...] + kr_sw * sin_ref[...]
        kr_ref[...] = kr

    @pl.when((j == 0) & (i < n_ctx_tiles))
    def _():
        prologue(xc_ref, False)

    @pl.when((j == 0) & (i >= n_ctx_tiles))
    def _():
        prologue(xl_ref, True)

    big_ref[...] = _bdot_t(h_scr[...], wb_ref[...]).astype(BF16)


def _in_proj(xc, xl, g_attn, mod6, w_small_t, g_qa, g_kv, cos_t, sin_t, w_all_t, *, n_big, tm, bn):
    n_ctx, d = xc.shape
    n_lat_seq = cos_t.shape[0]
    t = n_ctx + xl.shape[0]
    q_lora, kv_lora = g_qa.shape[1], g_kv.shape[1]
    blk0 = (w_all_t.shape[0] - n_big) // bn
    n_ctx_tiles = n_ctx // tm
    tiles_per_seq = n_lat_seq // tm

    def mod_row(i):
        return jnp.where(i < n_ctx_tiles, MOD_ROWS // 2, (i - n_ctx_tiles) // tiles_per_seq)

    def rope_blk(i):
        return jnp.maximum(i - n_ctx_tiles, 0) % tiles_per_seq

    kern = functools.partial(_inproj_kernel, n_ctx_tiles=n_ctx_tiles, q_lora=q_lora, kv_lora=kv_lora)
    return pl.pallas_call(
        kern,
        grid=(t // tm, n_big // bn),
        in_specs=[
            pl.BlockSpec((tm, d), lambda i, j: (jnp.minimum(i, n_ctx_tiles - 1), 0)),
            pl.BlockSpec((tm, d), lambda i, j: (jnp.maximum(i - n_ctx_tiles, 0), 0)),
            _const_spec((1, d)),
            pl.BlockSpec((None, None, 1, d), lambda i, j: (1, mod_row(i), 0, 0)),
            pl.BlockSpec((None, None, 1, d), lambda i, j: (0, mod_row(i), 0, 0)),
            _const_spec(w_small_t.shape),
            _const_spec((1, q_lora)),
            _const_spec((1, kv_lora)),
            pl.BlockSpec((tm, 128), lambda i, j: (rope_blk(i), 0)),
            pl.BlockSpec((tm, 128), lambda i, j: (rope_blk(i), 0)),
            pl.BlockSpec((bn, d), lambda i, j: (blk0 + j, 0)),
        ],
        out_specs=[
            pl.BlockSpec((tm, bn), lambda i, j: (i, j)),
            pl.BlockSpec((tm, q_lora), lambda i, j: (i, 0)),
            pl.BlockSpec((tm, kv_lora), lambda i, j: (i, 0)),
            pl.BlockSpec((tm, 128), lambda i, j: (i, 0)),
        ],
        out_shape=[
            jax.ShapeDtypeStruct((t, n_big), BF16),
            jax.ShapeDtypeStruct((t, q_lora), BF16),
            jax.ShapeDtypeStruct((t, kv_lora), F32),
            jax.ShapeDtypeStruct((t, 128), F32),
        ],
        scratch_shapes=[pltpu.VMEM((tm, d), BF16)],
        compiler_params=_params(2),
        name="in_proj",
    )(xc, xl, g_attn, mod6, mod6, w_small_t, g_qa, g_kv, cos_t, sin_t, w_all_t)


def _qproj_kernel(qa_ref, wq_ref, wsw_ref, cos_ref, sin_ref, q_ref, *, n_ctx_tiles, scale):
    i = pl.program_id(0)
    qa = qa_ref[...]

    @pl.when(i < n_ctx_tiles)
    def _():
        for h in range(N_HEADS):
            q = _bdot(qa, wq_ref[:, h * HEAD_SLOT:(h + 1) * HEAD_SLOT])
            q_ref[h] = (q * scale).astype(BF16)

    @pl.when(i >= n_ctx_tiles)
    def _():
        cos = cos_ref[...]
        sin = sin_ref[...]
        for hp in range(N_HEADS // 2):
            q_sw = _bdot(qa, wsw_ref[:, hp * 256:(hp + 1) * 256])
            for s in range(2):
                h = 2 * hp + s
                q = _bdot(qa, wq_ref[:, h * HEAD_SLOT:(h + 1) * HEAD_SLOT])
                q_ref[h, :, 0:128] = (q[:, 0:128] * scale).astype(BF16)
                rot = q[:, 128:256] * cos + q_sw[:, s * 128:(s + 1) * 128] * sin
                q_ref[h, :, 128:256] = (rot * scale).astype(BF16)


def _q_proj(qa_n, wq_p, wq_sw, cos_t, sin_t, *, n_ctx, n_lat_seq, tm):
    t, q_lora = qa_n.shape
    n_ctx_tiles = n_ctx // tm
    tiles_per_seq = n_lat_seq // tm

    def rope_blk(i):
        return jnp.maximum(i - n_ctx_tiles, 0) % tiles_per_seq

    scale = np.float32((D_NOPE + D_ROPE) ** -0.5)
    kern = functools.partial(_qproj_kernel, n_ctx_tiles=n_ctx_tiles, scale=scale)
    return pl.pallas_call(
        kern,
        grid=(t // tm,),
        in_specs=[
            pl.BlockSpec((tm, q_lora), lambda i: (i, 0)),
            _const_spec(wq_p.shape),
            _const_spec(wq_sw.shape),
            pl.BlockSpec((tm, 128), lambda i: (rope_blk(i), 0)),
            pl.BlockSpec((tm, 128), lambda i: (rope_blk(i), 0)),
        ],
        out_specs=pl.BlockSpec((N_HEADS, tm, HEAD_SLOT), lambda i: (0, i, 0)),
        out_shape=jax.ShapeDtypeStruct((N_HEADS, t, HEAD_SLOT), BF16),
        compiler_params=_params(1),
        name="q_proj",
    )(qa_n, wq_p, wq_sw, cos_t, sin_t)


def _attn_kernel(*refs, n_own, n_cache):
    if n_cache:
        (q_ref, ckv_ref, kr_ref, cckv_ref, ckr_ref, wuk_ref, wuv_ref,
         o_ref, kpad, vexp, kall, krp, o_scr) = refs
    else:
        (q_ref, ckv_ref, kr_ref, wuk_ref, wuv_ref, o_ref, kpad, vexp, kall, krp, o_scr) = refs
    qi = pl.program_id(1)

    @pl.when(qi == 0)
    def _():
        kall[0:n_own, :] = ckv_ref[...].astype(BF16)
        krp[0:n_own, :] = kr_ref[...].astype(BF16)
        if n_cache:
            kall[n_own:n_own + n_cache, :] = cckv_ref[...].astype(BF16)
            krp[n_own:n_own + n_cache, 0:D_ROPE] = ckr_ref[...].astype(BF16)
            krp[n_own:n_own + n_cache, D_ROPE:128] = jnp.zeros((n_cache, 128 - D_ROPE), BF16)

        def expand(hp, carry):
            kn = _bdot(kall[...], wuk_ref[hp]).astype(BF16)
            vv = _bdot(kall[...], wuv_ref[hp]).astype(BF16)
            for s in range(2):
                kpad[2 * hp + s, :, 0:128] = kn[:, s * 128:(s + 1) * 128]
                kpad[2 * hp + s, :, 128:256] = krp[...]
                vexp[2 * hp + s] = vv[:, s * 128:(s + 1) * 128]
            return carry

        lax.fori_loop(0, N_HEADS // 2, expand, 0)

    def head(h, carry):
        s = lax.dot_general(q_ref[h], kpad[h], (((1,), (1,)), ((), ())), preferred_element_type=F32)
        p = jnp.exp(s - jnp.max(s, axis=-1, keepdims=True))
        l = jnp.sum(p, axis=-1, keepdims=True)
        o = _bdot(p.astype(BF16), vexp[h])
        o_scr[h] = (o / l).astype(BF16)
        return carry

    lax.fori_loop(0, N_HEADS, head, 0)
    for h in range(N_HEADS):
        o_ref[:, h * D_V:(h + 1) * D_V] = o_scr[h]


def _attention(q_pad, ckv, kr, w_uk, w_uv, cache_ckv, cache_kr, *, row0, n_req, n_own, tq):
    kv_lora = ckv.shape[1]
    n_cache = 0 if cache_ckv is None else cache_ckv.shape[1]
    kn = n_own + n_cache
    qb = n_own // tq
    in_specs = [
        pl.BlockSpec((N_HEADS, tq, HEAD_SLOT), lambda b, qi: (0, row0 // tq + b * qb + qi, 0)),
        pl.BlockSpec((n_own, kv_lora), lambda b, qi: (row0 // n_own + b, 0)),
        pl.BlockSpec((n_own, 128), lambda b, qi: (row0 // n_own + b, 0)),
    ]
    args = [q_pad, ckv, kr]
    if n_cache:
        in_specs += [
            pl.BlockSpec((None, n_cache, kv_lora), lambda b, qi: (b, 0, 0)),
            pl.BlockSpec((None, n_cache, D_ROPE), lambda b, qi: (b, 0, 0)),
        ]
        args += [cache_ckv, cache_kr]
    in_specs += [_const_spec(w_uk.shape), _const_spec(w_uv.shape)]
    args += [w_uk, w_uv]
    return pl.pallas_call(
        functools.partial(_attn_kernel, n_own=n_own, n_cache=n_cache),
        grid=(n_req, qb),
        in_specs=in_specs,
        out_specs=pl.BlockSpec((tq, N_HEADS * D_V), lambda b, qi: (b * qb + qi, 0)),
        out_shape=jax.ShapeDtypeStruct((n_req * n_own, N_HEADS * D_V), BF16),
        scratch_shapes=[
            pltpu.VMEM((N_HEADS, kn, HEAD_SLOT), BF16),
            pltpu.VMEM((N_HEADS, kn, D_V), BF16),
            pltpu.VMEM((kn, kv_lora), BF16),
            pltpu.VMEM((kn, 128), BF16),
            pltpu.VMEM((N_HEADS, tq, D_V), BF16),
        ],
        compiler_params=_params(2),
        name="mla_attn_cache" if n_cache else "mla_attn",
    )(*args)


def _oproj_kernel(xc_ref, xl_ref, w_ref, o_ref, *, n_ctx_tiles):
    i = pl.program_id(0)

    @pl.when(i < n_ctx_tiles)
    def _():
        o_ref[...] = _bdot(xc_ref[...], w_ref[...]).astype(o_ref.dtype)

    @pl.when(i >= n_ctx_tiles)
    def _():
        o_ref[...] = _bdot(xl_ref[...], w_ref[...]).astype(o_ref.dtype)


def _o_proj(xc, xl, w, *, tm, bn):
    n_ctx, k = xc.shape
    t = n_ctx + xl.shape[0]
    n = w.shape[1]
    n_ctx_tiles = n_ctx // tm
    return pl.pallas_call(
        functools.partial(_oproj_kernel, n_ctx_tiles=n_ctx_tiles),
        grid=(t // tm, n // bn),
        in_specs=[
            pl.BlockSpec((tm, k), lambda i, j: (jnp.minimum(i, n_ctx_tiles - 1), 0)),
            pl.BlockSpec((tm, k), lambda i, j: (jnp.maximum(i - n_ctx_tiles, 0), 0)),
            pl.BlockSpec((k, bn), lambda i, j: (0, j)),
        ],
        out_specs=pl.BlockSpec((tm, bn), lambda i, j: (i, j)),
        out_shape=jax.ShapeDtypeStruct((t, n), BF16),
        compiler_params=_params(2),
        name="o_proj",
    )(xc, xl, w)


def _sgu_kernel(gu_ref, gv_ref, g_ref, ws_ref, bs_ref, wo_ref, o_ref, z_scr, *, tm, gc):
    j = pl.program_id(1)

    @pl.when(j == 0)
    def _():
        for c in range(tm // CHUNK):
            r0 = c * CHUNK
            v = _gelu_tanh(gv_ref[r0:r0 + CHUNK, :].astype(F32))
            vn = _rms_rows(v, g_ref[...]).astype(BF16)
            for g in range(GM_GROUPS):
                c0 = g * gc
                s = _bdot(ws_ref[g], vn[:, c0:c0 + gc]) + bs_ref[:, g:g + 1]
                u = _gelu_tanh(gu_ref[r0:r0 + CHUNK, c0:c0 + gc].astype(F32))
                z_scr[r0:r0 + CHUNK, c0:c0 + gc] = (u * s).astype(BF16)

    o_ref[...] = _bdot(z_scr[...], wo_ref[...]).astype(BF16)


def _sgu(big, g_sgu, w_s, b_s_t, w_o_gm, *, tm, bn):
    t = big.shape[0]
    width = g_sgu.shape[1]
    d = w_o_gm.shape[1]
    gc = width // GM_GROUPS
    return pl.pallas_call(
        functools.partial(_sgu_kernel, tm=tm, gc=gc),
        grid=(t // tm, d // bn),
        in_specs=[
            pl.BlockSpec((tm, width), lambda i, j: (i, 0)),
            pl.BlockSpec((tm, width), lambda i, j: (i, 1)),
            _const_spec((1, width)),
            _const_spec(w_s.shape),
            _const_spec(b_s_t.shape),
            pl.BlockSpec((width, bn), lambda i, j: (0, j)),
        ],
        out_specs=pl.BlockSpec((tm, bn), lambda i, j: (i, j)),
        out_shape=jax.ShapeDtypeStruct((t, d), BF16),
        scratch_shapes=[pltpu.VMEM((tm, width), BF16)],
        compiler_params=_params(2),
        name="sgu",
    )(big, big, g_sgu, w_s, b_s_t, w_o_gm)


def _merge_kernel(oa_ref, ob_ref, ga_ref, gb_ref, xc_ref, xl_ref, gate_ref, sc_ref, sh_ref, g_ref, wo_ref, wr_ref,
                  x1_ref, h2_ref, lg_ref, m_scr, x1_scr, *, bn, n_blk, n_ctx_tiles):
    i = pl.program_id(0)
    j = pl.program_id(1)

    @pl.when(j == 0)
    def _():
        for r0 in range(0, m_scr.shape[0], 128):
            rows = slice(r0, r0 + 128)
            merged = (_sigmoid(ga_ref[rows, :].astype(F32)) * oa_ref[rows, :].astype(F32)
                      + _sigmoid(gb_ref[rows, :].astype(F32)) * ob_ref[rows, :].astype(F32))
            m_scr[rows, :] = merged.astype(BF16)

    x = jnp.where(i < n_ctx_tiles, xc_ref[...], xl_ref[...])
    x1 = x + gate_ref[...] * _bdot(m_scr[...], wo_ref[...])
    x1_ref[...] = x1
    for jj in range(n_blk):
        @pl.when(j == jj)
        def _(jj=jj):
            x1_scr[:, jj * bn:(jj + 1) * bn] = x1

    @pl.when(j == n_blk - 1)
    def _():
        h2 = (_rms_rows(x1_scr[...], g_ref[...]) * (1.0 + sc_ref[...]) + sh_ref[...]).astype(BF16)
        h2_ref[...] = h2
        lg_ref[...] = _bdot(h2, wr_ref[...])


def _merge(o_a, o_b, big, xc, xl, mod6, g_ffn, w_out, w_router_p, *, n_lat_seq, tm, bn):
    n_ctx, d = xc.shape
    t = n_ctx + xl.shape[0]
    n_ctx_tiles = n_ctx // tm
    tiles_per_seq = n_lat_seq // tm
    n_blk = d // bn

    def mod_row(i):
        return jnp.where(i < n_ctx_tiles, MOD_ROWS // 2, (i - n_ctx_tiles) // tiles_per_seq)

    def xc_idx(i, j):
        return jnp.minimum(i, n_ctx_tiles - 1), jnp.where(i < n_ctx_tiles, j, n_blk - 1)

    def xl_idx(i, j):
        return jnp.maximum(i - n_ctx_tiles, 0), jnp.where(i < n_ctx_tiles, 0, j)

    return pl.pallas_call(
        functools.partial(_merge_kernel, bn=bn, n_blk=n_blk, n_ctx_tiles=n_ctx_tiles),
        grid=(t // tm, n_blk),
        in_specs=[
            pl.BlockSpec((tm, d), lambda i, j: (i, 0)),
            pl.BlockSpec((tm, d), lambda i, j: (i, 0)),
            pl.BlockSpec((tm, d), lambda i, j: (i, 2)),
            pl.BlockSpec((tm, d), lambda i, j: (i, 3)),
            pl.BlockSpec((tm, bn), xc_idx),
            pl.BlockSpec((tm, bn), xl_idx),
            pl.BlockSpec((None, None, 1, bn), lambda i, j: (2, mod_row(i), 0, j)),
            pl.BlockSpec((None, None, 1, d), lambda i, j: (4, mod_row(i), 0, 0)),
            pl.BlockSpec((None, None, 1, d), lambda i, j: (3, mod_row(i), 0, 0)),
            _const_spec((1, d)),
            pl.BlockSpec((d, bn), lambda i, j: (0, j)),
            _const_spec(w_router_p.shape),
        ],
        out_specs=[
            pl.BlockSpec((tm, bn), lambda i, j: (i, j)),
            pl.BlockSpec((tm, d), lambda i, j: (i, 0)),
            pl.BlockSpec((tm, LOGIT_LANES), lambda i, j: (i, 0)),
        ],
        out_shape=[
            jax.ShapeDtypeStruct((t, d), F32),
            jax.ShapeDtypeStruct((t, d), BF16),
            jax.ShapeDtypeStruct((t, LOGIT_LANES), F32),
        ],
        scratch_shapes=[pltpu.VMEM((tm, d), BF16), pltpu.VMEM((tm, d), F32)],
        compiler_params=_params(2),
        name="merge_out_proj",
    )(o_a, o_b, big, big, xc, xl, mod6, mod6, mod6, g_ffn, w_out, w_router_p)


BISECT_STEPS = 48
MIN_NORMAL_F32 = float(np.finfo(np.float32).tiny)


def _route_kernel(lg_ref, key_ref, aff_ref, tri_scr, *, n_sets, n, cap):
    for r0 in range(0, n, 128):
        r = lax.broadcasted_iota(jnp.int32, (128, n), 0) + r0
        c = lax.broadcasted_iota(jnp.int32, (128, n), 1)
        tri_scr[r0:r0 + 128, :] = jnp.where(r < c, 1.0, 0.0).astype(BF16)

    for s in range(n_sets):
        logits = lg_ref[s * n:(s + 1) * n, :].T[0:N_EXPERTS, :]
        e = jnp.exp(logits - jnp.max(logits, axis=0, keepdims=True))
        aff_ref[s * N_EXPERTS:(s + 1) * N_EXPERTS, :] = e / jnp.sum(e, axis=0, keepdims=True)
    aff = aff_ref[...]
    rows = n_sets * N_EXPERTS

    def count_ge(thr):
        return jnp.sum(jnp.where(aff >= thr, 1.0, 0.0), axis=1, keepdims=True)

    def bisect(_, carry):
        lo, hi = carry
        mid = jnp.sqrt(lo) * jnp.sqrt(hi)
        ok = count_ge(mid) >= cap
        return jnp.where(ok, mid, lo), jnp.where(ok, hi, mid)

    lo0 = jnp.full((rows, 1), MIN_NORMAL_F32, F32)
    hi0 = jnp.full((rows, 1), 2.0, F32)
    lo, hi = lax.fori_loop(0, BISECT_STEPS, bisect, (lo0, hi0))
    lo = jnp.where(count_ge(lo) >= cap, lo, 0.0)

    above = aff >= hi
    band = (aff >= lo) & jnp.logical_not(above)
    need = cap - jnp.sum(jnp.where(above, 1.0, 0.0), axis=1, keepdims=True)
    tri = tri_scr[...]
    band_before = _bdot(jnp.where(band, 1.0, 0.0).astype(BF16), tri)
    sel = above | (band & (band_before < need))
    pos = _bdot(jnp.where(sel, 1.0, 0.0).astype(BF16), tri)
    key_ref[...] = jnp.where(sel, pos, -1.0)


def _route(logits, *, row0, n_sets, n):
    cap = EC_FACTOR * n // N_EXPERTS
    rows = n_sets * N_EXPERTS
    blk = row0 // (n_sets * n)
    return pl.pallas_call(
        functools.partial(_route_kernel, n_sets=n_sets, n=n, cap=cap),
        grid=(1,),
        in_specs=[pl.BlockSpec((n_sets * n, LOGIT_LANES), lambda g: (blk, 0))],
        out_specs=[pl.BlockSpec((rows, n), lambda g: (0, 0)), pl.BlockSpec((rows, n), lambda g: (0, 0))],
        out_shape=[jax.ShapeDtypeStruct((rows, n), F32), jax.ShapeDtypeStruct((rows, n), F32)],
        scratch_shapes=[pltpu.VMEM((n, n), BF16)],
        compiler_params=_params(1),
        name=f"route_{n}",
    )(logits)


def _gather_kernel(key_ref, aff_ref, h2_ref, pt_ref, gate_ref, xg_ref, p_scr, *, n, cap, d, nb):
    key = key_ref[...]
    aff = aff_ref[...]
    slot = lax.broadcasted_iota(jnp.int32, (cap, n), 0).astype(F32)
    per_group = 128 // cap
    for grp in range(N_EXPERTS // per_group):
        pieces = []
        for ex in range(grp * per_group, (grp + 1) * per_group):
            hit = slot == key[ex:ex + 1, :]
            gate_ref[ex] = jnp.sum(jnp.where(hit, aff[ex:ex + 1, :], 0.0), axis=1, keepdims=True)
            pieces.append(jnp.where(hit, 1.0, 0.0))
        hit128 = pieces[0] if per_group == 1 else jnp.concatenate(pieces, axis=0)
        p_scr[grp * 128:(grp + 1) * 128, :] = hit128.astype(BF16)
        pt_ref[:, grp * 128:(grp + 1) * 128] = hit128.T.astype(BF16)

    p = p_scr[...]
    for c in range(d // nb):
        xg = _bdot(p, h2_ref[:, c * nb:(c + 1) * nb]).astype(BF16)
        for ex in range(N_EXPERTS):
            xg_ref[ex, :, c * nb:(c + 1) * nb] = xg[ex * cap:(ex + 1) * cap, :]


def _dispatch_gather(key, aff, h2, *, row0, n_sets, n):
    d = h2.shape[1]
    cap = EC_FACTOR * n // N_EXPERTS
    slots = N_EXPERTS * cap
    blk0 = row0 // n
    return pl.pallas_call(
        functools.partial(_gather_kernel, n=n, cap=cap, d=d, nb=256),
        grid=(n_sets,),
        in_specs=[
            pl.BlockSpec((N_EXPERTS, n), lambda b: (b, 0)),
            pl.BlockSpec((N_EXPERTS, n), lambda b: (b, 0)),
            pl.BlockSpec((n, d), lambda b: (blk0 + b, 0)),
        ],
        out_specs=[
            pl.BlockSpec((None, n, slots), lambda b: (b, 0, 0)),
            pl.BlockSpec((N_EXPERTS, None, cap, 1), lambda b: (0, b, 0, 0)),
            pl.BlockSpec((N_EXPERTS, None, cap, d), lambda b: (0, b, 0, 0)),
        ],
        out_shape=[
            jax.ShapeDtypeStruct((n_sets, n, slots), BF16),
            jax.ShapeDtypeStruct((N_EXPERTS, n_sets, cap, 1), F32),
            jax.ShapeDtypeStruct((N_EXPERTS, n_sets, cap, d), BF16),
        ],
        scratch_shapes=[pltpu.VMEM((slots, n), BF16)],
        compiler_params=_params(1),
        name=f"dispatch_gather_{n}",
    )(key, aff, h2)


def _expert_kernel(xc_ref, xl_ref, gc_ref, gl_ref, w1_ref, w3_ref, w2_ref, yc_ref, yl_ref, hc_scr, hl_scr,
                   *, n_f, fc, rows_c, rows_l, d):
    k = pl.program_id(1)

    @pl.when(k < n_f)
    def _():
        w1 = w1_ref[...].astype(BF16)
        w3 = w3_ref[...].astype(BF16)
        for x_ref, h_scr, rows in ((xc_ref, hc_scr, rows_c), (xl_ref, hl_scr, rows_l)):
            x = x_ref[...].reshape(rows, d)
            a = _bdot(x, w1)
            g = _bdot(x, w3)
            h_scr[k] = (a * _sigmoid(a) * g).astype(BF16)

    @pl.when(k >= n_f)
    def _():
        w2 = w2_ref[...].astype(BF16)
        for h_scr, g_ref, y_ref, rows in ((hc_scr, gc_ref, yc_ref, rows_c), (hl_scr, gl_ref, yl_ref, rows_l)):
            y = _bdot(h_scr[0], w2[0:fc, :])
            for kk in range(1, n_f):
                y = y + _bdot(h_scr[kk], w2[kk * fc:(kk + 1) * fc, :])
            y_ref[...] = (y * g_ref[...].reshape(rows, 1)).astype(BF16).reshape(y_ref.shape)


def _experts(xg_c, xg_l, gate_c, gate_l, w_e1, w_e3, w_e2, *, fc=256, nc=512):
    n_e, sets_c, cap_c, d = xg_c.shape
    _, sets_l, cap_l, _ = xg_l.shape
    d_ff = w_e1.shape[2]
    n_f = d_ff // fc
    n_c = d // nc
    rows_c, rows_l = sets_c * cap_c, sets_l * cap_l

    def f_idx(k):
        return jnp.minimum(k, n_f - 1)

    def c_idx(k):
        return jnp.maximum(k - n_f, 0)

    kern = functools.partial(_expert_kernel, n_f=n_f, fc=fc, rows_c=rows_c, rows_l=rows_l, d=d)
    return pl.pallas_call(
        kern,
        grid=(n_e, n_f + n_c),
        in_specs=[
            pl.BlockSpec((None, sets_c, cap_c, d), lambda e, k: (e, 0, 0, 0)),
            pl.BlockSpec((None, sets_l, cap_l, d), lambda e, k: (e, 0, 0, 0)),
            pl.BlockSpec((None, sets_c, cap_c, 1), lambda e, k: (e, 0, 0, 0)),
            pl.BlockSpec((None, sets_l, cap_l, 1), lambda e, k: (e, 0, 0, 0)),
            pl.BlockSpec((None, d, fc), lambda e, k: (e, 0, f_idx(k))),
            pl.BlockSpec((None, d, fc), lambda e, k: (e, 0, f_idx(k))),
            pl.BlockSpec((None, d_ff, nc), lambda e, k: (e, 0, c_idx(k))),
        ],
        out_specs=[
            pl.BlockSpec((None, sets_c, cap_c, nc), lambda e, k: (e, 0, 0, c_idx(k))),
            pl.BlockSpec((None, sets_l, cap_l, nc), lambda e, k: (e, 0, 0, c_idx(k))),
        ],
        out_shape=[
            jax.ShapeDtypeStruct(xg_c.shape, BF16),
            jax.ShapeDtypeStruct(xg_l.shape, BF16),
        ],
        scratch_shapes=[pltpu.VMEM((n_f, rows_c, fc), BF16), pltpu.VMEM((n_f, rows_l, fc), BF16)],
        compiler_params=_params(2),
        name="experts",
    )(xg_c, xg_l, gate_c, gate_l, w_e1, w_e3, w_e2)


def _combine_kernel(pt_ref, y_ref, x1_ref, gate_ref, g_ref, o_ref, acc_scr, *, slots, d, nb):
    pt = pt_ref[...]
    for c in range(d // nb):
        y = y_ref[:, :, c * nb:(c + 1) * nb].reshape(slots, nb)
        moe = _bdot(pt, y)
        acc_scr[:, c * nb:(c + 1) * nb] = (x1_ref[:, c * nb:(c + 1) * nb]
                                           + gate_ref[:, c * nb:(c + 1) * nb] * moe)
    o_ref[...] = _rms_rows(acc_scr[...], g_ref[...])


def _combine(pt, y, x1, mod6, g_final, *, row0, n_sets, n, mod_row_fn):
    d = x1.shape[1]
    cap = EC_FACTOR * n // N_EXPERTS
    slots = N_EXPERTS * cap
    tn = min(n, 512)
    nt = n // tn
    blk0 = row0 // tn
    return pl.pallas_call(
        functools.partial(_combine_kernel, slots=slots, d=d, nb=512),
        grid=(n_sets, nt),
        in_specs=[
            pl.BlockSpec((None, tn, slots), lambda b, r: (b, r, 0)),
            pl.BlockSpec((N_EXPERTS, None, cap, d), lambda b, r: (0, b, 0, 0)),
            pl.BlockSpec((tn, d), lambda b, r: (blk0 + b * nt + r, 0)),
            pl.BlockSpec((None, None, 1, d), lambda b, r: (5, mod_row_fn(b), 0, 0)),
            _const_spec((1, d)),
        ],
        out_specs=pl.BlockSpec((tn, d), lambda b, r: (b * nt + r, 0)),
        out_shape=jax.ShapeDtypeStruct((n_sets * n, d), F32),
        scratch_shapes=[pltpu.VMEM((tn, d), F32)],
        compiler_params=_params(2),
        name=f"combine_{n}",
    )(pt, y, x1, mod6, g_final)


def _rope_tables(n):
    tpos = jnp.arange(n, dtype=jnp.int32)
    row = (tpos // GRID_W).astype(F32)
    col = (tpos % GRID_W).astype(F32)
    inv = 1.0 / (ROPE_BASE ** (jnp.arange(ROPE_FREQS, dtype=F32) / ROPE_FREQS))
    ang = jnp.stack([row[:, None] * inv, col[:, None] * inv], axis=1)
    cos = jnp.cos(ang)[:, :, None, :]
    sin = jnp.sin(ang)[:, :, None, :]
    cos = jnp.broadcast_to(cos, (n, 2, 2, ROPE_FREQS)).reshape(n, D_ROPE)
    sin = jnp.concatenate([-sin, sin], axis=2).reshape(n, D_ROPE)
    pad = jnp.zeros((n, 128 - D_ROPE), F32)
    return jnp.concatenate([cos, pad], axis=1), jnp.concatenate([sin, pad], axis=1)


def _swap_rotary_halves(w):
    perm = np.arange(D_ROPE) ^ ROPE_FREQS
    return w[..., perm]


def kernel(x_prompt, x_sample, c, cache_ckv, cache_krope, c_ctx, g_attn, g_ffn, w_ada, b_ada, w_in, g_qa,
           w_qb, g_kv, w_uk, w_uv, w_o_mla, g_sgu, w_s, b_s, w_o_gm, w_out, w_router, w_e1, w_e3, w_e2,
           g_final):
    batch, seq, d = x_prompt.shape
    dec_batch, dec_seq, _ = x_sample.shape
    depth = g_attn.shape[0]
    assert depth == 1
    q_lora, kv_lora = g_qa.shape[1], g_kv.shape[1]
    gm_width = g_sgu.shape[1]
    n_ctx, n_lat = batch * seq, dec_batch * dec_seq
    assert dec_batch < MOD_ROWS // 2 + 1

    xc = x_prompt.reshape(n_ctx, d)
    xl = x_sample.reshape(n_lat, d)

    c_rows = jnp.zeros((MOD_ROWS, d), F32).at[:dec_batch].set(c).at[MOD_ROWS // 2].set(c_ctx)
    mod = _modulation(c_rows, w_ada[0], b_ada[0][None, :])
    mod6 = mod.reshape(MOD_ROWS, 6, 1, d).transpose(1, 0, 2, 3)

    cos_t, sin_t = _rope_tables(dec_seq)

    w_in_t = w_in[0].T
    o_kr = q_lora + kv_lora
    w_kr_t = w_in_t[o_kr:o_kr + D_ROPE]
    zpad = jnp.zeros((128 - D_ROPE, d), F32)
    w_small_t = jnp.concatenate(
        [w_in_t[:o_kr], w_kr_t, zpad, w_kr_t[np.arange(D_ROPE) ^ ROPE_FREQS], zpad], axis=0).astype(BF16)
    tm, bn_in = 512, 1024
    front = -(o_kr + D_ROPE) % bn_in
    w_all_t = _cast_bf16(w_in_t, rows=front, front_blocks=1)
    big, qa_n, ckv, kr = _in_proj(xc, xl, g_attn, mod6, w_small_t, g_qa, g_kv, cos_t, sin_t, w_all_t,
                                  n_big=w_in_t.shape[0] - o_kr - D_ROPE, tm=tm, bn=bn_in)

    wq3 = w_qb[0].reshape(q_lora, N_HEADS, D_NOPE + D_ROPE)
    wq_nope, wq_rope = wq3[:, :, :D_NOPE], wq3[:, :, D_NOPE:]
    zq = jnp.zeros((q_lora, N_HEADS, HEAD_SLOT - D_NOPE - D_ROPE), F32)
    wq_p = jnp.concatenate([wq_nope, wq_rope, zq], axis=2).reshape(q_lora, N_HEADS * HEAD_SLOT).astype(BF16)
    wq_sw = jnp.concatenate([_swap_rotary_halves(wq_rope), zq], axis=2).reshape(
        q_lora, N_HEADS * 128).astype(BF16)
    q_pad = _q_proj(qa_n, wq_p, wq_sw, cos_t, sin_t, n_ctx=n_ctx, n_lat_seq=dec_seq, tm=tm)

    w_uk2 = w_uk[0].reshape(kv_lora, N_HEADS // 2, 2 * D_NOPE).transpose(1, 0, 2).astype(BF16)
    w_uv2 = w_uv[0].reshape(kv_lora, N_HEADS // 2, 2 * D_V).transpose(1, 0, 2).astype(BF16)
    o_ctx = _attention(q_pad, ckv, kr, w_uk2, w_uv2, None, None,
                       row0=0, n_req=batch, n_own=seq, tq=seq)
    o_lat = _attention(q_pad, ckv, kr, w_uk2, w_uv2, cache_ckv[:, 0], cache_krope[:, 0],
                       row0=n_ctx, n_req=dec_batch, n_own=dec_seq, tq=512)

    o_a = _o_proj(o_ctx, o_lat, w_o_mla[0].astype(BF16), tm=tm, bn=1024)
    o_b = _sgu(big, g_sgu, w_s[0].astype(BF16), b_s[0].T, w_o_gm[0].astype(BF16), tm=tm, bn=1024)

    w_router_p = jnp.concatenate(
        [w_router[0], jnp.zeros((d, LOGIT_LANES - N_EXPERTS), F32)], axis=1).astype(BF16)
    x1, h2, logits = _merge(o_a, o_b, big, xc, xl, mod6, g_ffn, w_out[0].astype(BF16), w_router_p,
                            n_lat_seq=dec_seq, tm=tm, bn=1024)

    key_c, aff_c = _route(logits, row0=0, n_sets=batch, n=seq)
    key_l, aff_l = _route(logits, row0=n_ctx, n_sets=dec_batch, n=dec_seq)
    p_c, gate_c, xg_c = _dispatch_gather(key_c, aff_c, h2, row0=0, n_sets=batch, n=seq)
    p_l, gate_l, xg_l = _dispatch_gather(key_l, aff_l, h2, row0=n_ctx, n_sets=dec_batch, n=dec_seq)

    y_c, y_l = _experts(xg_c, xg_l, gate_c, gate_l, w_e1[0], w_e3[0], w_e2[0])

    y_prompt = _combine(p_c, y_c, x1, mod6, g_final[None, :], row0=0, n_sets=batch, n=seq,
                        mod_row_fn=lambda b: MOD_ROWS // 2)
    y_sample = _combine(p_l, y_l, x1, mod6, g_final[None, :], row0=n_ctx, n_sets=dec_batch, n=dec_seq,
                        mod_row_fn=lambda b: b)

    new_ckv = ckv[:n_ctx].reshape(batch, 1, seq, kv_lora)
    new_krope = kr[:n_ctx, :D_ROPE].reshape(batch, 1, seq, D_ROPE)
    return (y_prompt.reshape(batch, seq, d), y_sample.reshape(dec_batch, dec_seq, d), new_ckv, new_krope)
```

```python
import functools

import numpy as np
import jax
import jax.numpy as jnp
from jax import lax
from jax.experimental import pallas as pl
from jax.experimental.pallas import tpu as pltpu

F32 = jnp.float32
BF16 = jnp.bfloat16

N_HEADS = 16
D_NOPE = 128
D_ROPE = 64
D_V = 128
HEAD_SLOT = 256
ROPE_FREQS = D_ROPE // 4
ROPE_BASE = 10000.0
GRID_W = 64
CHUNK = 128
GM_GROUPS = 8
N_EXPERTS = 16
EC_FACTOR = 2
EPS = 1e-6
MOD_ROWS = 8
LOGIT_LANES = 128

VMEM_LIMIT_V7X = 56 * 1024 * 1024


def _params(n_axes):
    return pltpu.CompilerParams(
        dimension_semantics=("arbitrary",) * n_axes, vmem_limit_bytes=VMEM_LIMIT_V7X)


def _const_spec(shape):
    nd = len(shape)
    return pl.BlockSpec(shape, lambda *_: (0,) * nd, pipeline_mode=pl.Buffered(1))


def _half_tanh_half(x):
    return jnp.tanh(0.5 * x)


def _silu(x):
    h = 0.5 * x
    return h * jnp.tanh(h) + h


def _gelu_tanh(x):
    c = np.float32(np.sqrt(2.0 / np.pi))
    return x * (0.5 * (1.0 + jnp.tanh(c * (x + np.float32(0.044715) * (x * x * x)))))


def _rms_rows(x, g):
    return x * lax.rsqrt(jnp.mean(x * x, axis=-1, keepdims=True) + EPS) * g


def _bdot(a, b):
    return jnp.dot(a, b, preferred_element_type=F32)


def _bdot_t(a, bt):
    return lax.dot_general(a, bt, (((1,), (1,)), ((), ())), preferred_element_type=F32)


def _mod_kernel(c_ref, w_ref, b_ref, o_ref):
    c = c_ref[...]
    s = _silu(c).astype(BF16)
    o_ref[...] = _bdot(s, w_ref[...].astype(BF16)) + b_ref[...]


def _modulation(c_rows, w_ada, b_ada, bn=1024):
    d, n = w_ada.shape
    return pl.pallas_call(
        _mod_kernel,
        grid=(n // bn,),
        in_specs=[
            _const_spec((MOD_ROWS, d)),
            pl.BlockSpec((d, bn), lambda j: (0, j)),
            pl.BlockSpec((1, bn), lambda j: (0, j)),
        ],
        out_specs=pl.BlockSpec((MOD_ROWS, bn), lambda j: (0, j)),
        out_shape=jax.ShapeDtypeStruct((MOD_ROWS, n), F32),
        compiler_params=_params(1),
        name="adaln_mod",
    )(c_rows, w_ada, b_ada)


def _cast_kernel(x_ref, o_ref, *, front_blocks):
    i = pl.program_id(0)

    @pl.when(i < front_blocks)
    def _():
        o_ref[...] = jnp.zeros(o_ref.shape, o_ref.dtype)

    @pl.when(i >= front_blocks)
    def _():
        o_ref[...] = x_ref[...].astype(o_ref.dtype)


def _cast_bf16(x, *, rows, front_blocks=0):
    r, c = x.shape
    assert r % rows == 0 and rows % 16 == 0
    nb = r // rows + front_blocks
    return pl.pallas_call(
        functools.partial(_cast_kernel, front_blocks=front_blocks),
        grid=(nb,),
        in_specs=[pl.BlockSpec((rows, c), lambda i: (jnp.maximum(i - front_blocks, 0), 0))],
        out_specs=pl.BlockSpec((rows, c), lambda i: (i, 0)),
        out_shape=jax.ShapeDtypeStruct((nb * rows, c), BF16),
        compiler_params=_params(1),
        name="cast_bf16",
    )(x)


def _inproj_kernel(xc_ref, xl_ref, g_ref, sc_ref, sh_ref, ws_ref, gqa_ref, gkv_ref, cos_ref, sin_ref, wb_ref,
                   big_ref, qa_ref, ckv_ref, kr_ref, h_scr, *, n_ctx_tiles, q_lora, kv_lora):
    i = pl.program_id(0)
    j = pl.program_id(1)

    def prologue(x_ref, rotary):
        h = _rms_rows(x_ref[...], g_ref[...]) * (1.0 + sc_ref[...]) + sh_ref[...]
        hb = h.astype(BF16)
        h_scr[...] = hb
        small = _bdot_t(hb, ws_ref[...])
        qa_ref[...] = _rms_rows(small[:, :q_lora], gqa_ref[...]).astype(BF16)
        ckv_ref[...] = _rms_rows(small[:, q_lora:q_lora + kv_lora], gkv_ref[...])
        o = q_lora + kv_lora
        kr = small[:, o:o + 128]
        if rotary:
            kr_sw = small[:, o + 128:o + 256]
            kr = kr * cos_ref[...] + kr_sw * sin_ref[...]
        kr_ref[...] = kr

    @pl.when((j == 0) & (i < n_ctx_tiles))
    def _():
        prologue(xc_ref, False)

    @pl.when((j == 0) & (i >= n_ctx_tiles))
    def _():
        prologue(xl_ref, True)

    big_ref[...] = _bdot_t(h_scr[...], wb_ref[...]).astype(BF16)


def _in_proj(xc, xl, g_attn, mod6, w_small_t, g_qa, g_kv, cos_t, sin_t, w_all_t, *, n_big, tm, bn):
    n_ctx, d = xc.shape
    n_lat_seq = cos_t.shape[0]
    t = n_ctx + xl.shape[0]
    q_lora, kv_lora = g_qa.shape[1], g_kv.shape[1]
    blk0 = (w_all_t.shape[0] - n_big) // bn
    n_ctx_tiles = n_ctx // tm
    tiles_per_seq = n_lat_seq // tm

    def mod_row(i):
        return jnp.where(i < n_ctx_tiles, MOD_ROWS // 2, (i - n_ctx_tiles) // tiles_per_seq)

    def rope_blk(i):
        return jnp.maximum(i - n_ctx_tiles, 0) % tiles_per_seq

    kern = functools.partial(_inproj_kernel, n_ctx_tiles=n_ctx_tiles, q_lora=q_lora, kv_lora=kv_lora)
    return pl.pallas_call(
        kern,
        grid=(t // tm, n_big // bn),
        in_specs=[
            pl.BlockSpec((tm, d), lambda i, j: (jnp.minimum(i, n_ctx_tiles - 1), 0)),
            pl.BlockSpec((tm, d), lambda i, j: (jnp.maximum(i - n_ctx_tiles, 0), 0)),
            _const_spec((1, d)),
            pl.BlockSpec((None, None, 1, d), lambda i, j: (1, mod_row(i), 0, 0)),
            pl.BlockSpec((None, None, 1, d), lambda i, j: (0, mod_row(i), 0, 0)),
            _const_spec(w_small_t.shape),
            _const_spec((1, q_lora)),
            _const_spec((1, kv_lora)),
            pl.BlockSpec((tm, 128), lambda i, j: (rope_blk(i), 0)),
            pl.BlockSpec((tm, 128), lambda i, j: (rope_blk(i), 0)),
            pl.BlockSpec((bn, d), lambda i, j: (blk0 + j, 0)),
        ],
        out_specs=[
            pl.BlockSpec((tm, bn), lambda i, j: (i, j)),
            pl.BlockSpec((tm, q_lora), lambda i, j: (i, 0)),
            pl.BlockSpec((tm, kv_lora), lambda i, j: (i, 0)),
            pl.BlockSpec((tm, 128), lambda i, j: (i, 0)),
        ],
        out_shape=[
            jax.ShapeDtypeStruct((t, n_big), BF16),
            jax.ShapeDtypeStruct((t, q_lora), BF16),
            jax.ShapeDtypeStruct((t, kv_lora), F32),
            jax.ShapeDtypeStruct((t, 128), F32),
        ],
        scratch_shapes=[pltpu.VMEM((tm, d), BF16)],
        compiler_params=_params(2),
        name="in_proj",
    )(xc, xl, g_attn, mod6, mod6, w_small_t, g_qa, g_kv, cos_t, sin_t, w_all_t)


def _qproj_kernel(qa_ref, wq_ref, wsw_ref, cos_ref, sin_ref, q_ref, *, n_ctx_tiles, scale):
    i = pl.program_id(0)
    qa = qa_ref[...]

    @pl.when(i < n_ctx_tiles)
    def _():
        for h in range(N_HEADS):
            q = _bdot(qa, wq_ref[:, h * HEAD_SLOT:(h + 1) * HEAD_SLOT])
            q_ref[h] = (q * scale).astype(BF16)

    @pl.when(i >= n_ctx_tiles)
    def _():
        cos = cos_ref[...]
        sin = sin_ref[...]
        for hp in range(N_HEADS // 2):
            q_sw = _bdot(qa, wsw_ref[:, hp * 256:(hp + 1) * 256])
            for s in range(2):
                h = 2 * hp + s
                q = _bdot(qa, wq_ref[:, h * HEAD_SLOT:(h + 1) * HEAD_SLOT])
                q_ref[h, :, 0:128] = (q[:, 0:128] * scale).astype(BF16)
                rot = q[:, 128:256] * cos + q_sw[:, s * 128:(s + 1) * 128] * sin
                q_ref[h, :, 128:256] = (rot * scale).astype(BF16)


def _q_proj(qa_n, wq_p, wq_sw, cos_t, sin_t, *, n_ctx, n_lat_seq, tm):
    t, q_lora = qa_n.shape
    n_ctx_tiles = n_ctx // tm
    tiles_per_seq = n_lat_seq // tm

    def rope_blk(i):
        return jnp.maximum(i - n_ctx_tiles, 0) % tiles_per_seq

    scale = np.float32((D_NOPE + D_ROPE) ** -0.5)
    kern = functools.partial(_qproj_kernel, n_ctx_tiles=n_ctx_tiles, scale=scale)
    return pl.pallas_call(
        kern,
        grid=(t // tm,),
        in_specs=[
            pl.BlockSpec((tm, q_lora), lambda i: (i, 0)),
            _const_spec(wq_p.shape),
            _const_spec(wq_sw.shape),
            pl.BlockSpec((tm, 128), lambda i: (rope_blk(i), 0)),
            pl.BlockSpec((tm, 128), lambda i: (rope_blk(i), 0)),
        ],
        out_specs=pl.BlockSpec((N_HEADS, tm, HEAD_SLOT), lambda i: (0, i, 0)),
        out_shape=jax.ShapeDtypeStruct((N_HEADS, t, HEAD_SLOT), BF16),
        compiler_params=_params(1),
        name="q_proj",
    )(qa_n, wq_p, wq_sw, cos_t, sin_t)


def _attn_kernel(*refs, n_own, n_cache, heads_per_iter):
    if n_cache:
        (q_ref, ckv_ref, kr_ref, cckv_ref, ckr_ref, wuk_ref, wuv_ref,
         o_ref, kpad, vexp, kall, krp, o_scr) = refs
    else:
        (q_ref, ckv_ref, kr_ref, wuk_ref, wuv_ref, o_ref, kpad, vexp, kall, krp, o_scr) = refs
    qi = pl.program_id(1)

    @pl.when(qi == 0)
    def _():
        kall[0:n_own, :] = ckv_ref[...].astype(BF16)
        krp[0:n_own, :] = kr_ref[...].astype(BF16)
        if n_cache:
            kall[n_own:n_own + n_cache, :] = cckv_ref[...].astype(BF16)
            krp[n_own:n_own + n_cache, 0:D_ROPE] = ckr_ref[...].astype(BF16)
            krp[n_own:n_own + n_cache, D_ROPE:128] = jnp.zeros((n_cache, 128 - D_ROPE), BF16)

        def expand(hp, carry):
            kn = _bdot(kall[...], wuk_ref[hp]).astype(BF16)
            vv = _bdot(kall[...], wuv_ref[hp]).astype(BF16)
            for s in range(2):
                kpad[2 * hp + s, :, 0:128] = kn[:, s * 128:(s + 1) * 128]
                kpad[2 * hp + s, :, 128:256] = krp[...]
                vexp[2 * hp + s] = vv[:, s * 128:(s + 1) * 128]
            return carry

        if heads_per_iter == N_HEADS:
            for hp in range(N_HEADS // 2):
                expand(hp, 0)
        else:
            lax.fori_loop(0, N_HEADS // 2, expand, 0)

    def one_head(h):
        s = _bdot_t(q_ref[h], kpad[h])
        p = jnp.exp(s - jnp.max(s, axis=-1, keepdims=True))
        l = jnp.sum(p, axis=-1, keepdims=True)
        o = _bdot(p.astype(BF16), vexp[h])
        return (o / l).astype(BF16)

    if heads_per_iter == N_HEADS:
        for h in range(N_HEADS):
            o_ref[:, h * D_V:(h + 1) * D_V] = one_head(h)
    else:
        def head_group(hg, carry):
            for g in range(heads_per_iter):
                h = hg * heads_per_iter + g
                o_scr[h] = one_head(h)
            return carry

        lax.fori_loop(0, N_HEADS // heads_per_iter, head_group, 0)
        for h in range(N_HEADS):
            o_ref[:, h * D_V:(h + 1) * D_V] = o_scr[h]


def _attention(q_pad, ckv, kr, w_uk, w_uv, cache_ckv, cache_kr, *, row0, n_req, n_own, tq, heads_per_iter):
    kv_lora = ckv.shape[1]
    n_cache = 0 if cache_ckv is None else cache_ckv.shape[1]
    kn = n_own + n_cache
    qb = n_own // tq
    in_specs = [
        pl.BlockSpec((N_HEADS, tq, HEAD_SLOT), lambda b, qi: (0, row0 // tq + b * qb + qi, 0)),
        pl.BlockSpec((n_own, kv_lora), lambda b, qi: (row0 // n_own + b, 0)),
        pl.BlockSpec((n_own, 128), lambda b, qi: (row0 // n_own + b, 0)),
    ]
    args = [q_pad, ckv, kr]
    if n_cache:
        in_specs += [
            pl.BlockSpec((None, n_cache, kv_lora), lambda b, qi: (b, 0, 0)),
            pl.BlockSpec((None, n_cache, D_ROPE), lambda b, qi: (b, 0, 0)),
        ]
        args += [cache_ckv, cache_kr]
    in_specs += [_const_spec(w_uk.shape), _const_spec(w_uv.shape)]
    args += [w_uk, w_uv]
    return pl.pallas_call(
        functools.partial(_attn_kernel, n_own=n_own, n_cache=n_cache, heads_per_iter=heads_per_iter),
        grid=(n_req, qb),
        in_specs=in_specs,
        out_specs=pl.BlockSpec((tq, N_HEADS * D_V), lambda b, qi: (b * qb + qi, 0)),
        out_shape=jax.ShapeDtypeStruct((n_req * n_own, N_HEADS * D_V), BF16),
        scratch_shapes=[
            pltpu.VMEM((N_HEADS, kn, HEAD_SLOT), BF16),
            pltpu.VMEM((N_HEADS, kn, D_V), BF16),
            pltpu.VMEM((kn, kv_lora), BF16),
            pltpu.VMEM((kn, 128), BF16),
            pltpu.VMEM((N_HEADS, tq, D_V), BF16),
        ],
        compiler_params=_params(2),
        name="mla_attn_cache" if n_cache else "mla_attn",
    )(*args)


def _oproj_kernel(xc_ref, xl_ref, w_ref, o_ref, *, n_ctx_tiles):
    i = pl.program_id(0)

    @pl.when(i < n_ctx_tiles)
    def _():
        o_ref[...] = _bdot(xc_ref[...], w_ref[...]).astype(o_ref.dtype)

    @pl.when(i >= n_ctx_tiles)
    def _():
        o_ref[...] = _bdot(xl_ref[...], w_ref[...]).astype(o_ref.dtype)


def _o_proj(xc, xl, w, *, tm):
    n_ctx, k = xc.shape
    t = n_ctx + xl.shape[0]
    n = w.shape[1]
    n_ctx_tiles = n_ctx // tm
    return pl.pallas_call(
        functools.partial(_oproj_kernel, n_ctx_tiles=n_ctx_tiles),
        grid=(t // tm,),
        in_specs=[
            pl.BlockSpec((tm, k), lambda i: (jnp.minimum(i, n_ctx_tiles - 1), 0)),
            pl.BlockSpec((tm, k), lambda i: (jnp.maximum(i - n_ctx_tiles, 0), 0)),
            _const_spec(w.shape),
        ],
        out_specs=pl.BlockSpec((tm, n), lambda i: (i, 0)),
        out_shape=jax.ShapeDtypeStruct((t, n), BF16),
        compiler_params=_params(1),
        name="o_proj",
    )(xc, xl, w)


def _sgu_merge_kernel(gu_ref, gv_ref, ga_ref, gb_ref, oa_ref, g_ref, ws_ref, bs_ref, wo_ref, m_ref, z_scr,
                      *, tm, gc):
    s = pl.program_id(0)

    @pl.when(s == 0)
    def _():
        z_scr[1] = jnp.zeros(z_scr.shape[1:], BF16)

    o_b = _bdot(z_scr[(s + 1) % 2], wo_ref[...])
    for r0 in range(0, tm, CHUNK):
        rows = slice(r0, r0 + CHUNK)
        oa = oa_ref[rows, :].astype(F32)
        ob = o_b[rows, :]
        ta = _half_tanh_half(ga_ref[rows, :].astype(F32))
        tb = _half_tanh_half(gb_ref[rows, :].astype(F32))
        m_ref[rows, :] = (0.5 * ((oa + ob) + (ta * oa + tb * ob))).astype(BF16)

    slot = s % 2
    for r0 in range(0, tm, CHUNK):
        v = _gelu_tanh(gv_ref[r0:r0 + CHUNK, :].astype(F32))
        vn = _rms_rows(v, g_ref[...]).astype(BF16)
        for g in range(GM_GROUPS):
            c0 = g * gc
            mix = _bdot(ws_ref[g], vn[:, c0:c0 + gc]) + bs_ref[:, g:g + 1]
            u = _gelu_tanh(gu_ref[r0:r0 + CHUNK, c0:c0 + gc].astype(F32))
            z_scr[slot, r0:r0 + CHUNK, c0:c0 + gc] = (u * mix).astype(BF16)


def _sgu_merge(big, o_a, g_sgu, w_s, b_s_t, w_o_gm, *, tm):
    t = big.shape[0]
    width = g_sgu.shape[1]
    d = w_o_gm.shape[1]
    gc = width // GM_GROUPS
    nt = t // tm

    def fill(s):
        return jnp.minimum(s, nt - 1)

    def drain(s):
        return jnp.maximum(s - 1, 0)

    return pl.pallas_call(
        functools.partial(_sgu_merge_kernel, tm=tm, gc=gc),
        grid=(nt + 1,),
        in_specs=[
            pl.BlockSpec((tm, width), lambda s: (fill(s), 0)),
            pl.BlockSpec((tm, width), lambda s: (fill(s), 1)),
            pl.BlockSpec((tm, d), lambda s: (drain(s), 2)),
            pl.BlockSpec((tm, d), lambda s: (drain(s), 3)),
            pl.BlockSpec((tm, d), lambda s: (drain(s), 0)),
            _const_spec((1, width)),
            _const_spec(w_s.shape),
            _const_spec(b_s_t.shape),
            _const_spec(w_o_gm.shape),
        ],
        out_specs=pl.BlockSpec((tm, d), lambda s: (drain(s), 0)),
        out_shape=jax.ShapeDtypeStruct((t, d), BF16),
        scratch_shapes=[pltpu.VMEM((2, tm, width), BF16)],
        compiler_params=_params(1),
        name="sgu_merge",
    )(big, big, big, big, o_a, g_sgu, w_s, b_s_t, w_o_gm)


def _outproj_kernel(m_ref, xc_ref, xl_ref, gate_ref, sc_ref, sh_ref, g_ref, wo_ref, wr_ref,
                    x1_ref, h2_ref, lg_ref, *, tm, n_ctx_tiles):
    i = pl.program_id(0)
    r = _bdot(m_ref[...], wo_ref[...])

    def epilogue(x_ref):
        for r0 in range(0, tm, CHUNK):
            rows = slice(r0, r0 + CHUNK)
            x1 = x_ref[rows, :] + gate_ref[...] * r[rows, :]
            x1_ref[rows, :] = x1
            h2 = (_rms_rows(x1, g_ref[...]) * (1.0 + sc_ref[...]) + sh_ref[...]).astype(BF16)
            h2_ref[rows, :] = h2
            lg_ref[rows, :] = _bdot(h2, wr_ref[...])

    @pl.when(i < n_ctx_tiles)
    def _():
        epilogue(xc_ref)

    @pl.when(i >= n_ctx_tiles)
    def _():
        epilogue(xl_ref)


def _out_proj(merged, xc, xl, mod6, g_ffn, w_out, w_router_p, *, n_lat_seq, tm):
    n_ctx, d = xc.shape
    t = n_ctx + xl.shape[0]
    n_ctx_tiles = n_ctx // tm
    tiles_per_seq = n_lat_seq // tm

    def mod_row(i):
        return jnp.where(i < n_ctx_tiles, MOD_ROWS // 2, (i - n_ctx_tiles) // tiles_per_seq)

    return pl.pallas_call(
        functools.partial(_outproj_kernel, tm=tm, n_ctx_tiles=n_ctx_tiles),
        grid=(t // tm,),
        in_specs=[
            pl.BlockSpec((tm, d), lambda i: (i, 0)),
            pl.BlockSpec((tm, d), lambda i: (jnp.minimum(i, n_ctx_tiles - 1), 0)),
            pl.BlockSpec((tm, d), lambda i: (jnp.maximum(i - n_ctx_tiles, 0), 0)),
            pl.BlockSpec((None, None, 1, d), lambda i: (2, mod_row(i), 0, 0)),
            pl.BlockSpec((None, None, 1, d), lambda i: (4, mod_row(i), 0, 0)),
            pl.BlockSpec((None, None, 1, d), lambda i: (3, mod_row(i), 0, 0)),
            _const_spec((1, d)),
            _const_spec(w_out.shape),
            _const_spec(w_router_p.shape),
        ],
        out_specs=[
            pl.BlockSpec((tm, d), lambda i: (i, 0)),
            pl.BlockSpec((tm, d), lambda i: (i, 0)),
            pl.BlockSpec((tm, LOGIT_LANES), lambda i: (i, 0)),
        ],
        out_shape=[
            jax.ShapeDtypeStruct((t, d), F32),
            jax.ShapeDtypeStruct((t, d), BF16),
            jax.ShapeDtypeStruct((t, LOGIT_LANES), F32),
        ],
        compiler_params=_params(1),
        name="out_proj",
    )(merged, xc, xl, mod6, mod6, mod6, g_ffn, w_out, w_router_p)


BISECT_STEPS = 48
MIN_NORMAL_F32 = float(np.finfo(np.float32).tiny)


def _route_kernel(lg_ref, key_ref, aff_ref, tri_scr, *, n_sets, n, cap):
    for r0 in range(0, n, 128):
        r = lax.broadcasted_iota(jnp.int32, (128, n), 0) + r0
        c = lax.broadcasted_iota(jnp.int32, (128, n), 1)
        tri_scr[r0:r0 + 128, :] = jnp.where(r < c, 1.0, 0.0).astype(BF16)

    for s in range(n_sets):
        logits = lg_ref[s * n:(s + 1) * n, :].T[0:N_EXPERTS, :]
        e = jnp.exp(logits - jnp.max(logits, axis=0, keepdims=True))
        aff_ref[s * N_EXPERTS:(s + 1) * N_EXPERTS, :] = e / jnp.sum(e, axis=0, keepdims=True)
    aff = aff_ref[...]
    rows = n_sets * N_EXPERTS

    def count_ge(thr):
        return jnp.sum(jnp.where(aff >= thr, 1.0, 0.0), axis=1, keepdims=True)

    def bisect(_, carry):
        lo, hi = carry
        mid = jnp.sqrt(lo) * jnp.sqrt(hi)
        ok = count_ge(mid) >= cap
        return jnp.where(ok, mid, lo), jnp.where(ok, hi, mid)

    lo0 = jnp.full((rows, 1), MIN_NORMAL_F32, F32)
    hi0 = jnp.full((rows, 1), 2.0, F32)
    lo, hi = lax.fori_loop(0, BISECT_STEPS, bisect, (lo0, hi0))
    lo = jnp.where(count_ge(lo) >= cap, lo, 0.0)

    above = aff >= hi
    band = (aff >= lo) & jnp.logical_not(above)
    need = cap - jnp.sum(jnp.where(above, 1.0, 0.0), axis=1, keepdims=True)
    tri = tri_scr[...]
    band_before = _bdot(jnp.where(band, 1.0, 0.0).astype(BF16), tri)
    sel = above | (band & (band_before < need))
    pos = _bdot(jnp.where(sel, 1.0, 0.0).astype(BF16), tri)
    key_ref[...] = jnp.where(sel, pos, -1.0)


def _route(logits, *, row0, n_sets, n):
    cap = EC_FACTOR * n // N_EXPERTS
    rows = n_sets * N_EXPERTS
    blk = row0 // (n_sets * n)
    return pl.pallas_call(
        functools.partial(_route_kernel, n_sets=n_sets, n=n, cap=cap),
        grid=(1,),
        in_specs=[pl.BlockSpec((n_sets * n, LOGIT_LANES), lambda g: (blk, 0))],
        out_specs=[pl.BlockSpec((rows, n), lambda g: (0, 0)), pl.BlockSpec((rows, n), lambda g: (0, 0))],
        out_shape=[jax.ShapeDtypeStruct((rows, n), F32), jax.ShapeDtypeStruct((rows, n), F32)],
        scratch_shapes=[pltpu.VMEM((n, n), BF16)],
        compiler_params=_params(1),
        name=f"route_{n}",
    )(logits)


def _gather_kernel(key_ref, aff_ref, h2_ref, pt_ref, gate_ref, xg_ref, p_scr, *, n, cap, d, nb):
    key = key_ref[...]
    aff = aff_ref[...]
    slot = lax.broadcasted_iota(jnp.int32, (cap, n), 0).astype(F32)
    per_group = 128 // cap
    for grp in range(N_EXPERTS // per_group):
        pieces = []
        for ex in range(grp * per_group, (grp + 1) * per_group):
            hit = slot == key[ex:ex + 1, :]
            gate_ref[ex] = jnp.sum(jnp.where(hit, aff[ex:ex + 1, :], 0.0), axis=1, keepdims=True)
            pieces.append(jnp.where(hit, 1.0, 0.0))
        hit128 = pieces[0] if per_group == 1 else jnp.concatenate(pieces, axis=0)
        p_scr[grp * 128:(grp + 1) * 128, :] = hit128.astype(BF16)
        pt_ref[:, grp * 128:(grp + 1) * 128] = hit128.T.astype(BF16)

    p = p_scr[...]
    for c in range(d // nb):
        xg = _bdot(p, h2_ref[:, c * nb:(c + 1) * nb]).astype(BF16)
        for ex in range(N_EXPERTS):
            xg_ref[ex, :, c * nb:(c + 1) * nb] = xg[ex * cap:(ex + 1) * cap, :]


def _dispatch_gather(key, aff, h2, *, row0, n_sets, n):
    d = h2.shape[1]
    cap = EC_FACTOR * n // N_EXPERTS
    slots = N_EXPERTS * cap
    blk0 = row0 // n
    return pl.pallas_call(
        functools.partial(_gather_kernel, n=n, cap=cap, d=d, nb=256),
        grid=(n_sets,),
        in_specs=[
            pl.BlockSpec((N_EXPERTS, n), lambda b: (b, 0)),
            pl.BlockSpec((N_EXPERTS, n), lambda b: (b, 0)),
            pl.BlockSpec((n, d), lambda b: (blk0 + b, 0)),
        ],
        out_specs=[
            pl.BlockSpec((None, n, slots), lambda b: (b, 0, 0)),
            pl.BlockSpec((N_EXPERTS, None, cap, 1), lambda b: (0, b, 0, 0)),
            pl.BlockSpec((N_EXPERTS, None, cap, d), lambda b: (0, b, 0, 0)),
        ],
        out_shape=[
            jax.ShapeDtypeStruct((n_sets, n, slots), BF16),
            jax.ShapeDtypeStruct((N_EXPERTS, n_sets, cap, 1), F32),
            jax.ShapeDtypeStruct((N_EXPERTS, n_sets, cap, d), BF16),
        ],
        scratch_shapes=[pltpu.VMEM((slots, n), BF16)],
        compiler_params=_params(1),
        name=f"dispatch_gather_{n}",
    )(key, aff, h2)


def _expert_kernel(xc_ref, xl_ref, gc_ref, gl_ref, w1_ref, w3_ref, w2_ref, yc_ref, yl_ref, hc_scr, hl_scr,
                   *, n_f, fc, rows_c, rows_l, d):
    k = pl.program_id(1)

    @pl.when(k < n_f)
    def _():
        w1 = w1_ref[...].astype(BF16)
        w3 = w3_ref[...].astype(BF16)
        for x_ref, h_scr, rows in ((xc_ref, hc_scr, rows_c), (xl_ref, hl_scr, rows_l)):
            x = x_ref[...].reshape(rows, d)
            a = _bdot(x, w1)
            g = _bdot(x, w3)
            h_scr[k] = (_silu(a) * g).astype(BF16)

    @pl.when(k >= n_f)
    def _():
        w2 = w2_ref[...].astype(BF16)
        for h_scr, g_ref, y_ref, rows in ((hc_scr, gc_ref, yc_ref, rows_c), (hl_scr, gl_ref, yl_ref, rows_l)):
            y = _bdot(h_scr[0], w2[0:fc, :])
            for kk in range(1, n_f):
                y = y + _bdot(h_scr[kk], w2[kk * fc:(kk + 1) * fc, :])
            y_ref[...] = (y * g_ref[...].reshape(rows, 1)).astype(BF16).reshape(y_ref.shape)


def _experts(xg_c, xg_l, gate_c, gate_l, w_e1, w_e3, w_e2, *, fc=256, nc=512):
    n_e, sets_c, cap_c, d = xg_c.shape
    _, sets_l, cap_l, _ = xg_l.shape
    d_ff = w_e1.shape[2]
    n_f = d_ff // fc
    n_c = d // nc
    rows_c, rows_l = sets_c * cap_c, sets_l * cap_l

    def f_idx(k):
        return jnp.minimum(k, n_f - 1)

    def c_idx(k):
        return jnp.maximum(k - n_f, 0)

    kern = functools.partial(_expert_kernel, n_f=n_f, fc=fc, rows_c=rows_c, rows_l=rows_l, d=d)
    return pl.pallas_call(
        kern,
        grid=(n_e, n_f + n_c),
        in_specs=[
            pl.BlockSpec((None, sets_c, cap_c, d), lambda e, k: (e, 0, 0, 0)),
            pl.BlockSpec((None, sets_l, cap_l, d), lambda e, k: (e, 0, 0, 0)),
            pl.BlockSpec((None, sets_c, cap_c, 1), lambda e, k: (e, 0, 0, 0)),
            pl.BlockSpec((None, sets_l, cap_l, 1), lambda e, k: (e, 0, 0, 0)),
            pl.BlockSpec((None, d, fc), lambda e, k: (e, 0, f_idx(k))),
            pl.BlockSpec((None, d, fc), lambda e, k: (e, 0, f_idx(k))),
            pl.BlockSpec((None, d_ff, nc), lambda e, k: (e, 0, c_idx(k))),
        ],
        out_specs=[
            pl.BlockSpec((None, sets_c, cap_c, nc), lambda e, k: (e, 0, 0, c_idx(k))),
            pl.BlockSpec((None, sets_l, cap_l, nc), lambda e, k: (e, 0, 0, c_idx(k))),
        ],
        out_shape=[
            jax.ShapeDtypeStruct(xg_c.shape, BF16),
            jax.ShapeDtypeStruct(xg_l.shape, BF16),
        ],
        scratch_shapes=[pltpu.VMEM((n_f, rows_c, fc), BF16), pltpu.VMEM((n_f, rows_l, fc), BF16)],
        compiler_params=_params(2),
        name="experts",
    )(xg_c, xg_l, gate_c, gate_l, w_e1, w_e3, w_e2)


def _combine_kernel(pt_ref, y_ref, x1_ref, gate_ref, g_ref, o_ref, acc_scr, *, slots, d, nb):
    pt = pt_ref[...]
    for c in range(d // nb):
        y = y_ref[:, :, c * nb:(c + 1) * nb].reshape(slots, nb)
        moe = _bdot(pt, y)
        acc_scr[:, c * nb:(c + 1) * nb] = (x1_ref[:, c * nb:(c + 1) * nb]
                                           + gate_ref[:, c * nb:(c + 1) * nb] * moe)
    o_ref[...] = _rms_rows(acc_scr[...], g_ref[...])


def _combine(pt, y, x1, mod6, g_final, *, row0, n_sets, n, mod_row_fn):
    d = x1.shape[1]
    cap = EC_FACTOR * n // N_EXPERTS
    slots = N_EXPERTS * cap
    tn = min(n, 512)
    nt = n // tn
    blk0 = row0 // tn
    return pl.pallas_call(
        functools.partial(_combine_kernel, slots=slots, d=d, nb=512),
        grid=(n_sets, nt),
        in_specs=[
            pl.BlockSpec((None, tn, slots), lambda b, r: (b, r, 0)),
            pl.BlockSpec((N_EXPERTS, None, cap, d), lambda b, r: (0, b, 0, 0)),
            pl.BlockSpec((tn, d), lambda b, r: (blk0 + b * nt + r, 0)),
            pl.BlockSpec((None, None, 1, d), lambda b, r: (5, mod_row_fn(b), 0, 0)),
            _const_spec((1, d)),
        ],
        out_specs=pl.BlockSpec((tn, d), lambda b, r: (b * nt + r, 0)),
        out_shape=jax.ShapeDtypeStruct((n_sets * n, d), F32),
        scratch_shapes=[pltpu.VMEM((tn, d), F32)],
        compiler_params=_params(2),
        name=f"combine_{n}",
    )(pt, y, x1, mod6, g_final)


def _rope_tables(n):
    tpos = jnp.arange(n, dtype=jnp.int32)
    row = (tpos // GRID_W).astype(F32)
    col = (tpos % GRID_W).astype(F32)
    inv = 1.0 / (ROPE_BASE ** (jnp.arange(ROPE_FREQS, dtype=F32) / ROPE_FREQS))
    ang = jnp.stack([row[:, None] * inv, col[:, None] * inv], axis=1)
    cos = jnp.cos(ang)[:, :, None, :]
    sin = jnp.sin(ang)[:, :, None, :]
    cos = jnp.broadcast_to(cos, (n, 2, 2, ROPE_FREQS)).reshape(n, D_ROPE)
    sin = jnp.concatenate([-sin, sin], axis=2).reshape(n, D_ROPE)
    pad = jnp.zeros((n, 128 - D_ROPE), F32)
    return jnp.concatenate([cos, pad], axis=1), jnp.concatenate([sin, pad], axis=1)


def _swap_rotary_halves(w):
    perm = np.arange(D_ROPE) ^ ROPE_FREQS
    return w[..., perm]


def kernel(x_prompt, x_sample, c, cache_ckv, cache_krope, c_ctx, g_attn, g_ffn, w_ada, b_ada, w_in, g_qa,
           w_qb, g_kv, w_uk, w_uv, w_o_mla, g_sgu, w_s, b_s, w_o_gm, w_out, w_router, w_e1, w_e3, w_e2,
           g_final):
    batch, seq, d = x_prompt.shape
    dec_batch, dec_seq, _ = x_sample.shape
    depth = g_attn.shape[0]
    assert depth == 1
    q_lora, kv_lora = g_qa.shape[1], g_kv.shape[1]
    gm_width = g_sgu.shape[1]
    n_ctx, n_lat = batch * seq, dec_batch * dec_seq
    assert dec_batch < MOD_ROWS // 2 + 1

    xc = x_prompt.reshape(n_ctx, d)
    xl = x_sample.reshape(n_lat, d)

    c_rows = jnp.zeros((MOD_ROWS, d), F32).at[:dec_batch].set(c).at[MOD_ROWS // 2].set(c_ctx)
    mod = _modulation(c_rows, w_ada[0], b_ada[0][None, :])
    mod6 = mod.reshape(MOD_ROWS, 6, 1, d).transpose(1, 0, 2, 3)

    cos_t, sin_t = _rope_tables(dec_seq)

    w_in_t = w_in[0].T
    o_kr = q_lora + kv_lora
    w_kr_t = w_in_t[o_kr:o_kr + D_ROPE]
    zpad = jnp.zeros((128 - D_ROPE, d), F32)
    w_small_t = jnp.concatenate(
        [w_in_t[:o_kr], w_kr_t, zpad, w_kr_t[np.arange(D_ROPE) ^ ROPE_FREQS], zpad], axis=0).astype(BF16)
    tm, bn_in = 512, 1024
    front = -(o_kr + D_ROPE) % bn_in
    w_all_t = _cast_bf16(w_in_t, rows=front, front_blocks=1)
    big, qa_n, ckv, kr = _in_proj(xc, xl, g_attn, mod6, w_small_t, g_qa, g_kv, cos_t, sin_t, w_all_t,
                                  n_big=w_in_t.shape[0] - o_kr - D_ROPE, tm=tm, bn=bn_in)

    wq3 = w_qb[0].reshape(q_lora, N_HEADS, D_NOPE + D_ROPE)
    wq_nope, wq_rope = wq3[:, :, :D_NOPE], wq3[:, :, D_NOPE:]
    zq = jnp.zeros((q_lora, N_HEADS, HEAD_SLOT - D_NOPE - D_ROPE), F32)
    wq_p = jnp.concatenate([wq_nope, wq_rope, zq], axis=2).reshape(q_lora, N_HEADS * HEAD_SLOT).astype(BF16)
    wq_sw = jnp.concatenate([_swap_rotary_halves(wq_rope), zq], axis=2).reshape(
        q_lora, N_HEADS * 128).astype(BF16)
    q_pad = _q_proj(qa_n, wq_p, wq_sw, cos_t, sin_t, n_ctx=n_ctx, n_lat_seq=dec_seq, tm=tm)

    w_uk2 = w_uk[0].reshape(kv_lora, N_HEADS // 2, 2 * D_NOPE).transpose(1, 0, 2).astype(BF16)
    w_uv2 = w_uv[0].reshape(kv_lora, N_HEADS // 2, 2 * D_V).transpose(1, 0, 2).astype(BF16)
    o_ctx = _attention(q_pad, ckv, kr, w_uk2, w_uv2, None, None,
                       row0=0, n_req=batch, n_own=seq, tq=seq, heads_per_iter=N_HEADS)
    o_lat = _attention(q_pad, ckv, kr, w_uk2, w_uv2, cache_ckv[:, 0], cache_krope[:, 0],
                       row0=n_ctx, n_req=dec_batch, n_own=dec_seq, tq=512, heads_per_iter=2)

    o_a = _o_proj(o_ctx, o_lat, w_o_mla[0].astype(BF16), tm=1024)
    merged = _sgu_merge(big, o_a, g_sgu, w_s[0].astype(BF16), b_s[0].T, w_o_gm[0].astype(BF16), tm=tm)

    w_router_p = jnp.concatenate(
        [w_router[0], jnp.zeros((d, LOGIT_LANES - N_EXPERTS), F32)], axis=1).astype(BF16)
    x1, h2, logits = _out_proj(merged, xc, xl, mod6, g_ffn, w_out[0].astype(BF16), w_router_p,
                               n_lat_seq=dec_seq, tm=tm)

    key_c, aff_c = _route(logits, row0=0, n_sets=batch, n=seq)
    key_l, aff_l = _route(logits, row0=n_ctx, n_sets=dec_batch, n=dec_seq)
    p_c, gate_c, xg_c = _dispatch_gather(key_c, aff_c, h2, row0=0, n_sets=batch, n=seq)
    p_l, gate_l, xg_l = _dispatch_gather(key_l, aff_l, h2, row0=n_ctx, n_sets=dec_batch, n=dec_seq)

    y_c, y_l = _experts(xg_c, xg_l, gate_c, gate_l, w_e1[0], w_e3[0], w_e2[0])

    y_prompt = _combine(p_c, y_c, x1, mod6, g_final[None, :], row0=0, n_sets=batch, n=seq,
                        mod_row_fn=lambda b: MOD_ROWS // 2)
    y_sample = _combine(p_l, y_l, x1, mod6, g_final[None, :], row0=n_ctx, n_sets=dec_batch, n=dec_seq,
                        mod_row_fn=lambda b: b)

    new_ckv = ckv[:n_ctx].reshape(batch, 1, seq, kv_lora)
    new_krope = kr[:n_ctx, :D_ROPE].reshape(batch, 1, seq, D_ROPE)
    return (y_prompt.reshape(batch, seq, d), y_sample.reshape(dec_batch, dec_seq, d), new_ckv, new_krope)
```

```python
import functools

import numpy as np
import jax
import jax.numpy as jnp
from jax import lax
from jax.experimental import pallas as pl
from jax.experimental.pallas import tpu as pltpu

F32 = jnp.float32
BF16 = jnp.bfloat16

N_HEADS = 16
D_NOPE = 128
D_ROPE = 64
D_V = 128
HEAD_SLOT = 256
ROPE_FREQS = D_ROPE // 4
ROPE_BASE = 10000.0
GRID_W = 64
CHUNK = 128
GM_GROUPS = 8
N_EXPERTS = 16
EC_FACTOR = 2
EPS = 1e-6
MOD_ROWS = 8
LOGIT_LANES = 128

VMEM_LIMIT_V7X = 56 * 1024 * 1024


def _params(n_axes):
    return pltpu.CompilerParams(
        dimension_semantics=("arbitrary",) * n_axes, vmem_limit_bytes=VMEM_LIMIT_V7X)


def _const_spec(shape):
    nd = len(shape)
    return pl.BlockSpec(shape, lambda *_: (0,) * nd, pipeline_mode=pl.Buffered(1))


def _half_tanh_half(x):
    return jnp.tanh(0.5 * x)


def _silu(x):
    h = 0.5 * x
    return h * jnp.tanh(h) + h


def _gelu_tanh(x):
    c = np.float32(np.sqrt(2.0 / np.pi))
    return x * (0.5 * (1.0 + jnp.tanh(c * (x + np.float32(0.044715) * (x * x * x)))))


def _rms_rows(x, g):
    return x * lax.rsqrt(jnp.mean(x * x, axis=-1, keepdims=True) + EPS) * g


def _bdot(a, b):
    return jnp.dot(a, b, preferred_element_type=F32)


def _bdot_t(a, bt):
    return lax.dot_general(a, bt, (((1,), (1,)), ((), ())), preferred_element_type=F32)


def _mod_kernel(c_ref, w_ref, b_ref, o_ref):
    c = c_ref[...]
    s = _silu(c).astype(BF16)
    o_ref[...] = _bdot(s, w_ref[...].astype(BF16)) + b_ref[...]


def _modulation(c_rows, w_ada, b_ada, bn=1024):
    d, n = w_ada.shape
    return pl.pallas_call(
        _mod_kernel,
        grid=(n // bn,),
        in_specs=[
            _const_spec((MOD_ROWS, d)),
            pl.BlockSpec((d, bn), lambda j: (0, j)),
            pl.BlockSpec((1, bn), lambda j: (0, j)),
        ],
        out_specs=pl.BlockSpec((MOD_ROWS, bn), lambda j: (0, j)),
        out_shape=jax.ShapeDtypeStruct((MOD_ROWS, n), F32),
        compiler_params=_params(1),
        name="adaln_mod",
    )(c_rows, w_ada, b_ada)


def _cast_kernel(x_ref, o_ref, *, front_blocks):
    i = pl.program_id(0)

    @pl.when(i < front_blocks)
    def _():
        o_ref[...] = jnp.zeros(o_ref.shape, o_ref.dtype)

    @pl.when(i >= front_blocks)
    def _():
        o_ref[...] = x_ref[...].astype(o_ref.dtype)


def _cast_bf16(x, *, rows, front_blocks=0):
    r, c = x.shape
    assert r % rows == 0 and rows % 16 == 0
    nb = r // rows + front_blocks
    return pl.pallas_call(
        functools.partial(_cast_kernel, front_blocks=front_blocks),
        grid=(nb,),
        in_specs=[pl.BlockSpec((rows, c), lambda i: (jnp.maximum(i - front_blocks, 0), 0))],
        out_specs=pl.BlockSpec((rows, c), lambda i: (i, 0)),
        out_shape=jax.ShapeDtypeStruct((nb * rows, c), BF16),
        compiler_params=_params(1),
        name="cast_bf16",
    )(x)


def _preproj_kernel(xc_ref, xl_ref, g_ref, sc_ref, sh_ref, ws_ref, gqa_ref, gkv_ref, cos_ref, sin_ref,
                    wq_ref, wsw_ref, h_ref, ckv_ref, kr_ref, q_ref, *, n_ctx_tiles, q_lora, kv_lora, scale):
    i = pl.program_id(0)

    def body(x_ref, rotary):
        h = _rms_rows(x_ref[...], g_ref[...]) * (1.0 + sc_ref[...]) + sh_ref[...]
        hb = h.astype(BF16)
        h_ref[...] = hb
        small = _bdot_t(hb, ws_ref[...])
        qa = _rms_rows(small[:, :q_lora], gqa_ref[...]).astype(BF16)
        ckv_ref[...] = _rms_rows(small[:, q_lora:q_lora + kv_lora], gkv_ref[...])
        o = q_lora + kv_lora
        kr = small[:, o:o + 128]
        if not rotary:
            kr_ref[...] = kr
            for hd in range(N_HEADS):
                q = _bdot(qa, wq_ref[:, hd * HEAD_SLOT:(hd + 1) * HEAD_SLOT])
                q_ref[hd] = (q * scale).astype(BF16)
            return
        cos = cos_ref[...]
        sin = sin_ref[...]
        kr_sw = small[:, o + 128:o + 256]
        kr_ref[...] = kr * cos + kr_sw * sin
        for hp in range(N_HEADS // 2):
            q_sw = _bdot(qa, wsw_ref[:, hp * 256:(hp + 1) * 256])
            for s in range(2):
                hd = 2 * hp + s
                q = _bdot(qa, wq_ref[:, hd * HEAD_SLOT:(hd + 1) * HEAD_SLOT])
                q_ref[hd, :, 0:128] = (q[:, 0:128] * scale).astype(BF16)
                rot = q[:, 128:256] * cos + q_sw[:, s * 128:(s + 1) * 128] * sin
                q_ref[hd, :, 128:256] = (rot * scale).astype(BF16)

    @pl.when(i < n_ctx_tiles)
    def _():
        body(xc_ref, False)

    @pl.when(i >= n_ctx_tiles)
    def _():
        body(xl_ref, True)


def _pre_proj(xc, xl, g_attn, mod6, w_small_t, g_qa, g_kv, cos_t, sin_t, wq_p, wq_sw, *, tm):
    n_ctx, d = xc.shape
    n_lat_seq = cos_t.shape[0]
    t = n_ctx + xl.shape[0]
    q_lora, kv_lora = g_qa.shape[1], g_kv.shape[1]
    n_ctx_tiles = n_ctx // tm
    tiles_per_seq = n_lat_seq // tm

    def mod_row(i):
        return jnp.where(i < n_ctx_tiles, MOD_ROWS // 2, (i - n_ctx_tiles) // tiles_per_seq)

    def rope_blk(i):
        return jnp.maximum(i - n_ctx_tiles, 0) % tiles_per_seq

    scale = np.float32((D_NOPE + D_ROPE) ** -0.5)
    kern = functools.partial(_preproj_kernel, n_ctx_tiles=n_ctx_tiles, q_lora=q_lora, kv_lora=kv_lora,
                             scale=scale)
    return pl.pallas_call(
        kern,
        grid=(t // tm,),
        in_specs=[
            pl.BlockSpec((tm, d), lambda i: (jnp.minimum(i, n_ctx_tiles - 1), 0)),
            pl.BlockSpec((tm, d), lambda i: (jnp.maximum(i - n_ctx_tiles, 0), 0)),
            _const_spec((1, d)),
            pl.BlockSpec((None, None, 1, d), lambda i: (1, mod_row(i), 0, 0)),
            pl.BlockSpec((None, None, 1, d), lambda i: (0, mod_row(i), 0, 0)),
            _const_spec(w_small_t.shape),
            _const_spec((1, q_lora)),
            _const_spec((1, kv_lora)),
            pl.BlockSpec((tm, 128), lambda i: (rope_blk(i), 0)),
            pl.BlockSpec((tm, 128), lambda i: (rope_blk(i), 0)),
            _const_spec(wq_p.shape),
            _const_spec(wq_sw.shape),
        ],
        out_specs=[
            pl.BlockSpec((tm, d), lambda i: (i, 0)),
            pl.BlockSpec((tm, kv_lora), lambda i: (i, 0)),
            pl.BlockSpec((tm, 128), lambda i: (i, 0)),
            pl.BlockSpec((N_HEADS, tm, HEAD_SLOT), lambda i: (0, i, 0)),
        ],
        out_shape=[
            jax.ShapeDtypeStruct((t, d), BF16),
            jax.ShapeDtypeStruct((t, kv_lora), F32),
            jax.ShapeDtypeStruct((t, 128), F32),
            jax.ShapeDtypeStruct((N_HEADS, t, HEAD_SLOT), BF16),
        ],
        compiler_params=_params(1),
        name="pre_proj",
    )(xc, xl, g_attn, mod6, mod6, w_small_t, g_qa, g_kv, cos_t, sin_t, wq_p, wq_sw)


def _wideproj_kernel(h_ref, wa_ref, wb_ref, o_ref, *, bn):
    h = h_ref[...]
    o_ref[:, 0:bn] = _bdot_t(h, wa_ref[...]).astype(BF16)
    o_ref[:, bn:2 * bn] = _bdot_t(h, wb_ref[...]).astype(BF16)


def _wide_proj(h, w_all_t, *, n_big, tm, bn):
    t, d = h.shape
    blk0 = (w_all_t.shape[0] - n_big) // bn
    return pl.pallas_call(
        functools.partial(_wideproj_kernel, bn=bn),
        grid=(n_big // (2 * bn), t // tm),
        in_specs=[
            pl.BlockSpec((tm, d), lambda j, i: (i, 0)),
            pl.BlockSpec((bn, d), lambda j, i: (blk0 + 2 * j, 0)),
            pl.BlockSpec((bn, d), lambda j, i: (blk0 + 2 * j + 1, 0)),
        ],
        out_specs=pl.BlockSpec((tm, 2 * bn), lambda j, i: (i, j)),
        out_shape=jax.ShapeDtypeStruct((t, n_big), BF16),
        compiler_params=_params(2),
        name="wide_proj",
    )(h, w_all_t, w_all_t)


def _attn_kernel(*refs, n_own, n_cache, heads_per_iter):
    if n_cache:
        (q_ref, ckv_ref, kr_ref, cckv_ref, ckr_ref, wuk_ref, wuv_ref,
         o_ref, kpad, vexp, kall, krp, o_scr) = refs
    else:
        (q_ref, ckv_ref, kr_ref, wuk_ref, wuv_ref, o_ref, kpad, vexp, kall, krp, o_scr) = refs
    qi = pl.program_id(1)

    @pl.when(qi == 0)
    def _():
        kall[0:n_own, :] = ckv_ref[...].astype(BF16)
        krp[0:n_own, :] = kr_ref[...].astype(BF16)
        if n_cache:
            kall[n_own:n_own + n_cache, :] = cckv_ref[...].astype(BF16)
            krp[n_own:n_own + n_cache, 0:D_ROPE] = ckr_ref[...].astype(BF16)
            krp[n_own:n_own + n_cache, D_ROPE:128] = jnp.zeros((n_cache, 128 - D_ROPE), BF16)

        def expand(hp, carry):
            kn = _bdot(kall[...], wuk_ref[hp]).astype(BF16)
            vv = _bdot(kall[...], wuv_ref[hp]).astype(BF16)
            for s in range(2):
                kpad[2 * hp + s, :, 0:128] = kn[:, s * 128:(s + 1) * 128]
                kpad[2 * hp + s, :, 128:256] = krp[...]
                vexp[2 * hp + s] = vv[:, s * 128:(s + 1) * 128]
            return carry

        if heads_per_iter == N_HEADS:
            for hp in range(N_HEADS // 2):
                expand(hp, 0)
        else:
            lax.fori_loop(0, N_HEADS // 2, expand, 0)

    def one_head(h):
        s = _bdot_t(q_ref[h], kpad[h])
        p = jnp.exp(s - jnp.max(s, axis=-1, keepdims=True))
        l = jnp.sum(p, axis=-1, keepdims=True)
        o = _bdot(p.astype(BF16), vexp[h])
        return (o / l).astype(BF16)

    if heads_per_iter == N_HEADS:
        for h in range(N_HEADS):
            o_ref[:, h * D_V:(h + 1) * D_V] = one_head(h)
    else:
        def head_group(hg, carry):
            for g in range(heads_per_iter):
                h = hg * heads_per_iter + g
                o_scr[h] = one_head(h)
            return carry

        lax.fori_loop(0, N_HEADS // heads_per_iter, head_group, 0)
        for h in range(N_HEADS):
            o_ref[:, h * D_V:(h + 1) * D_V] = o_scr[h]


def _attention(q_pad, ckv, kr, w_uk, w_uv, cache_ckv, cache_kr, *, row0, n_req, n_own, tq, heads_per_iter):
    kv_lora = ckv.shape[1]
    n_cache = 0 if cache_ckv is None else cache_ckv.shape[1]
    kn = n_own + n_cache
    qb = n_own // tq
    in_specs = [
        pl.BlockSpec((N_HEADS, tq, HEAD_SLOT), lambda b, qi: (0, row0 // tq + b * qb + qi, 0)),
        pl.BlockSpec((n_own, kv_lora), lambda b, qi: (row0 // n_own + b, 0)),
        pl.BlockSpec((n_own, 128), lambda b, qi: (row0 // n_own + b, 0)),
    ]
    args = [q_pad, ckv, kr]
    if n_cache:
        in_specs += [
            pl.BlockSpec((None, n_cache, kv_lora), lambda b, qi: (b, 0, 0)),
            pl.BlockSpec((None, n_cache, D_ROPE), lambda b, qi: (b, 0, 0)),
        ]
        args += [cache_ckv, cache_kr]
    in_specs += [_const_spec(w_uk.shape), _const_spec(w_uv.shape)]
    args += [w_uk, w_uv]
    return pl.pallas_call(
        functools.partial(_attn_kernel, n_own=n_own, n_cache=n_cache, heads_per_iter=heads_per_iter),
        grid=(n_req, qb),
        in_specs=in_specs,
        out_specs=pl.BlockSpec((tq, N_HEADS * D_V), lambda b, qi: (b * qb + qi, 0)),
        out_shape=jax.ShapeDtypeStruct((n_req * n_own, N_HEADS * D_V), BF16),
        scratch_shapes=[
            pltpu.VMEM((N_HEADS, kn, HEAD_SLOT), BF16),
            pltpu.VMEM((N_HEADS, kn, D_V), BF16),
            pltpu.VMEM((kn, kv_lora), BF16),
            pltpu.VMEM((kn, 128), BF16),
            pltpu.VMEM((N_HEADS, tq, D_V), BF16),
        ],
        compiler_params=_params(2),
        name="mla_attn_cache" if n_cache else "mla_attn",
    )(*args)


def _oproj_kernel(xc_ref, xl_ref, w_ref, o_ref, *, n_ctx_tiles):
    i = pl.program_id(0)

    @pl.when(i < n_ctx_tiles)
    def _():
        o_ref[...] = _bdot(xc_ref[...], w_ref[...]).astype(o_ref.dtype)

    @pl.when(i >= n_ctx_tiles)
    def _():
        o_ref[...] = _bdot(xl_ref[...], w_ref[...]).astype(o_ref.dtype)


def _o_proj(xc, xl, w, *, tm):
    n_ctx, k = xc.shape
    t = n_ctx + xl.shape[0]
    n = w.shape[1]
    n_ctx_tiles = n_ctx // tm
    return pl.pallas_call(
        functools.partial(_oproj_kernel, n_ctx_tiles=n_ctx_tiles),
        grid=(t // tm,),
        in_specs=[
            pl.BlockSpec((tm, k), lambda i: (jnp.minimum(i, n_ctx_tiles - 1), 0)),
            pl.BlockSpec((tm, k), lambda i: (jnp.maximum(i - n_ctx_tiles, 0), 0)),
            _const_spec(w.shape),
        ],
        out_specs=pl.BlockSpec((tm, n), lambda i: (i, 0)),
        out_shape=jax.ShapeDtypeStruct((t, n), BF16),
        compiler_params=_params(1),
        name="o_proj",
    )(xc, xl, w)


def _sgu_merge_kernel(gu_ref, gv_ref, ga_ref, gb_ref, oa_ref, g_ref, ws_ref, bs_ref, wo_ref, m_ref, z_scr,
                      *, tm, gc):
    s = pl.program_id(0)

    @pl.when(s == 0)
    def _():
        z_scr[1] = jnp.zeros(z_scr.shape[1:], BF16)

    o_b = _bdot(z_scr[(s + 1) % 2], wo_ref[...])
    for r0 in range(0, tm, CHUNK):
        rows = slice(r0, r0 + CHUNK)
        oa = oa_ref[rows, :].astype(F32)
        ob = o_b[rows, :]
        ta = _half_tanh_half(ga_ref[rows, :].astype(F32))
        tb = _half_tanh_half(gb_ref[rows, :].astype(F32))
        m_ref[rows, :] = (0.5 * ((oa + ob) + (ta * oa + tb * ob))).astype(BF16)

    slot = s % 2
    for r0 in range(0, tm, CHUNK):
        v = _gelu_tanh(gv_ref[r0:r0 + CHUNK, :].astype(F32))
        vn = _rms_rows(v, g_ref[...]).astype(BF16)
        for g in range(GM_GROUPS):
            c0 = g * gc
            mix = _bdot(ws_ref[g], vn[:, c0:c0 + gc]) + bs_ref[:, g:g + 1]
            u = _gelu_tanh(gu_ref[r0:r0 + CHUNK, c0:c0 + gc].astype(F32))
            z_scr[slot, r0:r0 + CHUNK, c0:c0 + gc] = (u * mix).astype(BF16)


def _sgu_merge(big, o_a, g_sgu, w_s, b_s_t, w_o_gm, *, tm):
    t = big.shape[0]
    width = g_sgu.shape[1]
    d = w_o_gm.shape[1]
    gc = width // GM_GROUPS
    nt = t // tm

    def fill(s):
        return jnp.minimum(s, nt - 1)

    def drain(s):
        return jnp.maximum(s - 1, 0)

    return pl.pallas_call(
        functools.partial(_sgu_merge_kernel, tm=tm, gc=gc),
        grid=(nt + 1,),
        in_specs=[
            pl.BlockSpec((tm, width), lambda s: (fill(s), 0)),
            pl.BlockSpec((tm, width), lambda s: (fill(s), 1)),
            pl.BlockSpec((tm, d), lambda s: (drain(s), 2)),
            pl.BlockSpec((tm, d), lambda s: (drain(s), 3)),
            pl.BlockSpec((tm, d), lambda s: (drain(s), 0)),
            _const_spec((1, width)),
            _const_spec(w_s.shape),
            _const_spec(b_s_t.shape),
            _const_spec(w_o_gm.shape),
        ],
        out_specs=pl.BlockSpec((tm, d), lambda s: (drain(s), 0)),
        out_shape=jax.ShapeDtypeStruct((t, d), BF16),
        scratch_shapes=[pltpu.VMEM((2, tm, width), BF16)],
        compiler_params=_params(1),
        name="sgu_merge",
    )(big, big, big, big, o_a, g_sgu, w_s, b_s_t, w_o_gm)


def _outproj_kernel(m_ref, xc_ref, xl_ref, gate_ref, sc_ref, sh_ref, g_ref, wo_ref, wr_ref,
                    x1_ref, h2_ref, lg_ref, *, tm, n_ctx_tiles):
    i = pl.program_id(0)
    r = _bdot(m_ref[...], wo_ref[...])

    def epilogue(x_ref):
        for r0 in range(0, tm, CHUNK):
            rows = slice(r0, r0 + CHUNK)
            x1 = x_ref[rows, :] + gate_ref[...] * r[rows, :]
            x1_ref[rows, :] = x1
            h2 = (_rms_rows(x1, g_ref[...]) * (1.0 + sc_ref[...]) + sh_ref[...]).astype(BF16)
            h2_ref[rows, :] = h2
            lg_ref[rows, :] = _bdot(h2, wr_ref[...])

    @pl.when(i < n_ctx_tiles)
    def _():
        epilogue(xc_ref)

    @pl.when(i >= n_ctx_tiles)
    def _():
        epilogue(xl_ref)


def _out_proj(merged, xc, xl, mod6, g_ffn, w_out, w_router_p, *, n_lat_seq, tm):
    n_ctx, d = xc.shape
    t = n_ctx + xl.shape[0]
    n_ctx_tiles = n_ctx // tm
    tiles_per_seq = n_lat_seq // tm

    def mod_row(i):
        return jnp.where(i < n_ctx_tiles, MOD_ROWS // 2, (i - n_ctx_tiles) // tiles_per_seq)

    return pl.pallas_call(
        functools.partial(_outproj_kernel, tm=tm, n_ctx_tiles=n_ctx_tiles),
        grid=(t // tm,),
        in_specs=[
            pl.BlockSpec((tm, d), lambda i: (i, 0)),
            pl.BlockSpec((tm, d), lambda i: (jnp.minimum(i, n_ctx_tiles - 1), 0)),
            pl.BlockSpec((tm, d), lambda i: (jnp.maximum(i - n_ctx_tiles, 0), 0)),
            pl.BlockSpec((None, None, 1, d), lambda i: (2, mod_row(i), 0, 0)),
            pl.BlockSpec((None, None, 1, d), lambda i: (4, mod_row(i), 0, 0)),
            pl.BlockSpec((None, None, 1, d), lambda i: (3, mod_row(i), 0, 0)),
            _const_spec((1, d)),
            _const_spec(w_out.shape),
            _const_spec(w_router_p.shape),
        ],
        out_specs=[
            pl.BlockSpec((tm, d), lambda i: (i, 0)),
            pl.BlockSpec((tm, d), lambda i: (i, 0)),
            pl.BlockSpec((tm, LOGIT_LANES), lambda i: (i, 0)),
        ],
        out_shape=[
            jax.ShapeDtypeStruct((t, d), F32),
            jax.ShapeDtypeStruct((t, d), BF16),
            jax.ShapeDtypeStruct((t, LOGIT_LANES), F32),
        ],
        compiler_params=_params(1),
        name="out_proj",
    )(merged, xc, xl, mod6, mod6, mod6, g_ffn, w_out, w_router_p)


BISECT_STEPS = 48
MIN_NORMAL_F32 = float(np.finfo(np.float32).tiny)


def _route_kernel(lg_ref, key_ref, aff_ref, tri_scr, *, n_sets, n, cap):
    for r0 in range(0, n, 128):
        r = lax.broadcasted_iota(jnp.int32, (128, n), 0) + r0
        c = lax.broadcasted_iota(jnp.int32, (128, n), 1)
        tri_scr[r0:r0 + 128, :] = jnp.where(r < c, 1.0, 0.0).astype(BF16)

    for s in range(n_sets):
        logits = lg_ref[s * n:(s + 1) * n, :].T[0:N_EXPERTS, :]
        e = jnp.exp(logits - jnp.max(logits, axis=0, keepdims=True))
        aff_ref[s * N_EXPERTS:(s + 1) * N_EXPERTS, :] = e / jnp.sum(e, axis=0, keepdims=True)
    aff = aff_ref[...]
    rows = n_sets * N_EXPERTS

    def count_ge(thr):
        return jnp.sum(jnp.where(aff >= thr, 1.0, 0.0), axis=1, keepdims=True)

    def bisect(_, carry):
        lo, hi = carry
        mid = jnp.sqrt(lo) * jnp.sqrt(hi)
        ok = count_ge(mid) >= cap
        return jnp.where(ok, mid, lo), jnp.where(ok, hi, mid)

    lo0 = jnp.full((rows, 1), MIN_NORMAL_F32, F32)
    hi0 = jnp.full((rows, 1), 2.0, F32)
    lo, hi = lax.fori_loop(0, BISECT_STEPS, bisect, (lo0, hi0))
    lo = jnp.where(count_ge(lo) >= cap, lo, 0.0)

    above = aff >= hi
    band = (aff >= lo) & jnp.logical_not(above)
    need = cap - jnp.sum(jnp.where(above, 1.0, 0.0), axis=1, keepdims=True)
    tri = tri_scr[...]
    band_before = _bdot(jnp.where(band, 1.0, 0.0).astype(BF16), tri)
    sel = above | (band & (band_before < need))
    pos = _bdot(jnp.where(sel, 1.0, 0.0).astype(BF16), tri)
    key_ref[...] = jnp.where(sel, pos, -1.0)


def _route(logits, *, row0, n_sets, n):
    cap = EC_FACTOR * n // N_EXPERTS
    rows = n_sets * N_EXPERTS
    blk = row0 // (n_sets * n)
    return pl.pallas_call(
        functools.partial(_route_kernel, n_sets=n_sets, n=n, cap=cap),
        grid=(1,),
        in_specs=[pl.BlockSpec((n_sets * n, LOGIT_LANES), lambda g: (blk, 0))],
        out_specs=[pl.BlockSpec((rows, n), lambda g: (0, 0)), pl.BlockSpec((rows, n), lambda g: (0, 0))],
        out_shape=[jax.ShapeDtypeStruct((rows, n), F32), jax.ShapeDtypeStruct((rows, n), F32)],
        scratch_shapes=[pltpu.VMEM((n, n), BF16)],
        compiler_params=_params(1),
        name=f"route_{n}",
    )(logits)


def _gather_kernel(key_ref, aff_ref, h2_ref, pt_ref, gate_ref, xg_ref, p_scr, *, n, cap, d, nb):
    key = key_ref[...]
    aff = aff_ref[...]
    slot = lax.broadcasted_iota(jnp.int32, (cap, n), 0).astype(F32)
    per_group = 128 // cap
    for grp in range(N_EXPERTS // per_group):
        pieces = []
        for ex in range(grp * per_group, (grp + 1) * per_group):
            hit = slot == key[ex:ex + 1, :]
            gate_ref[ex] = jnp.sum(jnp.where(hit, aff[ex:ex + 1, :], 0.0), axis=1, keepdims=True)
            pieces.append(jnp.where(hit, 1.0, 0.0))
        hit128 = pieces[0] if per_group == 1 else jnp.concatenate(pieces, axis=0)
        p_scr[grp * 128:(grp + 1) * 128, :] = hit128.astype(BF16)
        pt_ref[:, grp * 128:(grp + 1) * 128] = hit128.T.astype(BF16)

    p = p_scr[...]
    for c in range(d // nb):
        xg = _bdot(p, h2_ref[:, c * nb:(c + 1) * nb]).astype(BF16)
        for ex in range(N_EXPERTS):
            xg_ref[ex, :, c * nb:(c + 1) * nb] = xg[ex * cap:(ex + 1) * cap, :]


def _dispatch_gather(key, aff, h2, *, row0, n_sets, n):
    d = h2.shape[1]
    cap = EC_FACTOR * n // N_EXPERTS
    slots = N_EXPERTS * cap
    blk0 = row0 // n
    return pl.pallas_call(
        functools.partial(_gather_kernel, n=n, cap=cap, d=d, nb=256),
        grid=(n_sets,),
        in_specs=[
            pl.BlockSpec((N_EXPERTS, n), lambda b: (b, 0)),
            pl.BlockSpec((N_EXPERTS, n), lambda b: (b, 0)),
            pl.BlockSpec((n, d), lambda b: (blk0 + b, 0)),
        ],
        out_specs=[
            pl.BlockSpec((None, n, slots), lambda b: (b, 0, 0)),
            pl.BlockSpec((N_EXPERTS, None, cap, 1), lambda b: (0, b, 0, 0)),
            pl.BlockSpec((N_EXPERTS, None, cap, d), lambda b: (0, b, 0, 0)),
        ],
        out_shape=[
            jax.ShapeDtypeStruct((n_sets, n, slots), BF16),
            jax.ShapeDtypeStruct((N_EXPERTS, n_sets, cap, 1), F32),
            jax.ShapeDtypeStruct((N_EXPERTS, n_sets, cap, d), BF16),
        ],
        scratch_shapes=[pltpu.VMEM((slots, n), BF16)],
        compiler_params=_params(1),
        name=f"dispatch_gather_{n}",
    )(key, aff, h2)


def _expert_kernel(xc_ref, xl_ref, gc_ref, gl_ref, w1_ref, w3_ref, w2_ref, yc_ref, yl_ref, hc_scr, hl_scr,
                   *, n_f, fc, rows_c, rows_l, d):
    k = pl.program_id(1)

    @pl.when(k < n_f)
    def _():
        w1 = w1_ref[...].astype(BF16)
        w3 = w3_ref[...].astype(BF16)
        for x_ref, h_scr, rows in ((xc_ref, hc_scr, rows_c), (xl_ref, hl_scr, rows_l)):
            x = x_ref[...].reshape(rows, d)
            a = _bdot(x, w1)
            g = _bdot(x, w3)
            h_scr[k] = (_silu(a) * g).astype(BF16)

    @pl.when(k >= n_f)
    def _():
        w2 = w2_ref[...].astype(BF16)
        for h_scr, g_ref, y_ref, rows in ((hc_scr, gc_ref, yc_ref, rows_c), (hl_scr, gl_ref, yl_ref, rows_l)):
            y = _bdot(h_scr[0], w2[0:fc, :])
            for kk in range(1, n_f):
                y = y + _bdot(h_scr[kk], w2[kk * fc:(kk + 1) * fc, :])
            y_ref[...] = (y * g_ref[...].reshape(rows, 1)).astype(BF16).reshape(y_ref.shape)


def _experts(xg_c, xg_l, gate_c, gate_l, w_e1, w_e3, w_e2, *, fc=512, nc=512):
    n_e, sets_c, cap_c, d = xg_c.shape
    _, sets_l, cap_l, _ = xg_l.shape
    d_ff = w_e1.shape[2]
    n_f = d_ff // fc
    n_c = d // nc
    rows_c, rows_l = sets_c * cap_c, sets_l * cap_l

    def f_idx(k):
        return jnp.minimum(k, n_f - 1)

    def c_idx(k):
        return jnp.maximum(k - n_f, 0)

    kern = functools.partial(_expert_kernel, n_f=n_f, fc=fc, rows_c=rows_c, rows_l=rows_l, d=d)
    return pl.pallas_call(
        kern,
        grid=(n_e, n_f + n_c),
        in_specs=[
            pl.BlockSpec((None, sets_c, cap_c, d), lambda e, k: (e, 0, 0, 0)),
            pl.BlockSpec((None, sets_l, cap_l, d), lambda e, k: (e, 0, 0, 0)),
            pl.BlockSpec((None, sets_c, cap_c, 1), lambda e, k: (e, 0, 0, 0)),
            pl.BlockSpec((None, sets_l, cap_l, 1), lambda e, k: (e, 0, 0, 0)),
            pl.BlockSpec((None, d, fc), lambda e, k: (e, 0, f_idx(k))),
            pl.BlockSpec((None, d, fc), lambda e, k: (e, 0, f_idx(k))),
            pl.BlockSpec((None, d_ff, nc), lambda e, k: (e, 0, c_idx(k))),
        ],
        out_specs=[
            pl.BlockSpec((None, sets_c, cap_c, nc), lambda e, k: (e, 0, 0, c_idx(k))),
            pl.BlockSpec((None, sets_l, cap_l, nc), lambda e, k: (e, 0, 0, c_idx(k))),
        ],
        out_shape=[
            jax.ShapeDtypeStruct(xg_c.shape, BF16),
            jax.ShapeDtypeStruct(xg_l.shape, BF16),
        ],
        scratch_shapes=[pltpu.VMEM((n_f, rows_c, fc), BF16), pltpu.VMEM((n_f, rows_l, fc), BF16)],
        compiler_params=_params(2),
        name="experts",
    )(xg_c, xg_l, gate_c, gate_l, w_e1, w_e3, w_e2)


def _combine_kernel(pt_ref, y_ref, x1_ref, gate_ref, g_ref, o_ref, acc_scr, *, slots, d, nb):
    pt = pt_ref[...]
    for c in range(d // nb):
        y = y_ref[:, :, c * nb:(c + 1) * nb].reshape(slots, nb)
        moe = _bdot(pt, y)
        acc_scr[:, c * nb:(c + 1) * nb] = (x1_ref[:, c * nb:(c + 1) * nb]
                                           + gate_ref[:, c * nb:(c + 1) * nb] * moe)
    o_ref[...] = _rms_rows(acc_scr[...], g_ref[...])


def _combine(pt, y, x1, mod6, g_final, *, row0, n_sets, n, mod_row_fn):
    d = x1.shape[1]
    cap = EC_FACTOR * n // N_EXPERTS
    slots = N_EXPERTS * cap
    tn = min(n, 512)
    nt = n // tn
    blk0 = row0 // tn
    return pl.pallas_call(
        functools.partial(_combine_kernel, slots=slots, d=d, nb=512),
        grid=(n_sets, nt),
        in_specs=[
            pl.BlockSpec((None, tn, slots), lambda b, r: (b, r, 0)),
            pl.BlockSpec((N_EXPERTS, None, cap, d), lambda b, r: (0, b, 0, 0)),
            pl.BlockSpec((tn, d), lambda b, r: (blk0 + b * nt + r, 0)),
            pl.BlockSpec((None, None, 1, d), lambda b, r: (5, mod_row_fn(b), 0, 0)),
            _const_spec((1, d)),
        ],
        out_specs=pl.BlockSpec((tn, d), lambda b, r: (b * nt + r, 0)),
        out_shape=jax.ShapeDtypeStruct((n_sets * n, d), F32),
        scratch_shapes=[pltpu.VMEM((tn, d), F32)],
        compiler_params=_params(2),
        name=f"combine_{n}",
    )(pt, y, x1, mod6, g_final)


def _rope_tables(n):
    tpos = jnp.arange(n, dtype=jnp.int32)
    row = (tpos // GRID_W).astype(F32)
    col = (tpos % GRID_W).astype(F32)
    inv = 1.0 / (ROPE_BASE ** (jnp.arange(ROPE_FREQS, dtype=F32) / ROPE_FREQS))
    ang = jnp.stack([row[:, None] * inv, col[:, None] * inv], axis=1)
    cos = jnp.cos(ang)[:, :, None, :]
    sin = jnp.sin(ang)[:, :, None, :]
    cos = jnp.broadcast_to(cos, (n, 2, 2, ROPE_FREQS)).reshape(n, D_ROPE)
    sin = jnp.concatenate([-sin, sin], axis=2).reshape(n, D_ROPE)
    pad = jnp.zeros((n, 128 - D_ROPE), F32)
    return jnp.concatenate([cos, pad], axis=1), jnp.concatenate([sin, pad], axis=1)


def _swap_rotary_halves(w):
    perm = np.arange(D_ROPE) ^ ROPE_FREQS
    return w[..., perm]


def kernel(x_prompt, x_sample, c, cache_ckv, cache_krope, c_ctx, g_attn, g_ffn, w_ada, b_ada, w_in, g_qa,
           w_qb, g_kv, w_uk, w_uv, w_o_mla, g_sgu, w_s, b_s, w_o_gm, w_out, w_router, w_e1, w_e3, w_e2,
           g_final):
    batch, seq, d = x_prompt.shape
    dec_batch, dec_seq, _ = x_sample.shape
    depth = g_attn.shape[0]
    assert depth == 1
    q_lora, kv_lora = g_qa.shape[1], g_kv.shape[1]
    gm_width = g_sgu.shape[1]
    n_ctx, n_lat = batch * seq, dec_batch * dec_seq
    assert dec_batch < MOD_ROWS // 2 + 1

    xc = x_prompt.reshape(n_ctx, d)
    xl = x_sample.reshape(n_lat, d)

    c_rows = jnp.zeros((MOD_ROWS, d), F32).at[:dec_batch].set(c).at[MOD_ROWS // 2].set(c_ctx)
    mod = _modulation(c_rows, w_ada[0], b_ada[0][None, :])
    mod6 = mod.reshape(MOD_ROWS, 6, 1, d).transpose(1, 0, 2, 3)

    cos_t, sin_t = _rope_tables(dec_seq)

    w_in_t = w_in[0].T
    o_kr = q_lora + kv_lora
    w_kr_t = w_in_t[o_kr:o_kr + D_ROPE]
    zpad = jnp.zeros((128 - D_ROPE, d), F32)
    w_small_t = jnp.concatenate(
        [w_in_t[:o_kr], w_kr_t, zpad, w_kr_t[np.arange(D_ROPE) ^ ROPE_FREQS], zpad], axis=0).astype(BF16)
    tm, bn_in = 512, 1024
    front = -(o_kr + D_ROPE) % bn_in
    w_all_t = _cast_bf16(w_in_t, rows=front, front_blocks=1)

    wq3 = w_qb[0].reshape(q_lora, N_HEADS, D_NOPE + D_ROPE)
    wq_nope, wq_rope = wq3[:, :, :D_NOPE], wq3[:, :, D_NOPE:]
    zq = jnp.zeros((q_lora, N_HEADS, HEAD_SLOT - D_NOPE - D_ROPE), F32)
    wq_p = jnp.concatenate([wq_nope, wq_rope, zq], axis=2).reshape(q_lora, N_HEADS * HEAD_SLOT).astype(BF16)
    wq_sw = jnp.concatenate([_swap_rotary_halves(wq_rope), zq], axis=2).reshape(
        q_lora, N_HEADS * 128).astype(BF16)

    h, ckv, kr, q_pad = _pre_proj(xc, xl, g_attn, mod6, w_small_t, g_qa, g_kv, cos_t, sin_t, wq_p, wq_sw,
                                  tm=tm)
    big = _wide_proj(h, w_all_t, n_big=w_in_t.shape[0] - o_kr - D_ROPE, tm=1024, bn=bn_in)

    w_uk2 = w_uk[0].reshape(kv_lora, N_HEADS // 2, 2 * D_NOPE).transpose(1, 0, 2).astype(BF16)
    w_uv2 = w_uv[0].reshape(kv_lora, N_HEADS // 2, 2 * D_V).transpose(1, 0, 2).astype(BF16)
    o_ctx = _attention(q_pad, ckv, kr, w_uk2, w_uv2, None, None,
                       row0=0, n_req=batch, n_own=seq, tq=seq, heads_per_iter=N_HEADS)
    o_lat = _attention(q_pad, ckv, kr, w_uk2, w_uv2, cache_ckv[:, 0], cache_krope[:, 0],
                       row0=n_ctx, n_req=dec_batch, n_own=dec_seq, tq=512, heads_per_iter=2)

    o_a = _o_proj(o_ctx, o_lat, w_o_mla[0].astype(BF16), tm=1024)
    merged = _sgu_merge(big, o_a, g_sgu, w_s[0].astype(BF16), b_s[0].T, w_o_gm[0].astype(BF16), tm=tm)

    w_router_p = jnp.concatenate(
        [w_router[0], jnp.zeros((d, LOGIT_LANES - N_EXPERTS), F32)], axis=1).astype(BF16)
    x1, h2, logits = _out_proj(merged, xc, xl, mod6, g_ffn, w_out[0].astype(BF16), w_router_p,
                               n_lat_seq=dec_seq, tm=tm)

    key_c, aff_c = _route(logits, row0=0, n_sets=batch, n=seq)
    key_l, aff_l = _route(logits, row0=n_ctx, n_sets=dec_batch, n=dec_seq)
    p_c, gate_c, xg_c = _dispatch_gather(key_c, aff_c, h2, row0=0, n_sets=batch, n=seq)
    p_l, gate_l, xg_l = _dispatch_gather(key_l, aff_l, h2, row0=n_ctx, n_sets=dec_batch, n=dec_seq)

    y_c, y_l = _experts(xg_c, xg_l, gate_c, gate_l, w_e1[0], w_e3[0], w_e2[0])

    y_prompt = _combine(p_c, y_c, x1, mod6, g_final[None, :], row0=0, n_sets=batch, n=seq,
                        mod_row_fn=lambda b: MOD_ROWS // 2)
    y_sample = _combine(p_l, y_l, x1, mod6, g_final[None, :], row0=n_ctx, n_sets=dec_batch, n=dec_seq,
                        mod_row_fn=lambda b: b)

    new_ckv = ckv[:n_ctx].reshape(batch, 1, seq, kv_lora)
    new_krope = kr[:n_ctx, :D_ROPE].reshape(batch, 1, seq, D_ROPE)
    return (y_prompt.reshape(batch, seq, d), y_sample.reshape(dec_batch, dec_seq, d), new_ckv, new_krope)
```

```python
import functools

import numpy as np
import jax
import jax.numpy as jnp
from jax import lax
from jax.experimental import pallas as pl
from jax.experimental.pallas import tpu as pltpu

F32 = jnp.float32
BF16 = jnp.bfloat16

N_HEADS = 16
D_NOPE = 128
D_ROPE = 64
D_V = 128
HEAD_SLOT = 256
ROPE_FREQS = D_ROPE // 4
ROPE_BASE = 10000.0
GRID_W = 64
CHUNK = 128
GM_GROUPS = 8
N_EXPERTS = 16
EC_FACTOR = 2
EPS = 1e-6
MOD_ROWS = 8
LOGIT_LANES = 128

VMEM_LIMIT_V7X = 56 * 1024 * 1024


def _params(n_axes):
    return pltpu.CompilerParams(
        dimension_semantics=("arbitrary",) * n_axes, vmem_limit_bytes=VMEM_LIMIT_V7X)


def _const_spec(shape):
    nd = len(shape)
    return pl.BlockSpec(shape, lambda *_: (0,) * nd, pipeline_mode=pl.Buffered(1))


def _half_tanh_half(x):
    return jnp.tanh(0.5 * x)


def _silu(x):
    h = 0.5 * x
    return h * jnp.tanh(h) + h


def _gelu_tanh(x):
    assert x.dtype == F32
    c = float(np.sqrt(2.0 / np.pi))
    return x * (0.5 * (1.0 + jnp.tanh(c * (x + 0.044715 * (x * x * x)))))


def _rms_rows(x, g):
    return x * lax.rsqrt(jnp.mean(x * x, axis=-1, keepdims=True) + EPS) * g


def _bdot(a, b):
    return jnp.dot(a, b, preferred_element_type=F32)


def _bdot_t(a, bt):
    return lax.dot_general(a, bt, (((1,), (1,)), ((), ())), preferred_element_type=F32)


def _mod_kernel(c_ref, w_ref, b_ref, o_ref):
    c = c_ref[...]
    s = _silu(c).astype(BF16)
    o_ref[...] = _bdot(s, w_ref[...].astype(BF16)) + b_ref[...]


def _modulation(c_rows, w_ada, b_ada, bn=1024):
    d, n = w_ada.shape
    return pl.pallas_call(
        _mod_kernel,
        grid=(n // bn,),
        in_specs=[
            _const_spec((MOD_ROWS, d)),
            pl.BlockSpec((d, bn), lambda j: (0, j)),
            pl.BlockSpec((1, bn), lambda j: (0, j)),
        ],
        out_specs=pl.BlockSpec((MOD_ROWS, bn), lambda j: (0, j)),
        out_shape=jax.ShapeDtypeStruct((MOD_ROWS, n), F32),
        compiler_params=_params(1),
        name="adaln_mod",
    )(c_rows, w_ada, b_ada)


def _cast_kernel(x_ref, o_ref, *, front_blocks):
    i = pl.program_id(0)

    @pl.when(i < front_blocks)
    def _():
        o_ref[...] = jnp.zeros(o_ref.shape, o_ref.dtype)

    @pl.when(i >= front_blocks)
    def _():
        o_ref[...] = x_ref[...].astype(o_ref.dtype)


def _cast_bf16(x, *, rows, front_blocks=0):
    r, c = x.shape
    assert r % rows == 0 and rows % 16 == 0
    nb = r // rows + front_blocks
    return pl.pallas_call(
        functools.partial(_cast_kernel, front_blocks=front_blocks),
        grid=(nb,),
        in_specs=[pl.BlockSpec((rows, c), lambda i: (jnp.maximum(i - front_blocks, 0), 0))],
        out_specs=pl.BlockSpec((rows, c), lambda i: (i, 0)),
        out_shape=jax.ShapeDtypeStruct((nb * rows, c), BF16),
        compiler_params=_params(1),
        name="cast_bf16",
    )(x)


def _preproj_kernel(xc_ref, xl_ref, g_ref, sc_ref, sh_ref, ws_ref, gqa_ref, gkv_ref, cos_ref, sin_ref,
                    wq_ref, wsw_ref, h_ref, ckv_ref, kr_ref, q_ref, *, n_ctx_tiles, q_lora, kv_lora, scale):
    i = pl.program_id(0)

    def body(x_ref, rotary):
        h = _rms_rows(x_ref[...], g_ref[...]) * (1.0 + sc_ref[...]) + sh_ref[...]
        hb = h.astype(BF16)
        h_ref[...] = hb
        small = _bdot_t(hb, ws_ref[...])
        qa = _rms_rows(small[:, :q_lora], gqa_ref[...]).astype(BF16)
        ckv_ref[...] = _rms_rows(small[:, q_lora:q_lora + kv_lora], gkv_ref[...])
        o = q_lora + kv_lora
        kr = small[:, o:o + 128]
        if not rotary:
            kr_ref[...] = kr
            for hd in range(N_HEADS):
                q = _bdot(qa, wq_ref[:, hd * HEAD_SLOT:(hd + 1) * HEAD_SLOT])
                q_ref[hd] = (q * scale).astype(BF16)
            return
        cos = cos_ref[...]
        sin = sin_ref[...]
        kr_sw = small[:, o + 128:o + 256]
        kr_ref[...] = kr * cos + kr_sw * sin
        for hp in range(N_HEADS // 2):
            q_sw = _bdot(qa, wsw_ref[:, hp * 256:(hp + 1) * 256])
            for s in range(2):
                hd = 2 * hp + s
                q = _bdot(qa, wq_ref[:, hd * HEAD_SLOT:(hd + 1) * HEAD_SLOT])
                q_ref[hd, :, 0:128] = (q[:, 0:128] * scale).astype(BF16)
                rot = q[:, 128:256] * cos + q_sw[:, s * 128:(s + 1) * 128] * sin
                q_ref[hd, :, 128:256] = (rot * scale).astype(BF16)

    @pl.when(i < n_ctx_tiles)
    def _():
        body(xc_ref, False)

    @pl.when(i >= n_ctx_tiles)
    def _():
        body(xl_ref, True)


def _pre_proj(xc, xl, g_attn, mod6, w_small_t, g_qa, g_kv, cos_t, sin_t, wq_p, wq_sw, *, tm):
    n_ctx, d = xc.shape
    n_lat_seq = cos_t.shape[0]
    t = n_ctx + xl.shape[0]
    q_lora, kv_lora = g_qa.shape[1], g_kv.shape[1]
    n_ctx_tiles = n_ctx // tm
    tiles_per_seq = n_lat_seq // tm

    def mod_row(i):
        return jnp.where(i < n_ctx_tiles, MOD_ROWS // 2, (i - n_ctx_tiles) // tiles_per_seq)

    def rope_blk(i):
        return jnp.maximum(i - n_ctx_tiles, 0) % tiles_per_seq

    scale = np.float32((D_NOPE + D_ROPE) ** -0.5)
    kern = functools.partial(_preproj_kernel, n_ctx_tiles=n_ctx_tiles, q_lora=q_lora, kv_lora=kv_lora,
                             scale=scale)
    return pl.pallas_call(
        kern,
        grid=(t // tm,),
        in_specs=[
            pl.BlockSpec((tm, d), lambda i: (jnp.minimum(i, n_ctx_tiles - 1), 0)),
            pl.BlockSpec((tm, d), lambda i: (jnp.maximum(i - n_ctx_tiles, 0), 0)),
            _const_spec((1, d)),
            pl.BlockSpec((None, None, 1, d), lambda i: (1, mod_row(i), 0, 0)),
            pl.BlockSpec((None, None, 1, d), lambda i: (0, mod_row(i), 0, 0)),
            _const_spec(w_small_t.shape),
            _const_spec((1, q_lora)),
            _const_spec((1, kv_lora)),
            pl.BlockSpec((tm, 128), lambda i: (rope_blk(i), 0)),
            pl.BlockSpec((tm, 128), lambda i: (rope_blk(i), 0)),
            _const_spec(wq_p.shape),
            _const_spec(wq_sw.shape),
        ],
        out_specs=[
            pl.BlockSpec((tm, d), lambda i: (i, 0)),
            pl.BlockSpec((tm, kv_lora), lambda i: (i, 0)),
            pl.BlockSpec((tm, 128), lambda i: (i, 0)),
            pl.BlockSpec((N_HEADS, tm, HEAD_SLOT), lambda i: (0, i, 0)),
        ],
        out_shape=[
            jax.ShapeDtypeStruct((t, d), BF16),
            jax.ShapeDtypeStruct((t, kv_lora), F32),
            jax.ShapeDtypeStruct((t, 128), F32),
            jax.ShapeDtypeStruct((N_HEADS, t, HEAD_SLOT), BF16),
        ],
        compiler_params=_params(1),
        name="pre_proj",
    )(xc, xl, g_attn, mod6, mod6, w_small_t, g_qa, g_kv, cos_t, sin_t, wq_p, wq_sw)


def _wideproj_kernel(h_ref, wa_ref, wb_ref, o_ref, *, tm, bn, n_gelu_steps):
    j = pl.program_id(0)
    h = h_ref[...]

    def run(act):
        for c, w_ref in enumerate((wa_ref, wb_ref)):
            y = _bdot_t(h, w_ref[...])
            for r0 in range(0, tm, 256):
                o_ref[r0:r0 + 256, c * bn:(c + 1) * bn] = act(y[r0:r0 + 256, :]).astype(BF16)

    @pl.when(j < n_gelu_steps)
    def _():
        run(_gelu_tanh)

    @pl.when(j >= n_gelu_steps)
    def _():
        run(lambda y: y)


def _wide_proj(h, w_all_t, *, n_big, n_gelu, tm, bn):
    t, d = h.shape
    blk0 = (w_all_t.shape[0] - n_big) // bn
    return pl.pallas_call(
        functools.partial(_wideproj_kernel, tm=tm, bn=bn, n_gelu_steps=n_gelu // (2 * bn)),
        grid=(n_big // (2 * bn), t // tm),
        in_specs=[
            pl.BlockSpec((tm, d), lambda j, i: (i, 0)),
            pl.BlockSpec((bn, d), lambda j, i: (blk0 + 2 * j, 0)),
            pl.BlockSpec((bn, d), lambda j, i: (blk0 + 2 * j + 1, 0)),
        ],
        out_specs=pl.BlockSpec((tm, 2 * bn), lambda j, i: (i, j)),
        out_shape=jax.ShapeDtypeStruct((t, n_big), BF16),
        compiler_params=_params(2),
        name="wide_proj",
    )(h, w_all_t, w_all_t)


def _attn_kernel(*refs, n_own, n_cache, heads_per_iter):
    if n_cache:
        (q_ref, ckv_ref, kr_ref, cckv_ref, ckr_ref, wuk_ref, wuv_ref,
         o_ref, kpad, vexp, kall, krp, o_scr) = refs
    else:
        (q_ref, ckv_ref, kr_ref, wuk_ref, wuv_ref, o_ref, kpad, vexp, kall, krp, o_scr) = refs
    qi = pl.program_id(1)

    @pl.when(qi == 0)
    def _():
        kall[0:n_own, :] = ckv_ref[...].astype(BF16)
        krp[0:n_own, :] = kr_ref[...].astype(BF16)
        if n_cache:
            kall[n_own:n_own + n_cache, :] = cckv_ref[...].astype(BF16)
            krp[n_own:n_own + n_cache, 0:D_ROPE] = ckr_ref[...].astype(BF16)
            krp[n_own:n_own + n_cache, D_ROPE:128] = jnp.zeros((n_cache, 128 - D_ROPE), BF16)

        def expand(hp, carry):
            kn = _bdot(kall[...], wuk_ref[hp]).astype(BF16)
            vv = _bdot(kall[...], wuv_ref[hp]).astype(BF16)
            for s in range(2):
                kpad[2 * hp + s, :, 0:128] = kn[:, s * 128:(s + 1) * 128]
                kpad[2 * hp + s, :, 128:256] = krp[...]
                vexp[2 * hp + s] = vv[:, s * 128:(s + 1) * 128]
            return carry

        if heads_per_iter == N_HEADS:
            for hp in range(N_HEADS // 2):
                expand(hp, 0)
        else:
            lax.fori_loop(0, N_HEADS // 2, expand, 0)

    def one_head(h):
        s = _bdot_t(q_ref[h], kpad[h])
        p = jnp.exp(s - jnp.max(s, axis=-1, keepdims=True))
        l = jnp.sum(p, axis=-1, keepdims=True)
        o = _bdot(p.astype(BF16), vexp[h])
        return (o / l).astype(BF16)

    if heads_per_iter == N_HEADS:
        for h in range(N_HEADS):
            o_ref[:, h * D_V:(h + 1) * D_V] = one_head(h)
    else:
        def head_group(hg, carry):
            for g in range(heads_per_iter):
                h = hg * heads_per_iter + g
                o_scr[h] = one_head(h)
            return carry

        lax.fori_loop(0, N_HEADS // heads_per_iter, head_group, 0)
        for h in range(N_HEADS):
            o_ref[:, h * D_V:(h + 1) * D_V] = o_scr[h]


def _attention(q_pad, ckv, kr, w_uk, w_uv, cache_ckv, cache_kr, *, row0, n_req, n_own, tq, heads_per_iter):
    kv_lora = ckv.shape[1]
    n_cache = 0 if cache_ckv is None else cache_ckv.shape[1]
    kn = n_own + n_cache
    qb = n_own // tq
    in_specs = [
        pl.BlockSpec((N_HEADS, tq, HEAD_SLOT), lambda b, qi: (0, row0 // tq + b * qb + qi, 0)),
        pl.BlockSpec((n_own, kv_lora), lambda b, qi: (row0 // n_own + b, 0)),
        pl.BlockSpec((n_own, 128), lambda b, qi: (row0 // n_own + b, 0)),
    ]
    args = [q_pad, ckv, kr]
    if n_cache:
        in_specs += [
            pl.BlockSpec((None, n_cache, kv_lora), lambda b, qi: (b, 0, 0)),
            pl.BlockSpec((None, n_cache, D_ROPE), lambda b, qi: (b, 0, 0)),
        ]
        args += [cache_ckv, cache_kr]
    in_specs += [_const_spec(w_uk.shape), _const_spec(w_uv.shape)]
    args += [w_uk, w_uv]
    return pl.pallas_call(
        functools.partial(_attn_kernel, n_own=n_own, n_cache=n_cache, heads_per_iter=heads_per_iter),
        grid=(n_req, qb),
        in_specs=in_specs,
        out_specs=pl.BlockSpec((tq, N_HEADS * D_V), lambda b, qi: (b * qb + qi, 0)),
        out_shape=jax.ShapeDtypeStruct((n_req * n_own, N_HEADS * D_V), BF16),
        scratch_shapes=[
            pltpu.VMEM((N_HEADS, kn, HEAD_SLOT), BF16),
            pltpu.VMEM((N_HEADS, kn, D_V), BF16),
            pltpu.VMEM((kn, kv_lora), BF16),
            pltpu.VMEM((kn, 128), BF16),
            pltpu.VMEM((N_HEADS, tq, D_V), BF16),
        ],
        compiler_params=_params(2),
        name="mla_attn_cache" if n_cache else "mla_attn",
    )(*args)


def _oproj_kernel(xc_ref, xl_ref, w_ref, o_ref, *, n_ctx_tiles):
    i = pl.program_id(0)

    @pl.when(i < n_ctx_tiles)
    def _():
        o_ref[...] = _bdot(xc_ref[...], w_ref[...]).astype(o_ref.dtype)

    @pl.when(i >= n_ctx_tiles)
    def _():
        o_ref[...] = _bdot(xl_ref[...], w_ref[...]).astype(o_ref.dtype)


def _o_proj(xc, xl, w, *, tm):
    n_ctx, k = xc.shape
    t = n_ctx + xl.shape[0]
    n = w.shape[1]
    n_ctx_tiles = n_ctx // tm
    return pl.pallas_call(
        functools.partial(_oproj_kernel, n_ctx_tiles=n_ctx_tiles),
        grid=(t // tm,),
        in_specs=[
            pl.BlockSpec((tm, k), lambda i: (jnp.minimum(i, n_ctx_tiles - 1), 0)),
            pl.BlockSpec((tm, k), lambda i: (jnp.maximum(i - n_ctx_tiles, 0), 0)),
            _const_spec(w.shape),
        ],
        out_specs=pl.BlockSpec((tm, n), lambda i: (i, 0)),
        out_shape=jax.ShapeDtypeStruct((t, n), BF16),
        compiler_params=_params(1),
        name="o_proj",
    )(xc, xl, w)


def _sgu_merge_kernel(gu_ref, gv_ref, ga_ref, gb_ref, oa_ref, g_ref, ws_ref, bs_ref, wo_ref, m_ref, z_scr,
                      *, tm, gc):
    s = pl.program_id(0)

    @pl.when(s == 0)
    def _():
        z_scr[1] = jnp.zeros(z_scr.shape[1:], BF16)

    o_b = _bdot(z_scr[(s + 1) % 2], wo_ref[...])
    for r0 in range(0, tm, CHUNK):
        rows = slice(r0, r0 + CHUNK)
        oa = oa_ref[rows, :]
        ob = o_b[rows, :].astype(BF16)
        ta = _half_tanh_half(ga_ref[rows, :])
        tb = _half_tanh_half(gb_ref[rows, :])
        m_ref[rows, :] = 0.5 * ((oa + ob) + (ta * oa + tb * ob))

    slot = s % 2
    g_bf = g_ref[...].astype(BF16)
    for r0 in range(0, tm, CHUNK):
        v = gv_ref[r0:r0 + CHUNK, :]
        v32 = v.astype(F32)
        inv = lax.rsqrt(jnp.mean(v32 * v32, axis=-1, keepdims=True) + EPS)
        vn = v * inv.astype(BF16) * g_bf
        for g in range(GM_GROUPS):
            c0 = g * gc
            mix = _bdot(ws_ref[g], vn[:, c0:c0 + gc]) + bs_ref[:, g:g + 1]
            u = gu_ref[r0:r0 + CHUNK, c0:c0 + gc]
            z_scr[slot, r0:r0 + CHUNK, c0:c0 + gc] = u * mix.astype(BF16)


def _sgu_merge(big, o_a, g_sgu, w_s, b_s_t, w_o_gm, *, tm):
    t = big.shape[0]
    width = g_sgu.shape[1]
    d = w_o_gm.shape[1]
    gc = width // GM_GROUPS
    nt = t // tm

    def fill(s):
        return jnp.minimum(s, nt - 1)

    def drain(s):
        return jnp.maximum(s - 1, 0)

    return pl.pallas_call(
        functools.partial(_sgu_merge_kernel, tm=tm, gc=gc),
        grid=(nt + 1,),
        in_specs=[
            pl.BlockSpec((tm, width), lambda s: (fill(s), 0)),
            pl.BlockSpec((tm, width), lambda s: (fill(s), 1)),
            pl.BlockSpec((tm, d), lambda s: (drain(s), 2)),
            pl.BlockSpec((tm, d), lambda s: (drain(s), 3)),
            pl.BlockSpec((tm, d), lambda s: (drain(s), 0)),
            _const_spec((1, width)),
            _const_spec(w_s.shape),
            _const_spec(b_s_t.shape),
            _const_spec(w_o_gm.shape),
        ],
        out_specs=pl.BlockSpec((tm, d), lambda s: (drain(s), 0)),
        out_shape=jax.ShapeDtypeStruct((t, d), BF16),
        scratch_shapes=[pltpu.VMEM((2, tm, width), BF16)],
        compiler_params=_params(1),
        name="sgu_merge",
    )(big, big, big, big, o_a, g_sgu, w_s, b_s_t, w_o_gm)


def _outproj_kernel(m_ref, xc_ref, xl_ref, gate_ref, sc_ref, sh_ref, g_ref, wo_ref, wr_ref,
                    x1_ref, h2_ref, lg_ref, *, tm, n_ctx_tiles):
    i = pl.program_id(0)
    r = _bdot(m_ref[...], wo_ref[...])

    def epilogue(x_ref):
        for r0 in range(0, tm, CHUNK):
            rows = slice(r0, r0 + CHUNK)
            x1 = x_ref[rows, :] + gate_ref[...] * r[rows, :]
            x1_ref[rows, :] = x1
            h2 = (_rms_rows(x1, g_ref[...]) * (1.0 + sc_ref[...]) + sh_ref[...]).astype(BF16)
            h2_ref[rows, :] = h2
            lg_ref[rows, :] = _bdot(h2, wr_ref[...])

    @pl.when(i < n_ctx_tiles)
    def _():
        epilogue(xc_ref)

    @pl.when(i >= n_ctx_tiles)
    def _():
        epilogue(xl_ref)


def _out_proj(merged, xc, xl, mod6, g_ffn, w_out, w_router_p, *, n_lat_seq, tm):
    n_ctx, d = xc.shape
    t = n_ctx + xl.shape[0]
    n_ctx_tiles = n_ctx // tm
    tiles_per_seq = n_lat_seq // tm

    def mod_row(i):
        return jnp.where(i < n_ctx_tiles, MOD_ROWS // 2, (i - n_ctx_tiles) // tiles_per_seq)

    return pl.pallas_call(
        functools.partial(_outproj_kernel, tm=tm, n_ctx_tiles=n_ctx_tiles),
        grid=(t // tm,),
        in_specs=[
            pl.BlockSpec((tm, d), lambda i: (i, 0)),
            pl.BlockSpec((tm, d), lambda i: (jnp.minimum(i, n_ctx_tiles - 1), 0)),
            pl.BlockSpec((tm, d), lambda i: (jnp.maximum(i - n_ctx_tiles, 0), 0)),
            pl.BlockSpec((None, None, 1, d), lambda i: (2, mod_row(i), 0, 0)),
            pl.BlockSpec((None, None, 1, d), lambda i: (4, mod_row(i), 0, 0)),
            pl.BlockSpec((None, None, 1, d), lambda i: (3, mod_row(i), 0, 0)),
            _const_spec((1, d)),
            _const_spec(w_out.shape),
            _const_spec(w_router_p.shape),
        ],
        out_specs=[
            pl.BlockSpec((tm, d), lambda i: (i, 0)),
            pl.BlockSpec((tm, d), lambda i: (i, 0)),
            pl.BlockSpec((tm, LOGIT_LANES), lambda i: (i, 0)),
        ],
        out_shape=[
            jax.ShapeDtypeStruct((t, d), F32),
            jax.ShapeDtypeStruct((t, d), BF16),
            jax.ShapeDtypeStruct((t, LOGIT_LANES), F32),
        ],
        compiler_params=_params(1),
        name="out_proj",
    )(merged, xc, xl, mod6, mod6, mod6, g_ffn, w_out, w_router_p)


BISECT_STEPS = 48
MIN_NORMAL_F32 = float(np.finfo(np.float32).tiny)


def _route_kernel(lg_ref, key_ref, aff_ref, tri_scr, *, n_sets, n, cap):
    for r0 in range(0, n, 128):
        r = lax.broadcasted_iota(jnp.int32, (128, n), 0) + r0
        c = lax.broadcasted_iota(jnp.int32, (128, n), 1)
        tri_scr[r0:r0 + 128, :] = jnp.where(r < c, 1.0, 0.0).astype(BF16)

    for s in range(n_sets):
        logits = lg_ref[s * n:(s + 1) * n, :].T[0:N_EXPERTS, :]
        e = jnp.exp(logits - jnp.max(logits, axis=0, keepdims=True))
        aff_ref[s * N_EXPERTS:(s + 1) * N_EXPERTS, :] = e / jnp.sum(e, axis=0, keepdims=True)
    aff = aff_ref[...]
    rows = n_sets * N_EXPERTS

    def count_ge(thr):
        return jnp.sum(jnp.where(aff >= thr, 1.0, 0.0), axis=1, keepdims=True)

    def bisect(_, carry):
        lo, hi = carry
        mid = jnp.sqrt(lo) * jnp.sqrt(hi)
        ok = count_ge(mid) >= cap
        return jnp.where(ok, mid, lo), jnp.where(ok, hi, mid)

    lo0 = jnp.full((rows, 1), MIN_NORMAL_F32, F32)
    hi0 = jnp.full((rows, 1), 2.0, F32)
    lo, hi = lax.fori_loop(0, BISECT_STEPS, bisect, (lo0, hi0))
    lo = jnp.where(count_ge(lo) >= cap, lo, 0.0)

    above = aff >= hi
    band = (aff >= lo) & jnp.logical_not(above)
    need = cap - jnp.sum(jnp.where(above, 1.0, 0.0), axis=1, keepdims=True)
    tri = tri_scr[...]
    band_before = _bdot(jnp.where(band, 1.0, 0.0).astype(BF16), tri)
    sel = above | (band & (band_before < need))
    pos = _bdot(jnp.where(sel, 1.0, 0.0).astype(BF16), tri)
    key_ref[...] = jnp.where(sel, pos, -1.0)


def _route(logits, *, row0, n_sets, n):
    cap = EC_FACTOR * n // N_EXPERTS
    rows = n_sets * N_EXPERTS
    blk = row0 // (n_sets * n)
    return pl.pallas_call(
        functools.partial(_route_kernel, n_sets=n_sets, n=n, cap=cap),
        grid=(1,),
        in_specs=[pl.BlockSpec((n_sets * n, LOGIT_LANES), lambda g: (blk, 0))],
        out_specs=[pl.BlockSpec((rows, n), lambda g: (0, 0)), pl.BlockSpec((rows, n), lambda g: (0, 0))],
        out_shape=[jax.ShapeDtypeStruct((rows, n), F32), jax.ShapeDtypeStruct((rows, n), F32)],
        scratch_shapes=[pltpu.VMEM((n, n), BF16)],
        compiler_params=_params(1),
        name=f"route_{n}",
    )(logits)


def _gather_kernel(key_ref, aff_ref, h2_ref, pt_ref, gate_ref, xg_ref, p_scr, *, n, cap, d, nb):
    key = key_ref[...]
    aff = aff_ref[...]
    slot = lax.broadcasted_iota(jnp.int32, (cap, n), 0).astype(F32)
    per_group = 128 // cap
    for grp in range(N_EXPERTS // per_group):
        pieces = []
        for ex in range(grp * per_group, (grp + 1) * per_group):
            hit = slot == key[ex:ex + 1, :]
            gate_ref[ex] = jnp.sum(jnp.where(hit, aff[ex:ex + 1, :], 0.0), axis=1, keepdims=True)
            pieces.append(jnp.where(hit, 1.0, 0.0))
        hit128 = pieces[0] if per_group == 1 else jnp.concatenate(pieces, axis=0)
        p_scr[grp * 128:(grp + 1) * 128, :] = hit128.astype(BF16)
        pt_ref[:, grp * 128:(grp + 1) * 128] = hit128.T.astype(BF16)

    p = p_scr[...]
    for c in range(d // nb):
        xg = _bdot(p, h2_ref[:, c * nb:(c + 1) * nb]).astype(BF16)
        for ex in range(N_EXPERTS):
            xg_ref[ex, :, c * nb:(c + 1) * nb] = xg[ex * cap:(ex + 1) * cap, :]


def _dispatch_gather(key, aff, h2, *, row0, n_sets, n):
    d = h2.shape[1]
    cap = EC_FACTOR * n // N_EXPERTS
    slots = N_EXPERTS * cap
    blk0 = row0 // n
    return pl.pallas_call(
        functools.partial(_gather_kernel, n=n, cap=cap, d=d, nb=256),
        grid=(n_sets,),
        in_specs=[
            pl.BlockSpec((N_EXPERTS, n), lambda b: (b, 0)),
            pl.BlockSpec((N_EXPERTS, n), lambda b: (b, 0)),
            pl.BlockSpec((n, d), lambda b: (blk0 + b, 0)),
        ],
        out_specs=[
            pl.BlockSpec((None, n, slots), lambda b: (b, 0, 0)),
            pl.BlockSpec((N_EXPERTS, None, cap, 1), lambda b: (0, b, 0, 0)),
            pl.BlockSpec((N_EXPERTS, None, cap, d), lambda b: (0, b, 0, 0)),
        ],
        out_shape=[
            jax.ShapeDtypeStruct((n_sets, n, slots), BF16),
            jax.ShapeDtypeStruct((N_EXPERTS, n_sets, cap, 1), F32),
            jax.ShapeDtypeStruct((N_EXPERTS, n_sets, cap, d), BF16),
        ],
        scratch_shapes=[pltpu.VMEM((slots, n), BF16)],
        compiler_params=_params(1),
        name=f"dispatch_gather_{n}",
    )(key, aff, h2)


def _expert_kernel(xc_ref, xl_ref, gc_ref, gl_ref, w1_ref, w3_ref, w2_ref, yc_ref, yl_ref, hc_scr, hl_scr,
                   *, n_f, fc, rows_c, rows_l, d):
    k = pl.program_id(1)

    @pl.when(k < n_f)
    def _():
        w1 = w1_ref[...].astype(BF16)
        w3 = w3_ref[...].astype(BF16)
        for x_ref, h_scr, rows in ((xc_ref, hc_scr, rows_c), (xl_ref, hl_scr, rows_l)):
            x = x_ref[...].reshape(rows, d)
            a = _bdot(x, w1)
            g = _bdot(x, w3)
            h_scr[k] = (_silu(a) * g).astype(BF16)

    @pl.when(k >= n_f)
    def _():
        w2 = w2_ref[...].astype(BF16)
        for h_scr, g_ref, y_ref, rows in ((hc_scr, gc_ref, yc_ref, rows_c), (hl_scr, gl_ref, yl_ref, rows_l)):
            y = _bdot(h_scr[0], w2[0:fc, :])
            for kk in range(1, n_f):
                y = y + _bdot(h_scr[kk], w2[kk * fc:(kk + 1) * fc, :])
            y_ref[...] = (y * g_ref[...].reshape(rows, 1)).astype(BF16).reshape(y_ref.shape)


def _experts(xg_c, xg_l, gate_c, gate_l, w_e1, w_e3, w_e2, *, fc=512, nc=512):
    n_e, sets_c, cap_c, d = xg_c.shape
    _, sets_l, cap_l, _ = xg_l.shape
    d_ff = w_e1.shape[2]
    n_f = d_ff // fc
    n_c = d // nc
    rows_c, rows_l = sets_c * cap_c, sets_l * cap_l

    def e_in(e, k):
        return jnp.minimum(e + (k >= n_f).astype(jnp.int32), n_e - 1)

    def f_idx(k):
        return jnp.where(k < n_f, k, 0)

    def c_idx(k):
        return jnp.maximum(k - n_f, 0)

    kern = functools.partial(_expert_kernel, n_f=n_f, fc=fc, rows_c=rows_c, rows_l=rows_l, d=d)
    return pl.pallas_call(
        kern,
        grid=(n_e, n_f + n_c),
        in_specs=[
            pl.BlockSpec((None, sets_c, cap_c, d), lambda e, k: (e_in(e, k), 0, 0, 0)),
            pl.BlockSpec((None, sets_l, cap_l, d), lambda e, k: (e_in(e, k), 0, 0, 0)),
            pl.BlockSpec((None, sets_c, cap_c, 1), lambda e, k: (e, 0, 0, 0)),
            pl.BlockSpec((None, sets_l, cap_l, 1), lambda e, k: (e, 0, 0, 0)),
            pl.BlockSpec((None, d, fc), lambda e, k: (e_in(e, k), 0, f_idx(k))),
            pl.BlockSpec((None, d, fc), lambda e, k: (e_in(e, k), 0, f_idx(k))),
            pl.BlockSpec((None, d_ff, nc), lambda e, k: (e, 0, c_idx(k))),
        ],
        out_specs=[
            pl.BlockSpec((None, sets_c, cap_c, nc), lambda e, k: (e, 0, 0, c_idx(k))),
            pl.BlockSpec((None, sets_l, cap_l, nc), lambda e, k: (e, 0, 0, c_idx(k))),
        ],
        out_shape=[
            jax.ShapeDtypeStruct(xg_c.shape, BF16),
            jax.ShapeDtypeStruct(xg_l.shape, BF16),
        ],
        scratch_shapes=[pltpu.VMEM((n_f, rows_c, fc), BF16), pltpu.VMEM((n_f, rows_l, fc), BF16)],
        compiler_params=_params(2),
        name="experts",
    )(xg_c, xg_l, gate_c, gate_l, w_e1, w_e3, w_e2)


def _combine_kernel(pt_ref, y_ref, x1_ref, gate_ref, g_ref, o_ref, acc_scr, *, slots, d, nb):
    pt = pt_ref[...]
    for c in range(d // nb):
        y = y_ref[:, :, c * nb:(c + 1) * nb].reshape(slots, nb)
        moe = _bdot(pt, y)
        acc_scr[:, c * nb:(c + 1) * nb] = (x1_ref[:, c * nb:(c + 1) * nb]
                                           + gate_ref[:, c * nb:(c + 1) * nb] * moe)
    o_ref[...] = _rms_rows(acc_scr[...], g_ref[...])


def _combine(pt, y, x1, mod6, g_final, *, row0, n_sets, n, mod_row_fn):
    d = x1.shape[1]
    cap = EC_FACTOR * n // N_EXPERTS
    slots = N_EXPERTS * cap
    tn = min(n, 512)
    nt = n // tn
    blk0 = row0 // tn
    return pl.pallas_call(
        functools.partial(_combine_kernel, slots=slots, d=d, nb=512),
        grid=(n_sets, nt),
        in_specs=[
            pl.BlockSpec((None, tn, slots), lambda b, r: (b, r, 0)),
            pl.BlockSpec((N_EXPERTS, None, cap, d), lambda b, r: (0, b, 0, 0)),
            pl.BlockSpec((tn, d), lambda b, r: (blk0 + b * nt + r, 0)),
            pl.BlockSpec((None, None, 1, d), lambda b, r: (5, mod_row_fn(b), 0, 0)),
            _const_spec((1, d)),
        ],
        out_specs=pl.BlockSpec((tn, d), lambda b, r: (b * nt + r, 0)),
        out_shape=jax.ShapeDtypeStruct((n_sets * n, d), F32),
        scratch_shapes=[pltpu.VMEM((tn, d), F32)],
        compiler_params=_params(2),
        name=f"combine_{n}",
    )(pt, y, x1, mod6, g_final)


def _rope_tables(n):
    tpos = jnp.arange(n, dtype=jnp.int32)
    row = (tpos // GRID_W).astype(F32)
    col = (tpos % GRID_W).astype(F32)
    inv = 1.0 / (ROPE_BASE ** (jnp.arange(ROPE_FREQS, dtype=F32) / ROPE_FREQS))
    ang = jnp.stack([row[:, None] * inv, col[:, None] * inv], axis=1)
    cos = jnp.cos(ang)[:, :, None, :]
    sin = jnp.sin(ang)[:, :, None, :]
    cos = jnp.broadcast_to(cos, (n, 2, 2, ROPE_FREQS)).reshape(n, D_ROPE)
    sin = jnp.concatenate([-sin, sin], axis=2).reshape(n, D_ROPE)
    pad = jnp.zeros((n, 128 - D_ROPE), F32)
    return jnp.concatenate([cos, pad], axis=1), jnp.concatenate([sin, pad], axis=1)


def _swap_rotary_halves(w):
    perm = np.arange(D_ROPE) ^ ROPE_FREQS
    return w[..., perm]


def kernel(x_prompt, x_sample, c, cache_ckv, cache_krope, c_ctx, g_attn, g_ffn, w_ada, b_ada, w_in, g_qa,
           w_qb, g_kv, w_uk, w_uv, w_o_mla, g_sgu, w_s, b_s, w_o_gm, w_out, w_router, w_e1, w_e3, w_e2,
           g_final):
    batch, seq, d = x_prompt.shape
    dec_batch, dec_seq, _ = x_sample.shape
    depth = g_attn.shape[0]
    assert depth == 1
    q_lora, kv_lora = g_qa.shape[1], g_kv.shape[1]
    gm_width = g_sgu.shape[1]
    n_ctx, n_lat = batch * seq, dec_batch * dec_seq
    assert dec_batch < MOD_ROWS // 2 + 1

    xc = x_prompt.reshape(n_ctx, d)
    xl = x_sample.reshape(n_lat, d)

    c_rows = jnp.zeros((MOD_ROWS, d), F32).at[:dec_batch].set(c).at[MOD_ROWS // 2].set(c_ctx)
    mod = _modulation(c_rows, w_ada[0], b_ada[0][None, :])
    mod6 = mod.reshape(MOD_ROWS, 6, 1, d).transpose(1, 0, 2, 3)

    cos_t, sin_t = _rope_tables(dec_seq)

    w_in_t = w_in[0].T
    o_kr = q_lora + kv_lora
    w_kr_t = w_in_t[o_kr:o_kr + D_ROPE]
    zpad = jnp.zeros((128 - D_ROPE, d), F32)
    w_small_t = jnp.concatenate(
        [w_in_t[:o_kr], w_kr_t, zpad, w_kr_t[np.arange(D_ROPE) ^ ROPE_FREQS], zpad], axis=0).astype(BF16)
    tm, bn_in = 512, 1024
    front = -(o_kr + D_ROPE) % bn_in
    w_all_t = _cast_bf16(w_in_t, rows=front, front_blocks=1)

    wq3 = w_qb[0].reshape(q_lora, N_HEADS, D_NOPE + D_ROPE)
    wq_nope, wq_rope = wq3[:, :, :D_NOPE], wq3[:, :, D_NOPE:]
    zq = jnp.zeros((q_lora, N_HEADS, HEAD_SLOT - D_NOPE - D_ROPE), F32)
    wq_p = jnp.concatenate([wq_nope, wq_rope, zq], axis=2).reshape(q_lora, N_HEADS * HEAD_SLOT).astype(BF16)
    wq_sw = jnp.concatenate([_swap_rotary_halves(wq_rope), zq], axis=2).reshape(
        q_lora, N_HEADS * 128).astype(BF16)

    h, ckv, kr, q_pad = _pre_proj(xc, xl, g_attn, mod6, w_small_t, g_qa, g_kv, cos_t, sin_t, wq_p, wq_sw,
                                  tm=tm)
    big = _wide_proj(h, w_all_t, n_big=w_in_t.shape[0] - o_kr - D_ROPE, n_gelu=2 * gm_width, tm=1024,
                     bn=bn_in)

    w_uk2 = w_uk[0].reshape(kv_lora, N_HEADS // 2, 2 * D_NOPE).transpose(1, 0, 2).astype(BF16)
    w_uv2 = w_uv[0].reshape(kv_lora, N_HEADS // 2, 2 * D_V).transpose(1, 0, 2).astype(BF16)
    o_ctx = _attention(q_pad, ckv, kr, w_uk2, w_uv2, None, None,
                       row0=0, n_req=batch, n_own=seq, tq=seq, heads_per_iter=N_HEADS)
    o_lat = _attention(q_pad, ckv, kr, w_uk2, w_uv2, cache_ckv[:, 0], cache_krope[:, 0],
                       row0=n_ctx, n_req=dec_batch, n_own=dec_seq, tq=512, heads_per_iter=4)

    o_a = _o_proj(o_ctx, o_lat, w_o_mla[0].astype(BF16), tm=1024)
    merged = _sgu_merge(big, o_a, g_sgu, w_s[0].astype(BF16), b_s[0].T, w_o_gm[0].astype(BF16), tm=tm)

    w_router_p = jnp.concatenate(
        [w_router[0], jnp.zeros((d, LOGIT_LANES - N_EXPERTS), F32)], axis=1).astype(BF16)
    x1, h2, logits = _out_proj(merged, xc, xl, mod6, g_ffn, w_out[0].astype(BF16), w_router_p,
                               n_lat_seq=dec_seq, tm=tm)

    key_c, aff_c = _route(logits, row0=0, n_sets=batch, n=seq)
    key_l, aff_l = _route(logits, row0=n_ctx, n_sets=dec_batch, n=dec_seq)
    p_c, gate_c, xg_c = _dispatch_gather(key_c, aff_c, h2, row0=0, n_sets=batch, n=seq)
    p_l, gate_l, xg_l = _dispatch_gather(key_l, aff_l, h2, row0=n_ctx, n_sets=dec_batch, n=dec_seq)

    y_c, y_l = _experts(xg_c, xg_l, gate_c, gate_l, w_e1[0], w_e3[0], w_e2[0])

    y_prompt = _combine(p_c, y_c, x1, mod6, g_final[None, :], row0=0, n_sets=batch, n=seq,
                        mod_row_fn=lambda b: MOD_ROWS // 2)
    y_sample = _combine(p_l, y_l, x1, mod6, g_final[None, :], row0=n_ctx, n_sets=dec_batch, n=dec_seq,
                        mod_row_fn=lambda b: b)

    new_ckv = ckv[:n_ctx].reshape(batch, 1, seq, kv_lora)
    new_krope = kr[:n_ctx, :D_ROPE].reshape(batch, 1, seq, D_ROPE)
    return (y_prompt.reshape(batch, seq, d), y_sample.reshape(dec_batch, dec_seq, d), new_ckv, new_krope)
```

```python
import functools

import numpy as np
import jax
import jax.numpy as jnp
from jax import lax
from jax.experimental import pallas as pl
from jax.experimental.pallas import tpu as pltpu

F32 = jnp.float32
BF16 = jnp.bfloat16

N_HEADS = 16
D_NOPE = 128
D_ROPE = 64
D_V = 128
HEAD_SLOT = 256
ROPE_FREQS = D_ROPE // 4
ROPE_BASE = 10000.0
GRID_W = 64
CHUNK = 128
GM_GROUPS = 8
N_EXPERTS = 16
EC_FACTOR = 2
EPS = 1e-6
MOD_ROWS = 8
LOGIT_LANES = 128

VMEM_LIMIT_V7X = 56 * 1024 * 1024


def _params(n_axes):
    return pltpu.CompilerParams(
        dimension_semantics=("arbitrary",) * n_axes, vmem_limit_bytes=VMEM_LIMIT_V7X)


def _const_spec(shape):
    nd = len(shape)
    return pl.BlockSpec(shape, lambda *_: (0,) * nd, pipeline_mode=pl.Buffered(1))


def _half_tanh_half(x):
    return jnp.tanh(0.5 * x)


def _silu(x):
    h = 0.5 * x
    return h * jnp.tanh(h) + h


def _gelu_tanh(x):
    assert x.dtype == F32
    c = float(np.sqrt(2.0 / np.pi))
    return x * (0.5 * (1.0 + jnp.tanh(c * (x + 0.044715 * (x * x * x)))))


def _rms_rows(x, g):
    return x * lax.rsqrt(jnp.mean(x * x, axis=-1, keepdims=True) + EPS) * g


def _bdot(a, b):
    return jnp.dot(a, b, preferred_element_type=F32)


def _bdot_t(a, bt):
    return lax.dot_general(a, bt, (((1,), (1,)), ((), ())), preferred_element_type=F32)


def _mod_kernel(c_ref, w_ref, b_ref, o_ref):
    c = c_ref[...]
    s = _silu(c).astype(BF16)
    o_ref[...] = _bdot(s, w_ref[...].astype(BF16)) + b_ref[...]


def _modulation(c_rows, w_ada, b_ada, bn=1024):
    d, n = w_ada.shape
    return pl.pallas_call(
        _mod_kernel,
        grid=(n // bn,),
        in_specs=[
            _const_spec((MOD_ROWS, d)),
            pl.BlockSpec((d, bn), lambda j: (0, j)),
            pl.BlockSpec((1, bn), lambda j: (0, j)),
        ],
        out_specs=pl.BlockSpec((MOD_ROWS, bn), lambda j: (0, j)),
        out_shape=jax.ShapeDtypeStruct((MOD_ROWS, n), F32),
        compiler_params=_params(1),
        name="adaln_mod",
    )(c_rows, w_ada, b_ada)


def _cast_kernel(x_ref, o_ref, *, front_blocks):
    i = pl.program_id(0)

    @pl.when(i < front_blocks)
    def _():
        o_ref[...] = jnp.zeros(o_ref.shape, o_ref.dtype)

    @pl.when(i >= front_blocks)
    def _():
        o_ref[...] = x_ref[...].astype(o_ref.dtype)


def _cast_bf16(x, *, rows, front_blocks=0):
    r, c = x.shape
    assert r % rows == 0 and rows % 16 == 0
    nb = r // rows + front_blocks
    return pl.pallas_call(
        functools.partial(_cast_kernel, front_blocks=front_blocks),
        grid=(nb,),
        in_specs=[pl.BlockSpec((rows, c), lambda i: (jnp.maximum(i - front_blocks, 0), 0))],
        out_specs=pl.BlockSpec((rows, c), lambda i: (i, 0)),
        out_shape=jax.ShapeDtypeStruct((nb * rows, c), BF16),
        compiler_params=_params(1),
        name="cast_bf16",
    )(x)


def _preproj_kernel(xc_ref, xl_ref, g_ref, sc_ref, sh_ref, ws_ref, gqa_ref, gkv_ref, cos_ref, sin_ref,
                    wq_ref, wsw_ref, h_ref, ckv_ref, kr_ref, q_ref, *, n_ctx_tiles, q_lora, kv_lora, scale):
    i = pl.program_id(0)

    def body(x_ref, rotary):
        h = _rms_rows(x_ref[...], g_ref[...]) * (1.0 + sc_ref[...]) + sh_ref[...]
        hb = h.astype(BF16)
        h_ref[...] = hb
        small = _bdot_t(hb, ws_ref[...])
        qa = _rms_rows(small[:, :q_lora], gqa_ref[...]).astype(BF16)
        ckv_ref[...] = _rms_rows(small[:, q_lora:q_lora + kv_lora], gkv_ref[...])
        o = q_lora + kv_lora
        kr = small[:, o:o + 128]
        if not rotary:
            kr_ref[...] = kr
            for hd in range(N_HEADS):
                q = _bdot(qa, wq_ref[:, hd * HEAD_SLOT:(hd + 1) * HEAD_SLOT])
                q_ref[hd] = (q * scale).astype(BF16)
            return
        cos = cos_ref[...]
        sin = sin_ref[...]
        kr_sw = small[:, o + 128:o + 256]
        kr_ref[...] = kr * cos + kr_sw * sin
        for hp in range(N_HEADS // 2):
            q_sw = _bdot(qa, wsw_ref[:, hp * 256:(hp + 1) * 256])
            for s in range(2):
                hd = 2 * hp + s
                q = _bdot(qa, wq_ref[:, hd * HEAD_SLOT:(hd + 1) * HEAD_SLOT])
                q_ref[hd, :, 0:128] = (q[:, 0:128] * scale).astype(BF16)
                rot = q[:, 128:256] * cos + q_sw[:, s * 128:(s + 1) * 128] * sin
                q_ref[hd, :, 128:256] = (rot * scale).astype(BF16)

    @pl.when(i < n_ctx_tiles)
    def _():
        body(xc_ref, False)

    @pl.when(i >= n_ctx_tiles)
    def _():
        body(xl_ref, True)


def _pre_proj(xc, xl, g_attn, mod6, w_small_t, g_qa, g_kv, cos_t, sin_t, wq_p, wq_sw, *, tm):
    n_ctx, d = xc.shape
    n_lat_seq = cos_t.shape[0]
    t = n_ctx + xl.shape[0]
    q_lora, kv_lora = g_qa.shape[1], g_kv.shape[1]
    n_ctx_tiles = n_ctx // tm
    tiles_per_seq = n_lat_seq // tm

    def mod_row(i):
        return jnp.where(i < n_ctx_tiles, MOD_ROWS // 2, (i - n_ctx_tiles) // tiles_per_seq)

    def rope_blk(i):
        return jnp.maximum(i - n_ctx_tiles, 0) % tiles_per_seq

    scale = np.float32((D_NOPE + D_ROPE) ** -0.5)
    kern = functools.partial(_preproj_kernel, n_ctx_tiles=n_ctx_tiles, q_lora=q_lora, kv_lora=kv_lora,
                             scale=scale)
    return pl.pallas_call(
        kern,
        grid=(t // tm,),
        in_specs=[
            pl.BlockSpec((tm, d), lambda i: (jnp.minimum(i, n_ctx_tiles - 1), 0)),
            pl.BlockSpec((tm, d), lambda i: (jnp.maximum(i - n_ctx_tiles, 0), 0)),
            _const_spec((1, d)),
            pl.BlockSpec((None, None, 1, d), lambda i: (1, mod_row(i), 0, 0)),
            pl.BlockSpec((None, None, 1, d), lambda i: (0, mod_row(i), 0, 0)),
            _const_spec(w_small_t.shape),
            _const_spec((1, q_lora)),
            _const_spec((1, kv_lora)),
            pl.BlockSpec((tm, 128), lambda i: (rope_blk(i), 0)),
            pl.BlockSpec((tm, 128), lambda i: (rope_blk(i), 0)),
            _const_spec(wq_p.shape),
            _const_spec(wq_sw.shape),
        ],
        out_specs=[
            pl.BlockSpec((tm, d), lambda i: (i, 0)),
            pl.BlockSpec((tm, kv_lora), lambda i: (i, 0)),
            pl.BlockSpec((tm, 128), lambda i: (i, 0)),
            pl.BlockSpec((N_HEADS, tm, HEAD_SLOT), lambda i: (0, i, 0)),
        ],
        out_shape=[
            jax.ShapeDtypeStruct((t, d), BF16),
            jax.ShapeDtypeStruct((t, kv_lora), F32),
            jax.ShapeDtypeStruct((t, 128), F32),
            jax.ShapeDtypeStruct((N_HEADS, t, HEAD_SLOT), BF16),
        ],
        compiler_params=_params(1),
        name="pre_proj",
    )(xc, xl, g_attn, mod6, mod6, w_small_t, g_qa, g_kv, cos_t, sin_t, wq_p, wq_sw)


def _wideproj_kernel(h_ref, wa_ref, wb_ref, o_ref, *, tm, bn, n_gelu_steps):
    j = pl.program_id(0)
    h = h_ref[...]

    def run(act):
        for c, w_ref in enumerate((wa_ref, wb_ref)):
            y = _bdot_t(h, w_ref[...])
            for r0 in range(0, tm, 256):
                o_ref[r0:r0 + 256, c * bn:(c + 1) * bn] = act(y[r0:r0 + 256, :]).astype(BF16)

    @pl.when(j < n_gelu_steps)
    def _():
        run(_gelu_tanh)

    @pl.when(j >= n_gelu_steps)
    def _():
        run(lambda y: y)


def _wide_proj(h, w_all_t, *, n_big, n_gelu, tm, bn):
    t, d = h.shape
    blk0 = (w_all_t.shape[0] - n_big) // bn
    return pl.pallas_call(
        functools.partial(_wideproj_kernel, tm=tm, bn=bn, n_gelu_steps=n_gelu // (2 * bn)),
        grid=(n_big // (2 * bn), t // tm),
        in_specs=[
            pl.BlockSpec((tm, d), lambda j, i: (i, 0)),
            pl.BlockSpec((bn, d), lambda j, i: (blk0 + 2 * j, 0)),
            pl.BlockSpec((bn, d), lambda j, i: (blk0 + 2 * j + 1, 0)),
        ],
        out_specs=pl.BlockSpec((tm, 2 * bn), lambda j, i: (i, j)),
        out_shape=jax.ShapeDtypeStruct((t, n_big), BF16),
        compiler_params=_params(2),
        name="wide_proj",
    )(h, w_all_t, w_all_t)


def _attn_kernel(*refs, n_own, n_cache, heads_per_iter):
    if n_cache:
        (q_ref, ckv_ref, kr_ref, cckv_ref, ckr_ref, wuk_ref, wuv_ref,
         o_ref, kpad, vexp, kall, krp, o_scr) = refs
    else:
        (q_ref, ckv_ref, kr_ref, wuk_ref, wuv_ref, o_ref, kpad, vexp, kall, krp, o_scr) = refs
    qi = pl.program_id(1)

    @pl.when(qi == 0)
    def _():
        kall[0:n_own, :] = ckv_ref[...].astype(BF16)
        krp[0:n_own, :] = kr_ref[...].astype(BF16)
        if n_cache:
            kall[n_own:n_own + n_cache, :] = cckv_ref[...].astype(BF16)
            krp[n_own:n_own + n_cache, 0:D_ROPE] = ckr_ref[...].astype(BF16)
            krp[n_own:n_own + n_cache, D_ROPE:128] = jnp.zeros((n_cache, 128 - D_ROPE), BF16)

        def expand(hp, carry):
            kn = _bdot(kall[...], wuk_ref[hp]).astype(BF16)
            vv = _bdot(kall[...], wuv_ref[hp]).astype(BF16)
            for s in range(2):
                kpad[2 * hp + s, :, 0:128] = kn[:, s * 128:(s + 1) * 128]
                kpad[2 * hp + s, :, 128:256] = krp[...]
                vexp[2 * hp + s] = vv[:, s * 128:(s + 1) * 128]
            return carry

        if heads_per_iter == N_HEADS:
            for hp in range(N_HEADS // 2):
                expand(hp, 0)
        else:
            lax.fori_loop(0, N_HEADS // 2, expand, 0)

    def one_head(h):
        s = _bdot_t(q_ref[h], kpad[h])
        p = jnp.exp(s - jnp.max(s, axis=-1, keepdims=True))
        l = jnp.sum(p, axis=-1, keepdims=True)
        o = _bdot(p.astype(BF16), vexp[h])
        return (o / l).astype(BF16)

    if heads_per_iter == N_HEADS:
        for h in range(N_HEADS):
            o_ref[:, h * D_V:(h + 1) * D_V] = one_head(h)
    else:
        def head_group(hg, carry):
            for g in range(heads_per_iter):
                h = hg * heads_per_iter + g
                o_scr[h] = one_head(h)
            return carry

        lax.fori_loop(0, N_HEADS // heads_per_iter, head_group, 0)
        for h in range(N_HEADS):
            o_ref[:, h * D_V:(h + 1) * D_V] = o_scr[h]


def _attention(q_pad, ckv, kr, w_uk, w_uv, cache_ckv, cache_kr, *, row0, n_req, n_own, tq, heads_per_iter):
    kv_lora = ckv.shape[1]
    n_cache = 0 if cache_ckv is None else cache_ckv.shape[1]
    kn = n_own + n_cache
    qb = n_own // tq
    in_specs = [
        pl.BlockSpec((N_HEADS, tq, HEAD_SLOT), lambda b, qi: (0, row0 // tq + b * qb + qi, 0)),
        pl.BlockSpec((n_own, kv_lora), lambda b, qi: (row0 // n_own + b, 0)),
        pl.BlockSpec((n_own, 128), lambda b, qi: (row0 // n_own + b, 0)),
    ]
    args = [q_pad, ckv, kr]
    if n_cache:
        in_specs += [
            pl.BlockSpec((None, n_cache, kv_lora), lambda b, qi: (b, 0, 0)),
            pl.BlockSpec((None, n_cache, D_ROPE), lambda b, qi: (b, 0, 0)),
        ]
        args += [cache_ckv, cache_kr]
    in_specs += [_const_spec(w_uk.shape), _const_spec(w_uv.shape)]
    args += [w_uk, w_uv]
    return pl.pallas_call(
        functools.partial(_attn_kernel, n_own=n_own, n_cache=n_cache, heads_per_iter=heads_per_iter),
        grid=(n_req, qb),
        in_specs=in_specs,
        out_specs=pl.BlockSpec((tq, N_HEADS * D_V), lambda b, qi: (b * qb + qi, 0)),
        out_shape=jax.ShapeDtypeStruct((n_req * n_own, N_HEADS * D_V), BF16),
        scratch_shapes=[
            pltpu.VMEM((N_HEADS, kn, HEAD_SLOT), BF16),
            pltpu.VMEM((N_HEADS, kn, D_V), BF16),
            pltpu.VMEM((kn, kv_lora), BF16),
            pltpu.VMEM((kn, 128), BF16),
            pltpu.VMEM((N_HEADS, tq, D_V), BF16),
        ],
        compiler_params=_params(2),
        name="mla_attn_cache" if n_cache else "mla_attn",
    )(*args)


def _oproj_kernel(xc_ref, xl_ref, w_ref, o_ref, *, n_ctx_tiles):
    i = pl.program_id(0)

    @pl.when(i < n_ctx_tiles)
    def _():
        o_ref[...] = _bdot(xc_ref[...], w_ref[...]).astype(o_ref.dtype)

    @pl.when(i >= n_ctx_tiles)
    def _():
        o_ref[...] = _bdot(xl_ref[...], w_ref[...]).astype(o_ref.dtype)


def _o_proj(xc, xl, w, *, tm):
    n_ctx, k = xc.shape
    t = n_ctx + xl.shape[0]
    n = w.shape[1]
    n_ctx_tiles = n_ctx // tm
    return pl.pallas_call(
        functools.partial(_oproj_kernel, n_ctx_tiles=n_ctx_tiles),
        grid=(t // tm,),
        in_specs=[
            pl.BlockSpec((tm, k), lambda i: (jnp.minimum(i, n_ctx_tiles - 1), 0)),
            pl.BlockSpec((tm, k), lambda i: (jnp.maximum(i - n_ctx_tiles, 0), 0)),
            _const_spec(w.shape),
        ],
        out_specs=pl.BlockSpec((tm, n), lambda i: (i, 0)),
        out_shape=jax.ShapeDtypeStruct((t, n), BF16),
        compiler_params=_params(1),
        name="o_proj",
    )(xc, xl, w)


def _sgu_merge_kernel(gu_ref, gv_ref, ga_ref, gb_ref, oa_ref, g_ref, ws_ref, bs_ref, wo_ref, m_ref, z_scr,
                      *, tm, gc):
    s = pl.program_id(0)

    @pl.when(s == 0)
    def _():
        z_scr[1] = jnp.zeros(z_scr.shape[1:], BF16)

    o_b = _bdot(z_scr[(s + 1) % 2], wo_ref[...])
    for r0 in range(0, tm, CHUNK):
        rows = slice(r0, r0 + CHUNK)
        oa = oa_ref[rows, :]
        ob = o_b[rows, :].astype(BF16)
        ta = _half_tanh_half(ga_ref[rows, :])
        tb = _half_tanh_half(gb_ref[rows, :])
        m_ref[rows, :] = 0.5 * ((oa + ob) + (ta * oa + tb * ob))

    slot = s % 2
    g_bf = g_ref[...].astype(BF16)
    for r0 in range(0, tm, CHUNK):
        v = gv_ref[r0:r0 + CHUNK, :]
        v32 = v.astype(F32)
        inv = lax.rsqrt(jnp.mean(v32 * v32, axis=-1, keepdims=True) + EPS)
        vn = v * inv.astype(BF16) * g_bf
        for g in range(GM_GROUPS):
            c0 = g * gc
            mix = _bdot(ws_ref[g], vn[:, c0:c0 + gc]) + bs_ref[:, g:g + 1]
            u = gu_ref[r0:r0 + CHUNK, c0:c0 + gc]
            z_scr[slot, r0:r0 + CHUNK, c0:c0 + gc] = u * mix.astype(BF16)


def _sgu_merge(big, o_a, g_sgu, w_s, b_s_t, w_o_gm, *, tm):
    t = big.shape[0]
    width = g_sgu.shape[1]
    d = w_o_gm.shape[1]
    gc = width // GM_GROUPS
    nt = t // tm

    def fill(s):
        return jnp.minimum(s, nt - 1)

    def drain(s):
        return jnp.maximum(s - 1, 0)

    return pl.pallas_call(
        functools.partial(_sgu_merge_kernel, tm=tm, gc=gc),
        grid=(nt + 1,),
        in_specs=[
            pl.BlockSpec((tm, width), lambda s: (fill(s), 0)),
            pl.BlockSpec((tm, width), lambda s: (fill(s), 1)),
            pl.BlockSpec((tm, d), lambda s: (drain(s), 2)),
            pl.BlockSpec((tm, d), lambda s: (drain(s), 3)),
            pl.BlockSpec((tm, d), lambda s: (drain(s), 0)),
            _const_spec((1, width)),
            _const_spec(w_s.shape),
            _const_spec(b_s_t.shape),
            _const_spec(w_o_gm.shape),
        ],
        out_specs=pl.BlockSpec((tm, d), lambda s: (drain(s), 0)),
        out_shape=jax.ShapeDtypeStruct((t, d), BF16),
        scratch_shapes=[pltpu.VMEM((2, tm, width), BF16)],
        compiler_params=_params(1),
        name="sgu_merge",
    )(big, big, big, big, o_a, g_sgu, w_s, b_s_t, w_o_gm)


def _outproj_kernel(m_ref, xc_ref, xl_ref, gate_ref, sc_ref, sh_ref, g_ref, wo_ref, wr_ref,
                    x1_ref, h2_ref, lg_ref, *, tm, n_ctx_tiles):
    i = pl.program_id(0)

    def body(x_ref):
        half = tm // 2
        for b0 in range(0, tm, half):
            r = _bdot(m_ref[b0:b0 + half, :], wo_ref[...])
            for r0 in range(0, half, CHUNK):
                rows = slice(b0 + r0, b0 + r0 + CHUNK)
                x1 = x_ref[rows, :] + gate_ref[...] * r[r0:r0 + CHUNK, :]
                x1_ref[rows, :] = x1
                h2 = (_rms_rows(x1, g_ref[...]) * (1.0 + sc_ref[...]) + sh_ref[...]).astype(BF16)
                h2_ref[rows, :] = h2
                lg_ref[rows, :] = _bdot(h2, wr_ref[...])

    @pl.when(i < n_ctx_tiles)
    def _():
        body(xc_ref)

    @pl.when(i >= n_ctx_tiles)
    def _():
        body(xl_ref)


def _out_proj(merged, xc, xl, mod6, g_ffn, w_out, w_router_p, *, n_lat_seq, tm):
    n_ctx, d = xc.shape
    t = n_ctx + xl.shape[0]
    n_ctx_tiles = n_ctx // tm
    tiles_per_seq = n_lat_seq // tm

    def mod_row(i):
        return jnp.where(i < n_ctx_tiles, MOD_ROWS // 2, (i - n_ctx_tiles) // tiles_per_seq)

    return pl.pallas_call(
        functools.partial(_outproj_kernel, tm=tm, n_ctx_tiles=n_ctx_tiles),
        grid=(t // tm,),
        in_specs=[
            pl.BlockSpec((tm, d), lambda i: (i, 0)),
            pl.BlockSpec((tm, d), lambda i: (jnp.minimum(i, n_ctx_tiles - 1), 0)),
            pl.BlockSpec((tm, d), lambda i: (jnp.maximum(i - n_ctx_tiles, 0), 0)),
            pl.BlockSpec((None, None, 1, d), lambda i: (2, mod_row(i), 0, 0)),
            pl.BlockSpec((None, None, 1, d), lambda i: (4, mod_row(i), 0, 0)),
            pl.BlockSpec((None, None, 1, d), lambda i: (3, mod_row(i), 0, 0)),
            _const_spec((1, d)),
            _const_spec(w_out.shape),
            _const_spec(w_router_p.shape),
        ],
        out_specs=[
            pl.BlockSpec((tm, d), lambda i: (i, 0)),
            pl.BlockSpec((tm, d), lambda i: (i, 0)),
            pl.BlockSpec((tm, LOGIT_LANES), lambda i: (i, 0)),
        ],
        out_shape=[
            jax.ShapeDtypeStruct((t, d), F32),
            jax.ShapeDtypeStruct((t, d), BF16),
            jax.ShapeDtypeStruct((t, LOGIT_LANES), F32),
        ],
        compiler_params=_params(1),
        name="out_proj",
    )(merged, xc, xl, mod6, mod6, mod6, g_ffn, w_out, w_router_p)


BISECT_STEPS = 48
MIN_NORMAL_F32 = float(np.finfo(np.float32).tiny)


def _route_kernel(lg_ref, key_ref, aff_ref, tri_scr, *, n_sets, n, cap):
    for r0 in range(0, n, 128):
        r = lax.broadcasted_iota(jnp.int32, (128, n), 0) + r0
        c = lax.broadcasted_iota(jnp.int32, (128, n), 1)
        tri_scr[r0:r0 + 128, :] = jnp.where(r < c, 1.0, 0.0).astype(BF16)

    for s in range(n_sets):
        logits = lg_ref[s * n:(s + 1) * n, :].T[0:N_EXPERTS, :]
        e = jnp.exp(logits - jnp.max(logits, axis=0, keepdims=True))
        aff_ref[s * N_EXPERTS:(s + 1) * N_EXPERTS, :] = e / jnp.sum(e, axis=0, keepdims=True)
    aff = aff_ref[...]
    rows = n_sets * N_EXPERTS

    def count_ge(thr):
        return jnp.sum(jnp.where(aff >= thr, 1.0, 0.0), axis=1, keepdims=True)

    def bisect(_, carry):
        lo, hi = carry
        mid = jnp.sqrt(lo) * jnp.sqrt(hi)
        ok = count_ge(mid) >= cap
        return jnp.where(ok, mid, lo), jnp.where(ok, hi, mid)

    lo0 = jnp.full((rows, 1), MIN_NORMAL_F32, F32)
    hi0 = jnp.full((rows, 1), 2.0, F32)
    lo, hi = lax.fori_loop(0, BISECT_STEPS, bisect, (lo0, hi0))
    lo = jnp.where(count_ge(lo) >= cap, lo, 0.0)

    above = aff >= hi
    band = (aff >= lo) & jnp.logical_not(above)
    need = cap - jnp.sum(jnp.where(above, 1.0, 0.0), axis=1, keepdims=True)
    tri = tri_scr[...]
    band_before = _bdot(jnp.where(band, 1.0, 0.0).astype(BF16), tri)
    sel = above | (band & (band_before < need))
    pos = _bdot(jnp.where(sel, 1.0, 0.0).astype(BF16), tri)
    key_ref[...] = jnp.where(sel, pos, -1.0)


def _route(logits, *, row0, n_sets, n):
    cap = EC_FACTOR * n // N_EXPERTS
    rows = n_sets * N_EXPERTS
    blk = row0 // (n_sets * n)
    return pl.pallas_call(
        functools.partial(_route_kernel, n_sets=n_sets, n=n, cap=cap),
        grid=(1,),
        in_specs=[pl.BlockSpec((n_sets * n, LOGIT_LANES), lambda g: (blk, 0))],
        out_specs=[pl.BlockSpec((rows, n), lambda g: (0, 0)), pl.BlockSpec((rows, n), lambda g: (0, 0))],
        out_shape=[jax.ShapeDtypeStruct((rows, n), F32), jax.ShapeDtypeStruct((rows, n), F32)],
        scratch_shapes=[pltpu.VMEM((n, n), BF16)],
        compiler_params=_params(1),
        name=f"route_{n}",
    )(logits)


def _gather_kernel(key_ref, aff_ref, h2_ref, pt_ref, gate_ref, xg_ref, p_scr, *, sps, n, cap, d, nb):
    slot = lax.broadcasted_iota(jnp.int32, (cap, n), 0).astype(F32)
    per_group = 128 // cap
    for s in range(sps):
        key = key_ref[s * N_EXPERTS:(s + 1) * N_EXPERTS, :]
        aff = aff_ref[s * N_EXPERTS:(s + 1) * N_EXPERTS, :]
        for grp in range(N_EXPERTS // per_group):
            pieces = []
            for ex in range(grp * per_group, (grp + 1) * per_group):
                hit = slot == key[ex:ex + 1, :]
                gate_ref[ex, s] = jnp.sum(jnp.where(hit, aff[ex:ex + 1, :], 0.0), axis=1, keepdims=True)
                pieces.append(jnp.where(hit, 1.0, 0.0))
            hit128 = pieces[0] if per_group == 1 else jnp.concatenate(pieces, axis=0)
            p_scr[s, grp * 128:(grp + 1) * 128, :] = hit128.astype(BF16)
            pt_ref[s, :, grp * 128:(grp + 1) * 128] = hit128.T.astype(BF16)

        p = p_scr[s]
        for c in range(d // nb):
            xg = _bdot(p, h2_ref[s * n:(s + 1) * n, c * nb:(c + 1) * nb]).astype(BF16)
            for ex in range(N_EXPERTS):
                xg_ref[ex, s, :, c * nb:(c + 1) * nb] = xg[ex * cap:(ex + 1) * cap, :]


def _dispatch_gather(key, aff, h2, *, row0, n_sets, n, sps):
    d = h2.shape[1]
    cap = EC_FACTOR * n // N_EXPERTS
    slots = N_EXPERTS * cap
    blk0 = row0 // (sps * n)
    return pl.pallas_call(
        functools.partial(_gather_kernel, sps=sps, n=n, cap=cap, d=d, nb=256),
        grid=(n_sets // sps,),
        in_specs=[
            pl.BlockSpec((sps * N_EXPERTS, n), lambda b: (b, 0)),
            pl.BlockSpec((sps * N_EXPERTS, n), lambda b: (b, 0)),
            pl.BlockSpec((sps * n, d), lambda b: (blk0 + b, 0)),
        ],
        out_specs=[
            pl.BlockSpec((sps, n, slots), lambda b: (b, 0, 0)),
            pl.BlockSpec((N_EXPERTS, sps, cap, 1), lambda b: (0, b, 0, 0)),
            pl.BlockSpec((N_EXPERTS, sps, cap, d), lambda b: (0, b, 0, 0)),
        ],
        out_shape=[
            jax.ShapeDtypeStruct((n_sets, n, slots), BF16),
            jax.ShapeDtypeStruct((N_EXPERTS, n_sets, cap, 1), F32),
            jax.ShapeDtypeStruct((N_EXPERTS, n_sets, cap, d), BF16),
        ],
        scratch_shapes=[pltpu.VMEM((sps, slots, n), BF16)],
        compiler_params=_params(1),
        name=f"dispatch_gather_{n}",
    )(key, aff, h2)


def _expert_kernel(xc_ref, xl_ref, gc_ref, gl_ref, w1_ref, w3_ref, w2_ref, yc_ref, yl_ref, hc_scr, hl_scr,
                   *, n_f, fc, rows_c, rows_l, d):
    k = pl.program_id(1)

    @pl.when(k < n_f)
    def _():
        w1 = w1_ref[...].astype(BF16)
        w3 = w3_ref[...].astype(BF16)
        for x_ref, h_scr, rows in ((xc_ref, hc_scr, rows_c), (xl_ref, hl_scr, rows_l)):
            x = x_ref[...].reshape(rows, d)
            a = _bdot(x, w1)
            g = _bdot(x, w3)
            h_scr[k] = (_silu(a) * g).astype(BF16)

    @pl.when(k >= n_f)
    def _():
        w2 = w2_ref[...].astype(BF16)
        for h_scr, g_ref, y_ref, rows in ((hc_scr, gc_ref, yc_ref, rows_c), (hl_scr, gl_ref, yl_ref, rows_l)):
            y = _bdot(h_scr[0], w2[0:fc, :])
            for kk in range(1, n_f):
                y = y + _bdot(h_scr[kk], w2[kk * fc:(kk + 1) * fc, :])
            y_ref[...] = (y * g_ref[...].reshape(rows, 1)).astype(BF16).reshape(y_ref.shape)


def _experts(xg_c, xg_l, gate_c, gate_l, w_e1, w_e3, w_e2, *, fc=512, nc=512):
    n_e, sets_c, cap_c, d = xg_c.shape
    _, sets_l, cap_l, _ = xg_l.shape
    d_ff = w_e1.shape[2]
    n_f = d_ff // fc
    n_c = d // nc
    rows_c, rows_l = sets_c * cap_c, sets_l * cap_l

    def e_in(e, k):
        return jnp.minimum(e + (k >= n_f).astype(jnp.int32), n_e - 1)

    def f_idx(k):
        return jnp.where(k < n_f, k, 0)

    def c_idx(k):
        return jnp.maximum(k - n_f, 0)

    kern = functools.partial(_expert_kernel, n_f=n_f, fc=fc, rows_c=rows_c, rows_l=rows_l, d=d)
    return pl.pallas_call(
        kern,
        grid=(n_e, n_f + n_c),
        in_specs=[
            pl.BlockSpec((None, sets_c, cap_c, d), lambda e, k: (e_in(e, k), 0, 0, 0)),
            pl.BlockSpec((None, sets_l, cap_l, d), lambda e, k: (e_in(e, k), 0, 0, 0)),
            pl.BlockSpec((None, sets_c, cap_c, 1), lambda e, k: (e, 0, 0, 0)),
            pl.BlockSpec((None, sets_l, cap_l, 1), lambda e, k: (e, 0, 0, 0)),
            pl.BlockSpec((None, d, fc), lambda e, k: (e_in(e, k), 0, f_idx(k))),
            pl.BlockSpec((None, d, fc), lambda e, k: (e_in(e, k), 0, f_idx(k))),
            pl.BlockSpec((None, d_ff, nc), lambda e, k: (e, 0, c_idx(k))),
        ],
        out_specs=[
            pl.BlockSpec((None, sets_c, cap_c, nc), lambda e, k: (e, 0, 0, c_idx(k))),
            pl.BlockSpec((None, sets_l, cap_l, nc), lambda e, k: (e, 0, 0, c_idx(k))),
        ],
        out_shape=[
            jax.ShapeDtypeStruct(xg_c.shape, BF16),
            jax.ShapeDtypeStruct(xg_l.shape, BF16),
        ],
        scratch_shapes=[pltpu.VMEM((n_f, rows_c, fc), BF16), pltpu.VMEM((n_f, rows_l, fc), BF16)],
        compiler_params=_params(2),
        name="experts",
    )(xg_c, xg_l, gate_c, gate_l, w_e1, w_e3, w_e2)


def _combine_kernel(pt_ref, y_ref, x1_ref, gate_ref, g_ref, o_ref, acc_scr, *, sps, tn, slots, d, nb):
    for s in range(sps):
        pt = pt_ref[s]
        rows = slice(s * tn, (s + 1) * tn)
        for c in range(d // nb):
            cols = slice(c * nb, (c + 1) * nb)
            moe = _bdot(pt, y_ref[:, s, :, cols].reshape(slots, nb))
            acc_scr[:, cols] = x1_ref[rows, cols] + gate_ref[:, cols] * moe
        o_ref[rows, :] = _rms_rows(acc_scr[...], g_ref[...])


def _combine(pt, y, x1, mod6, g_final, *, row0, n_sets, n, sps, mod_row_fn):
    d = x1.shape[1]
    cap = EC_FACTOR * n // N_EXPERTS
    slots = N_EXPERTS * cap
    tn = min(n, 512)
    nt = n // tn
    assert sps == 1 or nt == 1
    blk0 = row0 // (sps * tn)
    return pl.pallas_call(
        functools.partial(_combine_kernel, sps=sps, tn=tn, slots=slots, d=d, nb=512),
        grid=(n_sets // sps, nt),
        in_specs=[
            pl.BlockSpec((sps, tn, slots), lambda b, r: (b, r, 0)),
            pl.BlockSpec((N_EXPERTS, sps, cap, d), lambda b, r: (0, b, 0, 0)),
            pl.BlockSpec((sps * tn, d), lambda b, r: (blk0 + b * nt + r, 0)),
            pl.BlockSpec((None, None, 1, d), lambda b, r: (5, mod_row_fn(b), 0, 0)),
            _const_spec((1, d)),
        ],
        out_specs=pl.BlockSpec((sps * tn, d), lambda b, r: (b * nt + r, 0)),
        out_shape=jax.ShapeDtypeStruct((n_sets * n, d), F32),
        scratch_shapes=[pltpu.VMEM((tn, d), F32)],
        compiler_params=_params(2),
        name=f"combine_{n}",
    )(pt, y, x1, mod6, g_final)


def _rope_tables(n):
    tpos = jnp.arange(n, dtype=jnp.int32)
    row = (tpos // GRID_W).astype(F32)
    col = (tpos % GRID_W).astype(F32)
    inv = 1.0 / (ROPE_BASE ** (jnp.arange(ROPE_FREQS, dtype=F32) / ROPE_FREQS))
    ang = jnp.stack([row[:, None] * inv, col[:, None] * inv], axis=1)
    cos = jnp.cos(ang)[:, :, None, :]
    sin = jnp.sin(ang)[:, :, None, :]
    cos = jnp.broadcast_to(cos, (n, 2, 2, ROPE_FREQS)).reshape(n, D_ROPE)
    sin = jnp.concatenate([-sin, sin], axis=2).reshape(n, D_ROPE)
    pad = jnp.zeros((n, 128 - D_ROPE), F32)
    return jnp.concatenate([cos, pad], axis=1), jnp.concatenate([sin, pad], axis=1)


def _swap_rotary_halves(w):
    perm = np.arange(D_ROPE) ^ ROPE_FREQS
    return w[..., perm]


def kernel(x_prompt, x_sample, c, cache_ckv, cache_krope, c_ctx, g_attn, g_ffn, w_ada, b_ada, w_in, g_qa,
           w_qb, g_kv, w_uk, w_uv, w_o_mla, g_sgu, w_s, b_s, w_o_gm, w_out, w_router, w_e1, w_e3, w_e2,
           g_final):
    batch, seq, d = x_prompt.shape
    dec_batch, dec_seq, _ = x_sample.shape
    depth = g_attn.shape[0]
    assert depth == 1
    q_lora, kv_lora = g_qa.shape[1], g_kv.shape[1]
    gm_width = g_sgu.shape[1]
    n_ctx, n_lat = batch * seq, dec_batch * dec_seq
    assert dec_batch < MOD_ROWS // 2 + 1

    xc = x_prompt.reshape(n_ctx, d)
    xl = x_sample.reshape(n_lat, d)

    c_rows = jnp.zeros((MOD_ROWS, d), F32).at[:dec_batch].set(c).at[MOD_ROWS // 2].set(c_ctx)
    mod = _modulation(c_rows, w_ada[0], b_ada[0][None, :])
    mod6 = mod.reshape(MOD_ROWS, 6, 1, d).transpose(1, 0, 2, 3)

    cos_t, sin_t = _rope_tables(dec_seq)

    w_in_t = w_in[0].T
    o_kr = q_lora + kv_lora
    w_kr_t = w_in_t[o_kr:o_kr + D_ROPE]
    zpad = jnp.zeros((128 - D_ROPE, d), F32)
    w_small_t = jnp.concatenate(
        [w_in_t[:o_kr], w_kr_t, zpad, w_kr_t[np.arange(D_ROPE) ^ ROPE_FREQS], zpad], axis=0).astype(BF16)
    tm, bn_in = 512, 1024
    front = -(o_kr + D_ROPE) % bn_in
    w_all_t = _cast_bf16(w_in_t, rows=front, front_blocks=1)

    wq3 = w_qb[0].reshape(q_lora, N_HEADS, D_NOPE + D_ROPE)
    wq_nope, wq_rope = wq3[:, :, :D_NOPE], wq3[:, :, D_NOPE:]
    zq = jnp.zeros((q_lora, N_HEADS, HEAD_SLOT - D_NOPE - D_ROPE), F32)
    wq_p = jnp.concatenate([wq_nope, wq_rope, zq], axis=2).reshape(q_lora, N_HEADS * HEAD_SLOT).astype(BF16)
    wq_sw = jnp.concatenate([_swap_rotary_halves(wq_rope), zq], axis=2).reshape(
        q_lora, N_HEADS * 128).astype(BF16)

    h, ckv, kr, q_pad = _pre_proj(xc, xl, g_attn, mod6, w_small_t, g_qa, g_kv, cos_t, sin_t, wq_p, wq_sw,
                                  tm=tm)
    big = _wide_proj(h, w_all_t, n_big=w_in_t.shape[0] - o_kr - D_ROPE, n_gelu=2 * gm_width, tm=1024,
                     bn=bn_in)

    w_uk2 = w_uk[0].reshape(kv_lora, N_HEADS // 2, 2 * D_NOPE).transpose(1, 0, 2).astype(BF16)
    w_uv2 = w_uv[0].reshape(kv_lora, N_HEADS // 2, 2 * D_V).transpose(1, 0, 2).astype(BF16)
    o_ctx = _attention(q_pad, ckv, kr, w_uk2, w_uv2, None, None,
                       row0=0, n_req=batch, n_own=seq, tq=seq, heads_per_iter=N_HEADS)
    o_lat = _attention(q_pad, ckv, kr, w_uk2, w_uv2, cache_ckv[:, 0], cache_krope[:, 0],
                       row0=n_ctx, n_req=dec_batch, n_own=dec_seq, tq=512, heads_per_iter=4)

    o_a = _o_proj(o_ctx, o_lat, w_o_mla[0].astype(BF16), tm=1024)
    merged = _sgu_merge(big, o_a, g_sgu, w_s[0].astype(BF16), b_s[0].T, w_o_gm[0].astype(BF16), tm=tm)

    w_router_p = jnp.concatenate(
        [w_router[0], jnp.zeros((d, LOGIT_LANES - N_EXPERTS), F32)], axis=1).astype(BF16)
    x1, h2, logits = _out_proj(merged, xc, xl, mod6, g_ffn, w_out[0].astype(BF16), w_router_p,
                               n_lat_seq=dec_seq, tm=tm)

    key_c, aff_c = _route(logits, row0=0, n_sets=batch, n=seq)
    key_l, aff_l = _route(logits, row0=n_ctx, n_sets=dec_batch, n=dec_seq)
    p_c, gate_c, xg_c = _dispatch_gather(key_c, aff_c, h2, row0=0, n_sets=batch, n=seq, sps=4)
    p_l, gate_l, xg_l = _dispatch_gather(key_l, aff_l, h2, row0=n_ctx, n_sets=dec_batch, n=dec_seq, sps=1)

    y_c, y_l = _experts(xg_c, xg_l, gate_c, gate_l, w_e1[0], w_e3[0], w_e2[0])

    y_prompt = _combine(p_c, y_c, x1, mod6, g_final[None, :], row0=0, n_sets=batch, n=seq, sps=2,
                        mod_row_fn=lambda b: MOD_ROWS // 2)
    y_sample = _combine(p_l, y_l, x1, mod6, g_final[None, :], row0=n_ctx, n_sets=dec_batch, n=dec_seq, sps=1,
                        mod_row_fn=lambda b: b)

    new_ckv = ckv[:n_ctx].reshape(batch, 1, seq, kv_lora)
    new_krope = kr[:n_ctx, :D_ROPE].reshape(batch, 1, seq, D_ROPE)
    return (y_prompt.reshape(batch, seq, d), y_sample.reshape(dec_batch, dec_seq, d), new_ckv, new_krope)
```

```python
import functools

import numpy as np
import jax
import jax.numpy as jnp
from jax import lax
from jax.experimental import pallas as pl
from jax.experimental.pallas import tpu as pltpu

F32 = jnp.float32
BF16 = jnp.bfloat16

N_HEADS = 16
D_NOPE = 128
D_ROPE = 64
D_V = 128
HEAD_SLOT = 256
ROPE_FREQS = D_ROPE // 4
ROPE_BASE = 10000.0
GRID_W = 64
CHUNK = 128
GM_GROUPS = 8
N_EXPERTS = 16
EC_FACTOR = 2
EPS = 1e-6
MOD_ROWS = 8
LOGIT_LANES = 128

VMEM_LIMIT_V7X = 56 * 1024 * 1024


def _params(n_axes):
    return pltpu.CompilerParams(
        dimension_semantics=("arbitrary",) * n_axes, vmem_limit_bytes=VMEM_LIMIT_V7X)


def _const_spec(shape):
    nd = len(shape)
    return pl.BlockSpec(shape, lambda *_: (0,) * nd, pipeline_mode=pl.Buffered(1))


def _half_tanh_half(x):
    return jnp.tanh(0.5 * x)


def _silu(x):
    h = 0.5 * x
    return h * jnp.tanh(h) + h


def _gelu_tanh(x):
    assert x.dtype == F32
    c = float(np.sqrt(2.0 / np.pi))
    return x * (0.5 * (1.0 + jnp.tanh(c * (x + 0.044715 * (x * x * x)))))


def _rms_rows(x, g):
    return x * lax.rsqrt(jnp.mean(x * x, axis=-1, keepdims=True) + EPS) * g


def _bdot(a, b):
    return jnp.dot(a, b, preferred_element_type=F32)


def _bdot_t(a, bt):
    return lax.dot_general(a, bt, (((1,), (1,)), ((), ())), preferred_element_type=F32)


def _mod_kernel(c_ref, w_ref, b_ref, o_ref):
    c = c_ref[...]
    s = _silu(c).astype(BF16)
    o_ref[...] = _bdot(s, w_ref[...].astype(BF16)) + b_ref[...]


def _modulation(c_rows, w_ada, b_ada, bn=1024):
    d, n = w_ada.shape
    return pl.pallas_call(
        _mod_kernel,
        grid=(n // bn,),
        in_specs=[
            _const_spec((MOD_ROWS, d)),
            pl.BlockSpec((d, bn), lambda j: (0, j)),
            pl.BlockSpec((1, bn), lambda j: (0, j)),
        ],
        out_specs=pl.BlockSpec((MOD_ROWS, bn), lambda j: (0, j)),
        out_shape=jax.ShapeDtypeStruct((MOD_ROWS, n), F32),
        compiler_params=_params(1),
        name="adaln_mod",
    )(c_rows, w_ada, b_ada)


def _preproj_kernel(xc_ref, xl_ref, g_ref, sc_ref, sh_ref, ws_ref, gqa_ref, gkv_ref, cos_ref, sin_ref,
                    wq_ref, wsw_ref, wlo_ref, whi_ref, h_ref, ckv_ref, kr_ref, q_ref, wcast_ref,
                    *, n_ctx_tiles, q_lora, kv_lora, scale):
    i = pl.program_id(0)

    rb = wlo_ref.shape[0]
    lo = wlo_ref[...].astype(BF16)
    wcast_ref[0:rb, :] = jnp.where(i == 0, jnp.zeros_like(lo), lo)
    wcast_ref[rb:2 * rb, :] = whi_ref[...].astype(BF16)

    def body(x_ref, rotary):
        h = _rms_rows(x_ref[...], g_ref[...]) * (1.0 + sc_ref[...]) + sh_ref[...]
        hb = h.astype(BF16)
        h_ref[...] = hb
        small = _bdot_t(hb, ws_ref[...])
        qa = _rms_rows(small[:, :q_lora], gqa_ref[...]).astype(BF16)
        ckv_ref[...] = _rms_rows(small[:, q_lora:q_lora + kv_lora], gkv_ref[...])
        o = q_lora + kv_lora
        kr = small[:, o:o + 128]
        if not rotary:
            kr_ref[...] = kr
            for hd in range(N_HEADS):
                q = _bdot(qa, wq_ref[:, hd * HEAD_SLOT:(hd + 1) * HEAD_SLOT])
                q_ref[hd] = (q * scale).astype(BF16)
            return
        cos = cos_ref[...]
        sin = sin_ref[...]
        kr_sw = small[:, o + 128:o + 256]
        kr_ref[...] = kr * cos + kr_sw * sin
        for hp in range(N_HEADS // 2):
            q_sw = _bdot(qa, wsw_ref[:, hp * 256:(hp + 1) * 256])
            for s in range(2):
                hd = 2 * hp + s
                q = _bdot(qa, wq_ref[:, hd * HEAD_SLOT:(hd + 1) * HEAD_SLOT])
                q_ref[hd, :, 0:128] = (q[:, 0:128] * scale).astype(BF16)
                rot = q[:, 128:256] * cos + q_sw[:, s * 128:(s + 1) * 128] * sin
                q_ref[hd, :, 128:256] = (rot * scale).astype(BF16)

    @pl.when(i < n_ctx_tiles)
    def _():
        body(xc_ref, False)

    @pl.when(i >= n_ctx_tiles)
    def _():
        body(xl_ref, True)


def _pre_proj(xc, xl, g_attn, mod6, w_small_t, g_qa, g_kv, cos_t, sin_t, wq_p, wq_sw, w_in_t, *, front, tm):
    n_ctx, d = xc.shape
    n_lat_seq = cos_t.shape[0]
    t = n_ctx + xl.shape[0]
    q_lora, kv_lora = g_qa.shape[1], g_kv.shape[1]
    n_ctx_tiles = n_ctx // tm
    tiles_per_seq = n_lat_seq // tm
    n_wblk = w_in_t.shape[0] // front
    assert w_in_t.shape[0] == n_wblk * front and front % 16 == 0
    assert n_wblk + 1 == 2 * (t // tm), "weight row blocks must pair up with the token steps"

    def mod_row(i):
        return jnp.where(i < n_ctx_tiles, MOD_ROWS // 2, (i - n_ctx_tiles) // tiles_per_seq)

    def rope_blk(i):
        return jnp.maximum(i - n_ctx_tiles, 0) % tiles_per_seq

    scale = np.float32((D_NOPE + D_ROPE) ** -0.5)
    kern = functools.partial(_preproj_kernel, n_ctx_tiles=n_ctx_tiles, q_lora=q_lora, kv_lora=kv_lora,
                             scale=scale)
    return pl.pallas_call(
        kern,
        grid=(t // tm,),
        in_specs=[
            pl.BlockSpec((tm, d), lambda i: (jnp.minimum(i, n_ctx_tiles - 1), 0)),
            pl.BlockSpec((tm, d), lambda i: (jnp.maximum(i - n_ctx_tiles, 0), 0)),
            _const_spec((1, d)),
            pl.BlockSpec((None, None, 1, d), lambda i: (1, mod_row(i), 0, 0)),
            pl.BlockSpec((None, None, 1, d), lambda i: (0, mod_row(i), 0, 0)),
            _const_spec(w_small_t.shape),
            _const_spec((1, q_lora)),
            _const_spec((1, kv_lora)),
            pl.BlockSpec((tm, 128), lambda i: (rope_blk(i), 0)),
            pl.BlockSpec((tm, 128), lambda i: (rope_blk(i), 0)),
            _const_spec(wq_p.shape),
            _const_spec(wq_sw.shape),
            pl.BlockSpec((front, d), lambda i: (jnp.maximum(2 * i - 1, 0), 0)),
            pl.BlockSpec((front, d), lambda i: (2 * i, 0)),
        ],
        out_specs=[
            pl.BlockSpec((tm, d), lambda i: (i, 0)),
            pl.BlockSpec((tm, kv_lora), lambda i: (i, 0)),
            pl.BlockSpec((tm, 128), lambda i: (i, 0)),
            pl.BlockSpec((N_HEADS, tm, HEAD_SLOT), lambda i: (0, i, 0)),
            pl.BlockSpec((2 * front, d), lambda i: (i, 0)),
        ],
        out_shape=[
            jax.ShapeDtypeStruct((t, d), BF16),
            jax.ShapeDtypeStruct((t, kv_lora), F32),
            jax.ShapeDtypeStruct((t, 128), F32),
            jax.ShapeDtypeStruct((N_HEADS, t, HEAD_SLOT), BF16),
            jax.ShapeDtypeStruct(((n_wblk + 1) * front, d), BF16),
        ],
        compiler_params=_params(1),
        name="pre_proj",
    )(xc, xl, g_attn, mod6, mod6, w_small_t, g_qa, g_kv, cos_t, sin_t, wq_p, wq_sw, w_in_t, w_in_t)


def _wideproj_kernel(h_ref, wa_ref, wb_ref, o_ref, *, tm, bn, n_gelu_steps):
    j = pl.program_id(0)
    h = h_ref[...]

    def run(act):
        for c, w_ref in enumerate((wa_ref, wb_ref)):
            y = _bdot_t(h, w_ref[...])
            for r0 in range(0, tm, 256):
                o_ref[r0:r0 + 256, c * bn:(c + 1) * bn] = act(y[r0:r0 + 256, :]).astype(BF16)

    @pl.when(j < n_gelu_steps)
    def _():
        run(_gelu_tanh)

    @pl.when(j >= n_gelu_steps)
    def _():
        run(lambda y: y)


def _wide_proj(h, w_all_t, *, n_big, n_gelu, tm, bn):
    t, d = h.shape
    blk0 = (w_all_t.shape[0] - n_big) // bn
    return pl.pallas_call(
        functools.partial(_wideproj_kernel, tm=tm, bn=bn, n_gelu_steps=n_gelu // (2 * bn)),
        grid=(n_big // (2 * bn), t // tm),
        in_specs=[
            pl.BlockSpec((tm, d), lambda j, i: (i, 0)),
            pl.BlockSpec((bn, d), lambda j, i: (blk0 + 2 * j, 0)),
            pl.BlockSpec((bn, d), lambda j, i: (blk0 + 2 * j + 1, 0)),
        ],
        out_specs=pl.BlockSpec((tm, 2 * bn), lambda j, i: (i, j)),
        out_shape=jax.ShapeDtypeStruct((t, n_big), BF16),
        compiler_params=_params(2),
        name="wide_proj",
    )(h, w_all_t, w_all_t)


def _attn_kernel(*refs, n_own, n_cache, heads_per_iter):
    if n_cache:
        (q_ref, ckv_ref, kr_ref, cckv_ref, ckr_ref, wuk_ref, wuv_ref,
         o_ref, kpad, vexp, kall, krp, o_scr) = refs
    else:
        (q_ref, ckv_ref, kr_ref, wuk_ref, wuv_ref, o_ref, kpad, vexp, kall, krp, o_scr) = refs
    qi = pl.program_id(1)

    @pl.when(qi == 0)
    def _():
        kall[0:n_own, :] = ckv_ref[...].astype(BF16)
        krp[0:n_own, :] = kr_ref[...].astype(BF16)
        if n_cache:
            kall[n_own:n_own + n_cache, :] = cckv_ref[...].astype(BF16)
            krp[n_own:n_own + n_cache, 0:D_ROPE] = ckr_ref[...].astype(BF16)
            krp[n_own:n_own + n_cache, D_ROPE:128] = jnp.zeros((n_cache, 128 - D_ROPE), BF16)

        def expand(hp, carry):
            kn = _bdot(kall[...], wuk_ref[hp]).astype(BF16)
            vv = _bdot(kall[...], wuv_ref[hp]).astype(BF16)
            for s in range(2):
                kpad[2 * hp + s, :, 0:128] = kn[:, s * 128:(s + 1) * 128]
                kpad[2 * hp + s, :, 128:256] = krp[...]
                vexp[2 * hp + s] = vv[:, s * 128:(s + 1) * 128]
            return carry

        if heads_per_iter == N_HEADS:
            for hp in range(N_HEADS // 2):
                expand(hp, 0)
        else:
            lax.fori_loop(0, N_HEADS // 2, expand, 0)

    def one_head(h):
        s = _bdot_t(q_ref[h], kpad[h])
        p = jnp.exp(s - jnp.max(s, axis=-1, keepdims=True))
        l = jnp.sum(p, axis=-1, keepdims=True)
        o = _bdot(p.astype(BF16), vexp[h])
        return (o / l).astype(BF16)

    if heads_per_iter == N_HEADS:
        for h in range(N_HEADS):
            o_ref[:, h * D_V:(h + 1) * D_V] = one_head(h)
    else:
        def head_group(hg, carry):
            for g in range(heads_per_iter):
                h = hg * heads_per_iter + g
                o_scr[h] = one_head(h)
            return carry

        lax.fori_loop(0, N_HEADS // heads_per_iter, head_group, 0)
        for h in range(N_HEADS):
            o_ref[:, h * D_V:(h + 1) * D_V] = o_scr[h]


def _attention(q_pad, ckv, kr, w_uk, w_uv, cache_ckv, cache_kr, *, row0, n_req, n_own, tq, heads_per_iter):
    kv_lora = ckv.shape[1]
    n_cache = 0 if cache_ckv is None else cache_ckv.shape[1]
    kn = n_own + n_cache
    qb = n_own // tq
    in_specs = [
        pl.BlockSpec((N_HEADS, tq, HEAD_SLOT), lambda b, qi: (0, row0 // tq + b * qb + qi, 0)),
        pl.BlockSpec((n_own, kv_lora), lambda b, qi: (row0 // n_own + b, 0)),
        pl.BlockSpec((n_own, 128), lambda b, qi: (row0 // n_own + b, 0)),
    ]
    args = [q_pad, ckv, kr]
    if n_cache:
        in_specs += [
            pl.BlockSpec((None, n_cache, kv_lora), lambda b, qi: (b, 0, 0)),
            pl.BlockSpec((None, n_cache, D_ROPE), lambda b, qi: (b, 0, 0)),
        ]
        args += [cache_ckv, cache_kr]
    in_specs += [_const_spec(w_uk.shape), _const_spec(w_uv.shape)]
    args += [w_uk, w_uv]
    return pl.pallas_call(
        functools.partial(_attn_kernel, n_own=n_own, n_cache=n_cache, heads_per_iter=heads_per_iter),
        grid=(n_req, qb),
        in_specs=in_specs,
        out_specs=pl.BlockSpec((tq, N_HEADS * D_V), lambda b, qi: (b * qb + qi, 0)),
        out_shape=jax.ShapeDtypeStruct((n_req * n_own, N_HEADS * D_V), BF16),
        scratch_shapes=[
            pltpu.VMEM((N_HEADS, kn, HEAD_SLOT), BF16),
            pltpu.VMEM((N_HEADS, kn, D_V), BF16),
            pltpu.VMEM((kn, kv_lora), BF16),
            pltpu.VMEM((kn, 128), BF16),
            pltpu.VMEM((N_HEADS, tq, D_V), BF16),
        ],
        compiler_params=_params(2),
        name="mla_attn_cache" if n_cache else "mla_attn",
    )(*args)


def _oproj_kernel(xc_ref, xl_ref, w_ref, o_ref, *, n_ctx_tiles):
    i = pl.program_id(0)

    @pl.when(i < n_ctx_tiles)
    def _():
        o_ref[...] = _bdot(xc_ref[...], w_ref[...]).astype(o_ref.dtype)

    @pl.when(i >= n_ctx_tiles)
    def _():
        o_ref[...] = _bdot(xl_ref[...], w_ref[...]).astype(o_ref.dtype)


def _o_proj(xc, xl, w, *, tm):
    n_ctx, k = xc.shape
    t = n_ctx + xl.shape[0]
    n = w.shape[1]
    n_ctx_tiles = n_ctx // tm
    return pl.pallas_call(
        functools.partial(_oproj_kernel, n_ctx_tiles=n_ctx_tiles),
        grid=(t // tm,),
        in_specs=[
            pl.BlockSpec((tm, k), lambda i: (jnp.minimum(i, n_ctx_tiles - 1), 0)),
            pl.BlockSpec((tm, k), lambda i: (jnp.maximum(i - n_ctx_tiles, 0), 0)),
            _const_spec(w.shape),
        ],
        out_specs=pl.BlockSpec((tm, n), lambda i: (i, 0)),
        out_shape=jax.ShapeDtypeStruct((t, n), BF16),
        compiler_params=_params(1),
        name="o_proj",
    )(xc, xl, w)


def _sgu_merge_kernel(gu_ref, gv_ref, ga_ref, gb_ref, oa_ref, g_ref, ws_ref, bs_ref, wo_ref, m_ref, z_scr,
                      *, tm, gc):
    s = pl.program_id(0)

    @pl.when(s == 0)
    def _():
        z_scr[1] = jnp.zeros(z_scr.shape[1:], BF16)

    o_b = _bdot(z_scr[(s + 1) % 2], wo_ref[...])
    for r0 in range(0, tm, CHUNK):
        rows = slice(r0, r0 + CHUNK)
        oa = oa_ref[rows, :]
        ob = o_b[rows, :].astype(BF16)
        ta = _half_tanh_half(ga_ref[rows, :])
        tb = _half_tanh_half(gb_ref[rows, :])
        m_ref[rows, :] = 0.5 * ((oa + ob) + (ta * oa + tb * ob))

    slot = s % 2
    g_bf = g_ref[...].astype(BF16)
    for r0 in range(0, tm, CHUNK):
        v = gv_ref[r0:r0 + CHUNK, :]
        v32 = v.astype(F32)
        inv = lax.rsqrt(jnp.mean(v32 * v32, axis=-1, keepdims=True) + EPS)
        vn = v * inv.astype(BF16) * g_bf
        for g in range(GM_GROUPS):
            c0 = g * gc
            mix = _bdot(ws_ref[g], vn[:, c0:c0 + gc]) + bs_ref[:, g:g + 1]
            u = gu_ref[r0:r0 + CHUNK, c0:c0 + gc]
            z_scr[slot, r0:r0 + CHUNK, c0:c0 + gc] = u * mix.astype(BF16)


def _sgu_merge(big, o_a, g_sgu, w_s, b_s_t, w_o_gm, *, tm):
    t = big.shape[0]
    width = g_sgu.shape[1]
    d = w_o_gm.shape[1]
    gc = width // GM_GROUPS
    nt = t // tm

    def fill(s):
        return jnp.minimum(s, nt - 1)

    def drain(s):
        return jnp.maximum(s - 1, 0)

    return pl.pallas_call(
        functools.partial(_sgu_merge_kernel, tm=tm, gc=gc),
        grid=(nt + 1,),
        in_specs=[
            pl.BlockSpec((tm, width), lambda s: (fill(s), 0)),
            pl.BlockSpec((tm, width), lambda s: (fill(s), 1)),
            pl.BlockSpec((tm, d), lambda s: (drain(s), 2)),
            pl.BlockSpec((tm, d), lambda s: (drain(s), 3)),
            pl.BlockSpec((tm, d), lambda s: (drain(s), 0)),
            _const_spec((1, width)),
            _const_spec(w_s.shape),
            _const_spec(b_s_t.shape),
            _const_spec(w_o_gm.shape),
        ],
        out_specs=pl.BlockSpec((tm, d), lambda s: (drain(s), 0)),
        out_shape=jax.ShapeDtypeStruct((t, d), BF16),
        scratch_shapes=[pltpu.VMEM((2, tm, width), BF16)],
        compiler_params=_params(1),
        name="sgu_merge",
    )(big, big, big, big, o_a, g_sgu, w_s, b_s_t, w_o_gm)


def _outproj_kernel(m_ref, xc_ref, xl_ref, gate_ref, sc_ref, sh_ref, g_ref, wo_ref, wr_ref,
                    x1_ref, h2_ref, lg_ref, *, tm, n_ctx_tiles):
    i = pl.program_id(0)

    def body(x_ref):
        half = tm // 2
        for b0 in range(0, tm, half):
            r = _bdot(m_ref[b0:b0 + half, :], wo_ref[...])
            for r0 in range(0, half, CHUNK):
                rows = slice(b0 + r0, b0 + r0 + CHUNK)
                x1 = x_ref[rows, :] + gate_ref[...] * r[r0:r0 + CHUNK, :]
                x1_ref[rows, :] = x1
                h2 = (_rms_rows(x1, g_ref[...]) * (1.0 + sc_ref[...]) + sh_ref[...]).astype(BF16)
                h2_ref[rows, :] = h2
                lg_ref[rows, :] = _bdot(h2, wr_ref[...])

    @pl.when(i < n_ctx_tiles)
    def _():
        body(xc_ref)

    @pl.when(i >= n_ctx_tiles)
    def _():
        body(xl_ref)


def _out_proj(merged, xc, xl, mod6, g_ffn, w_out, w_router_p, *, n_lat_seq, tm):
    n_ctx, d = xc.shape
    t = n_ctx + xl.shape[0]
    n_ctx_tiles = n_ctx // tm
    tiles_per_seq = n_lat_seq // tm

    def mod_row(i):
        return jnp.where(i < n_ctx_tiles, MOD_ROWS // 2, (i - n_ctx_tiles) // tiles_per_seq)

    return pl.pallas_call(
        functools.partial(_outproj_kernel, tm=tm, n_ctx_tiles=n_ctx_tiles),
        grid=(t // tm,),
        in_specs=[
            pl.BlockSpec((tm, d), lambda i: (i, 0)),
            pl.BlockSpec((tm, d), lambda i: (jnp.minimum(i, n_ctx_tiles - 1), 0)),
            pl.BlockSpec((tm, d), lambda i: (jnp.maximum(i - n_ctx_tiles, 0), 0)),
            pl.BlockSpec((None, None, 1, d), lambda i: (2, mod_row(i), 0, 0)),
            pl.BlockSpec((None, None, 1, d), lambda i: (4, mod_row(i), 0, 0)),
            pl.BlockSpec((None, None, 1, d), lambda i: (3, mod_row(i), 0, 0)),
            _const_spec((1, d)),
            _const_spec(w_out.shape),
            _const_spec(w_router_p.shape),
        ],
        out_specs=[
            pl.BlockSpec((tm, d), lambda i: (i, 0)),
            pl.BlockSpec((tm, d), lambda i: (i, 0)),
            pl.BlockSpec((tm, LOGIT_LANES), lambda i: (i, 0)),
        ],
        out_shape=[
            jax.ShapeDtypeStruct((t, d), F32),
            jax.ShapeDtypeStruct((t, d), BF16),
            jax.ShapeDtypeStruct((t, LOGIT_LANES), F32),
        ],
        compiler_params=_params(1),
        name="out_proj",
    )(merged, xc, xl, mod6, mod6, mod6, g_ffn, w_out, w_router_p)


BISECT_STEPS = 48
MIN_NORMAL_F32 = float(np.finfo(np.float32).tiny)


def _route_kernel(lg_ref, key_ref, aff_ref, tri_scr, *, n_sets, n, cap):
    for r0 in range(0, n, 128):
        r = lax.broadcasted_iota(jnp.int32, (128, n), 0) + r0
        c = lax.broadcasted_iota(jnp.int32, (128, n), 1)
        tri_scr[r0:r0 + 128, :] = jnp.where(r < c, 1.0, 0.0).astype(BF16)

    for s in range(n_sets):
        logits = lg_ref[s * n:(s + 1) * n, :].T[0:N_EXPERTS, :]
        e = jnp.exp(logits - jnp.max(logits, axis=0, keepdims=True))
        aff_ref[s * N_EXPERTS:(s + 1) * N_EXPERTS, :] = e / jnp.sum(e, axis=0, keepdims=True)
    aff = aff_ref[...]
    rows = n_sets * N_EXPERTS

    def count_ge(thr):
        return jnp.sum(jnp.where(aff >= thr, 1.0, 0.0), axis=1, keepdims=True)

    def bisect(_, carry):
        lo, hi = carry
        mid = jnp.sqrt(lo) * jnp.sqrt(hi)
        ok = count_ge(mid) >= cap
        return jnp.where(ok, mid, lo), jnp.where(ok, hi, mid)

    lo0 = jnp.full((rows, 1), MIN_NORMAL_F32, F32)
    hi0 = jnp.full((rows, 1), 2.0, F32)
    lo, hi = lax.fori_loop(0, BISECT_STEPS, bisect, (lo0, hi0))
    lo = jnp.where(count_ge(lo) >= cap, lo, 0.0)

    above = aff >= hi
    band = (aff >= lo) & jnp.logical_not(above)
    need = cap - jnp.sum(jnp.where(above, 1.0, 0.0), axis=1, keepdims=True)
    tri = tri_scr[...]
    band_before = _bdot(jnp.where(band, 1.0, 0.0).astype(BF16), tri)
    sel = above | (band & (band_before < need))
    pos = _bdot(jnp.where(sel, 1.0, 0.0).astype(BF16), tri)
    key_ref[...] = jnp.where(sel, pos, -1.0)


def _route(logits, *, row0, n_sets, n):
    cap = EC_FACTOR * n // N_EXPERTS
    rows = n_sets * N_EXPERTS
    blk = row0 // (n_sets * n)
    return pl.pallas_call(
        functools.partial(_route_kernel, n_sets=n_sets, n=n, cap=cap),
        grid=(1,),
        in_specs=[pl.BlockSpec((n_sets * n, LOGIT_LANES), lambda g: (blk, 0))],
        out_specs=[pl.BlockSpec((rows, n), lambda g: (0, 0)), pl.BlockSpec((rows, n), lambda g: (0, 0))],
        out_shape=[jax.ShapeDtypeStruct((rows, n), F32), jax.ShapeDtypeStruct((rows, n), F32)],
        scratch_shapes=[pltpu.VMEM((n, n), BF16)],
        compiler_params=_params(1),
        name=f"route_{n}",
    )(logits)


def _gather_kernel(key_ref, aff_ref, h2_ref, pt_ref, gate_ref, xg_ref, p_scr, *, sps, n, cap, d, nb):
    slot = lax.broadcasted_iota(jnp.int32, (cap, n), 0).astype(F32)
    per_group = 128 // cap
    for s in range(sps):
        key = key_ref[s * N_EXPERTS:(s + 1) * N_EXPERTS, :]
        aff = aff_ref[s * N_EXPERTS:(s + 1) * N_EXPERTS, :]
        for grp in range(N_EXPERTS // per_group):
            pieces = []
            for ex in range(grp * per_group, (grp + 1) * per_group):
                hit = slot == key[ex:ex + 1, :]
                gate_ref[ex, s] = jnp.sum(jnp.where(hit, aff[ex:ex + 1, :], 0.0), axis=1, keepdims=True)
                pieces.append(jnp.where(hit, 1.0, 0.0))
            hit128 = pieces[0] if per_group == 1 else jnp.concatenate(pieces, axis=0)
            p_scr[s, grp * 128:(grp + 1) * 128, :] = hit128.astype(BF16)
            pt_ref[s, :, grp * 128:(grp + 1) * 128] = hit128.T.astype(BF16)

        p = p_scr[s]
        for c in range(d // nb):
            xg = _bdot(p, h2_ref[s * n:(s + 1) * n, c * nb:(c + 1) * nb]).astype(BF16)
            for ex in range(N_EXPERTS):
                xg_ref[ex, s, :, c * nb:(c + 1) * nb] = xg[ex * cap:(ex + 1) * cap, :]


def _dispatch_gather(key, aff, h2, *, row0, n_sets, n, sps):
    d = h2.shape[1]
    cap = EC_FACTOR * n // N_EXPERTS
    slots = N_EXPERTS * cap
    blk0 = row0 // (sps * n)
    return pl.pallas_call(
        functools.partial(_gather_kernel, sps=sps, n=n, cap=cap, d=d, nb=256),
        grid=(n_sets // sps,),
        in_specs=[
            pl.BlockSpec((sps * N_EXPERTS, n), lambda b: (b, 0)),
            pl.BlockSpec((sps * N_EXPERTS, n), lambda b: (b, 0)),
            pl.BlockSpec((sps * n, d), lambda b: (blk0 + b, 0)),
        ],
        out_specs=[
            pl.BlockSpec((sps, n, slots), lambda b: (b, 0, 0)),
            pl.BlockSpec((N_EXPERTS, sps, cap, 1), lambda b: (0, b, 0, 0)),
            pl.BlockSpec((N_EXPERTS, sps, cap, d), lambda b: (0, b, 0, 0)),
        ],
        out_shape=[
            jax.ShapeDtypeStruct((n_sets, n, slots), BF16),
            jax.ShapeDtypeStruct((N_EXPERTS, n_sets, cap, 1), F32),
            jax.ShapeDtypeStruct((N_EXPERTS, n_sets, cap, d), BF16),
        ],
        scratch_shapes=[pltpu.VMEM((sps, slots, n), BF16)],
        compiler_params=_params(1),
        name=f"dispatch_gather_{n}",
    )(key, aff, h2)


def _expert_kernel(xc_ref, xl_ref, gc_ref, gl_ref, w1_ref, w3_ref, w2_ref, yc_ref, yl_ref, hc_scr, hl_scr,
                   *, n_f, fc, rows_c, rows_l, d):
    k = pl.program_id(1)

    @pl.when(k < n_f)
    def _():
        w1 = w1_ref[...].astype(BF16)
        w3 = w3_ref[...].astype(BF16)
        for x_ref, h_scr, rows in ((xc_ref, hc_scr, rows_c), (xl_ref, hl_scr, rows_l)):
            x = x_ref[...].reshape(rows, d)
            a = _bdot(x, w1)
            g = _bdot(x, w3)
            h_scr[k] = (_silu(a) * g).astype(BF16)

    @pl.when(k >= n_f)
    def _():
        w2 = w2_ref[...].astype(BF16)
        for h_scr, g_ref, y_ref, rows in ((hc_scr, gc_ref, yc_ref, rows_c), (hl_scr, gl_ref, yl_ref, rows_l)):
            y = _bdot(h_scr[0], w2[0:fc, :])
            for kk in range(1, n_f):
                y = y + _bdot(h_scr[kk], w2[kk * fc:(kk + 1) * fc, :])
            y_ref[...] = (y * g_ref[...].reshape(rows, 1)).astype(BF16).reshape(y_ref.shape)


def _experts(xg_c, xg_l, gate_c, gate_l, w_e1, w_e3, w_e2, *, fc=512, nc=512):
    n_e, sets_c, cap_c, d = xg_c.shape
    _, sets_l, cap_l, _ = xg_l.shape
    d_ff = w_e1.shape[2]
    n_f = d_ff // fc
    n_c = d // nc
    rows_c, rows_l = sets_c * cap_c, sets_l * cap_l

    def e_in(e, k):
        return jnp.minimum(e + (k >= n_f).astype(jnp.int32), n_e - 1)

    def f_idx(k):
        return jnp.where(k < n_f, k, 0)

    def c_idx(k):
        return jnp.maximum(k - n_f, 0)

    kern = functools.partial(_expert_kernel, n_f=n_f, fc=fc, rows_c=rows_c, rows_l=rows_l, d=d)
    return pl.pallas_call(
        kern,
        grid=(n_e, n_f + n_c),
        in_specs=[
            pl.BlockSpec((None, sets_c, cap_c, d), lambda e, k: (e_in(e, k), 0, 0, 0)),
            pl.BlockSpec((None, sets_l, cap_l, d), lambda e, k: (e_in(e, k), 0, 0, 0)),
            pl.BlockSpec((None, sets_c, cap_c, 1), lambda e, k: (e, 0, 0, 0)),
            pl.BlockSpec((None, sets_l, cap_l, 1), lambda e, k: (e, 0, 0, 0)),
            pl.BlockSpec((None, d, fc), lambda e, k: (e_in(e, k), 0, f_idx(k))),
            pl.BlockSpec((None, d, fc), lambda e, k: (e_in(e, k), 0, f_idx(k))),
            pl.BlockSpec((None, d_ff, nc), lambda e, k: (e, 0, c_idx(k))),
        ],
        out_specs=[
            pl.BlockSpec((None, sets_c, cap_c, nc), lambda e, k: (e, 0, 0, c_idx(k))),
            pl.BlockSpec((None, sets_l, cap_l, nc), lambda e, k: (e, 0, 0, c_idx(k))),
        ],
        out_shape=[
            jax.ShapeDtypeStruct(xg_c.shape, BF16),
            jax.ShapeDtypeStruct(xg_l.shape, BF16),
        ],
        scratch_shapes=[pltpu.VMEM((n_f, rows_c, fc), BF16), pltpu.VMEM((n_f, rows_l, fc), BF16)],
        compiler_params=_params(2),
        name="experts",
    )(xg_c, xg_l, gate_c, gate_l, w_e1, w_e3, w_e2)


def _combine_kernel(pt_ref, y_ref, x1_ref, gate_ref, g_ref, o_ref, acc_scr, *, sps, tn, slots, d, nb):
    for s in range(sps):
        pt = pt_ref[s]
        rows = slice(s * tn, (s + 1) * tn)
        for c in range(d // nb):
            cols = slice(c * nb, (c + 1) * nb)
            moe = _bdot(pt, y_ref[:, s, :, cols].reshape(slots, nb))
            acc_scr[:, cols] = x1_ref[rows, cols] + gate_ref[:, cols] * moe
        o_ref[rows, :] = _rms_rows(acc_scr[...], g_ref[...])


def _combine(pt, y, x1, mod6, g_final, *, row0, n_sets, n, sps, mod_row_fn):
    d = x1.shape[1]
    cap = EC_FACTOR * n // N_EXPERTS
    slots = N_EXPERTS * cap
    tn = min(n, 512)
    nt = n // tn
    assert sps == 1 or nt == 1
    blk0 = row0 // (sps * tn)
    return pl.pallas_call(
        functools.partial(_combine_kernel, sps=sps, tn=tn, slots=slots, d=d, nb=512),
        grid=(n_sets // sps, nt),
        in_specs=[
            pl.BlockSpec((sps, tn, slots), lambda b, r: (b, r, 0)),
            pl.BlockSpec((N_EXPERTS, sps, cap, d), lambda b, r: (0, b, 0, 0)),
            pl.BlockSpec((sps * tn, d), lambda b, r: (blk0 + b * nt + r, 0)),
            pl.BlockSpec((None, None, 1, d), lambda b, r: (5, mod_row_fn(b), 0, 0)),
            _const_spec((1, d)),
        ],
        out_specs=pl.BlockSpec((sps * tn, d), lambda b, r: (b * nt + r, 0)),
        out_shape=jax.ShapeDtypeStruct((n_sets * n, d), F32),
        scratch_shapes=[pltpu.VMEM((tn, d), F32)],
        compiler_params=_params(2),
        name=f"combine_{n}",
    )(pt, y, x1, mod6, g_final)


def _rope_tables(n):
    tpos = jnp.arange(n, dtype=jnp.int32)
    row = (tpos // GRID_W).astype(F32)
    col = (tpos % GRID_W).astype(F32)
    inv = 1.0 / (ROPE_BASE ** (jnp.arange(ROPE_FREQS, dtype=F32) / ROPE_FREQS))
    ang = jnp.stack([row[:, None] * inv, col[:, None] * inv], axis=1)
    cos = jnp.cos(ang)[:, :, None, :]
    sin = jnp.sin(ang)[:, :, None, :]
    cos = jnp.broadcast_to(cos, (n, 2, 2, ROPE_FREQS)).reshape(n, D_ROPE)
    sin = jnp.concatenate([-sin, sin], axis=2).reshape(n, D_ROPE)
    pad = jnp.zeros((n, 128 - D_ROPE), F32)
    return jnp.concatenate([cos, pad], axis=1), jnp.concatenate([sin, pad], axis=1)


def _swap_rotary_halves(w):
    perm = np.arange(D_ROPE) ^ ROPE_FREQS
    return w[..., perm]


def kernel(x_prompt, x_sample, c, cache_ckv, cache_krope, c_ctx, g_attn, g_ffn, w_ada, b_ada, w_in, g_qa,
           w_qb, g_kv, w_uk, w_uv, w_o_mla, g_sgu, w_s, b_s, w_o_gm, w_out, w_router, w_e1, w_e3, w_e2,
           g_final):
    batch, seq, d = x_prompt.shape
    dec_batch, dec_seq, _ = x_sample.shape
    depth = g_attn.shape[0]
    assert depth == 1
    q_lora, kv_lora = g_qa.shape[1], g_kv.shape[1]
    gm_width = g_sgu.shape[1]
    n_ctx, n_lat = batch * seq, dec_batch * dec_seq
    assert dec_batch < MOD_ROWS // 2 + 1

    xc = x_prompt.reshape(n_ctx, d)
    xl = x_sample.reshape(n_lat, d)

    c_rows = jnp.zeros((MOD_ROWS, d), F32).at[:dec_batch].set(c).at[MOD_ROWS // 2].set(c_ctx)
    mod = _modulation(c_rows, w_ada[0], b_ada[0][None, :])
    mod6 = mod.reshape(MOD_ROWS, 6, 1, d).transpose(1, 0, 2, 3)

    cos_t, sin_t = _rope_tables(dec_seq)

    w_in_t = w_in[0].T
    o_kr = q_lora + kv_lora
    w_kr_t = w_in_t[o_kr:o_kr + D_ROPE]
    zpad = jnp.zeros((128 - D_ROPE, d), F32)
    w_small_t = jnp.concatenate(
        [w_in_t[:o_kr], w_kr_t, zpad, w_kr_t[np.arange(D_ROPE) ^ ROPE_FREQS], zpad], axis=0).astype(BF16)
    tm, bn_in = 512, 1024
    front = -(o_kr + D_ROPE) % bn_in

    wq3 = w_qb[0].reshape(q_lora, N_HEADS, D_NOPE + D_ROPE)
    wq_nope, wq_rope = wq3[:, :, :D_NOPE], wq3[:, :, D_NOPE:]
    zq = jnp.zeros((q_lora, N_HEADS, HEAD_SLOT - D_NOPE - D_ROPE), F32)
    wq_p = jnp.concatenate([wq_nope, wq_rope, zq], axis=2).reshape(q_lora, N_HEADS * HEAD_SLOT).astype(BF16)
    wq_sw = jnp.concatenate([_swap_rotary_halves(wq_rope), zq], axis=2).reshape(
        q_lora, N_HEADS * 128).astype(BF16)

    h, ckv, kr, q_pad, w_all_t = _pre_proj(xc, xl, g_attn, mod6, w_small_t, g_qa, g_kv, cos_t, sin_t,
                                           wq_p, wq_sw, w_in_t, front=front, tm=tm)
    big = _wide_proj(h, w_all_t, n_big=w_in_t.shape[0] - o_kr - D_ROPE, n_gelu=2 * gm_width, tm=1024,
                     bn=bn_in)

    w_uk2 = w_uk[0].reshape(kv_lora, N_HEADS // 2, 2 * D_NOPE).transpose(1, 0, 2).astype(BF16)
    w_uv2 = w_uv[0].reshape(kv_lora, N_HEADS // 2, 2 * D_V).transpose(1, 0, 2).astype(BF16)
    o_ctx = _attention(q_pad, ckv, kr, w_uk2, w_uv2, None, None,
                       row0=0, n_req=batch, n_own=seq, tq=seq, heads_per_iter=N_HEADS)
    o_lat = _attention(q_pad, ckv, kr, w_uk2, w_uv2, cache_ckv[:, 0], cache_krope[:, 0],
                       row0=n_ctx, n_req=dec_batch, n_own=dec_seq, tq=512, heads_per_iter=4)

    o_a = _o_proj(o_ctx, o_lat, w_o_mla[0].astype(BF16), tm=1024)
    merged = _sgu_merge(big, o_a, g_sgu, w_s[0].astype(BF16), b_s[0].T, w_o_gm[0].astype(BF16), tm=tm)

    w_router_p = jnp.concatenate(
        [w_router[0], jnp.zeros((d, LOGIT_LANES - N_EXPERTS), F32)], axis=1).astype(BF16)
    x1, h2, logits = _out_proj(merged, xc, xl, mod6, g_ffn, w_out[0].astype(BF16), w_router_p,
                               n_lat_seq=dec_seq, tm=tm)

    key_c, aff_c = _route(logits, row0=0, n_sets=batch, n=seq)
    key_l, aff_l = _route(logits, row0=n_ctx, n_sets=dec_batch, n=dec_seq)
    p_c, gate_c, xg_c = _dispatch_gather(key_c, aff_c, h2, row0=0, n_sets=batch, n=seq, sps=4)
    p_l, gate_l, xg_l = _dispatch_gather(key_l, aff_l, h2, row0=n_ctx, n_sets=dec_batch, n=dec_seq, sps=1)

    y_c, y_l = _experts(xg_c, xg_l, gate_c, gate_l, w_e1[0], w_e3[0], w_e2[0])

    y_prompt = _combine(p_c, y_c, x1, mod6, g_final[None, :], row0=0, n_sets=batch, n=seq, sps=2,
                        mod_row_fn=lambda b: MOD_ROWS // 2)
    y_sample = _combine(p_l, y_l, x1, mod6, g_final[None, :], row0=n_ctx, n_sets=dec_batch, n=dec_seq, sps=1,
                        mod_row_fn=lambda b: b)

    new_ckv = ckv[:n_ctx].reshape(batch, 1, seq, kv_lora)
    new_krope = kr[:n_ctx, :D_ROPE].reshape(batch, 1, seq, D_ROPE)
    return (y_prompt.reshape(batch, seq, d), y_sample.reshape(dec_batch, dec_seq, d), new_ckv, new_krope)
```

```python
import functools

import numpy as np
import jax
import jax.numpy as jnp
from jax import lax
from jax.experimental import pallas as pl
from jax.experimental.pallas import tpu as pltpu

F32 = jnp.float32
BF16 = jnp.bfloat16

N_HEADS = 16
D_NOPE = 128
D_ROPE = 64
D_V = 128
HEAD_SLOT = 256
ROPE_FREQS = D_ROPE // 4
ROPE_BASE = 10000.0
GRID_W = 64
CHUNK = 128
GM_GROUPS = 8
N_EXPERTS = 16
EC_FACTOR = 2
EPS = 1e-6
MOD_ROWS = 8
LOGIT_LANES = 128

VMEM_LIMIT_V7X = 56 * 1024 * 1024


def _params(n_axes):
    return pltpu.CompilerParams(
        dimension_semantics=("arbitrary",) * n_axes, vmem_limit_bytes=VMEM_LIMIT_V7X)


def _const_spec(shape):
    nd = len(shape)
    return pl.BlockSpec(shape, lambda *_: (0,) * nd, pipeline_mode=pl.Buffered(1))


def _half_tanh_half(x):
    return jnp.tanh(0.5 * x)


def _silu(x):
    h = 0.5 * x
    return h * jnp.tanh(h) + h


def _gelu_tanh(x):
    assert x.dtype == F32
    c = float(np.sqrt(2.0 / np.pi))
    return x * (0.5 * (1.0 + jnp.tanh(c * (x + 0.044715 * (x * x * x)))))


def _rms_rows(x, g):
    return x * lax.rsqrt(jnp.mean(x * x, axis=-1, keepdims=True) + EPS) * g


def _bdot(a, b):
    return jnp.dot(a, b, preferred_element_type=F32)


def _bdot_t(a, bt):
    return lax.dot_general(a, bt, (((1,), (1,)), ((), ())), preferred_element_type=F32)


def _mod_kernel(c_ref, w_ref, b_ref, o_ref):
    c = c_ref[...]
    s = _silu(c).astype(BF16)
    o_ref[...] = _bdot(s, w_ref[...].astype(BF16)) + b_ref[...]


def _modulation(c_rows, w_ada, b_ada, *, n_cols, bn=1024):
    d, n = w_ada.shape[0], n_cols
    return pl.pallas_call(
        _mod_kernel,
        grid=(n // bn,),
        in_specs=[
            _const_spec((MOD_ROWS, d)),
            pl.BlockSpec((d, bn), lambda j: (0, j)),
            pl.BlockSpec((1, bn), lambda j: (0, j)),
        ],
        out_specs=pl.BlockSpec((MOD_ROWS, bn), lambda j: (0, j)),
        out_shape=jax.ShapeDtypeStruct((MOD_ROWS, n), F32),
        compiler_params=_params(1),
        name="adaln_mod",
    )(c_rows, w_ada, b_ada)


def _preproj_kernel(xc_ref, xl_ref, g_ref, sc_ref, sh_ref, ws_ref, gqa_ref, gkv_ref, cos_ref, sin_ref,
                    wq_ref, wsw_ref, wlo_ref, whi_ref, h_ref, ckv_ref, kr_ref, q_ref, wcast_ref,
                    *, n_ctx_tiles, q_lora, kv_lora, scale):
    i = pl.program_id(0)

    rb = wlo_ref.shape[0]
    lo = wlo_ref[...].astype(BF16)
    wcast_ref[0:rb, :] = jnp.where(i == 0, jnp.zeros_like(lo), lo)
    wcast_ref[rb:2 * rb, :] = whi_ref[...].astype(BF16)

    def body(x_ref, rotary):
        h = _rms_rows(x_ref[...], g_ref[...]) * (1.0 + sc_ref[...]) + sh_ref[...]
        hb = h.astype(BF16)
        h_ref[...] = hb
        small = _bdot_t(hb, ws_ref[...])
        qa = _rms_rows(small[:, :q_lora], gqa_ref[...]).astype(BF16)
        ckv_ref[...] = _rms_rows(small[:, q_lora:q_lora + kv_lora], gkv_ref[...])
        o = q_lora + kv_lora
        kr = small[:, o:o + 128]
        if not rotary:
            kr_ref[...] = kr
            for hd in range(N_HEADS):
                q = _bdot(qa, wq_ref[:, hd * HEAD_SLOT:(hd + 1) * HEAD_SLOT])
                q_ref[hd] = (q * scale).astype(BF16)
            return
        cos = cos_ref[...]
        sin = sin_ref[...]
        kr_sw = small[:, o + 128:o + 256]
        kr_ref[...] = kr * cos + kr_sw * sin
        for hp in range(N_HEADS // 2):
            q_sw = _bdot(qa, wsw_ref[:, hp * 256:(hp + 1) * 256])
            for s in range(2):
                hd = 2 * hp + s
                q = _bdot(qa, wq_ref[:, hd * HEAD_SLOT:(hd + 1) * HEAD_SLOT])
                q_ref[hd, :, 0:128] = (q[:, 0:128] * scale).astype(BF16)
                rot = q[:, 128:256] * cos + q_sw[:, s * 128:(s + 1) * 128] * sin
                q_ref[hd, :, 128:256] = (rot * scale).astype(BF16)

    @pl.when(i < n_ctx_tiles)
    def _():
        body(xc_ref, False)

    @pl.when(i >= n_ctx_tiles)
    def _():
        body(xl_ref, True)


def _pre_proj(xc, xl, g_attn, mod6, w_small_t, g_qa, g_kv, cos_t, sin_t, wq_p, wq_sw, w_in_t, *, front, tm):
    n_ctx, d = xc.shape
    n_lat_seq = cos_t.shape[0]
    t = n_ctx + xl.shape[0]
    q_lora, kv_lora = g_qa.shape[1], g_kv.shape[1]
    n_ctx_tiles = n_ctx // tm
    tiles_per_seq = n_lat_seq // tm
    n_wblk = w_in_t.shape[0] // front
    assert w_in_t.shape[0] == n_wblk * front and front % 16 == 0
    assert n_wblk + 1 == 2 * (t // tm), "weight row blocks must pair up with the token steps"

    def mod_row(i):
        return jnp.where(i < n_ctx_tiles, MOD_ROWS // 2, (i - n_ctx_tiles) // tiles_per_seq)

    def rope_blk(i):
        return jnp.maximum(i - n_ctx_tiles, 0) % tiles_per_seq

    scale = np.float32((D_NOPE + D_ROPE) ** -0.5)
    kern = functools.partial(_preproj_kernel, n_ctx_tiles=n_ctx_tiles, q_lora=q_lora, kv_lora=kv_lora,
                             scale=scale)
    return pl.pallas_call(
        kern,
        grid=(t // tm,),
        in_specs=[
            pl.BlockSpec((tm, d), lambda i: (jnp.minimum(i, n_ctx_tiles - 1), 0)),
            pl.BlockSpec((tm, d), lambda i: (jnp.maximum(i - n_ctx_tiles, 0), 0)),
            _const_spec((1, d)),
            pl.BlockSpec((None, None, 1, d), lambda i: (1, mod_row(i), 0, 0)),
            pl.BlockSpec((None, None, 1, d), lambda i: (0, mod_row(i), 0, 0)),
            _const_spec(w_small_t.shape),
            _const_spec((1, q_lora)),
            _const_spec((1, kv_lora)),
            pl.BlockSpec((tm, 128), lambda i: (rope_blk(i), 0)),
            pl.BlockSpec((tm, 128), lambda i: (rope_blk(i), 0)),
            _const_spec(wq_p.shape),
            _const_spec(wq_sw.shape),
            pl.BlockSpec((front, d), lambda i: (jnp.maximum(2 * i - 1, 0), 0)),
            pl.BlockSpec((front, d), lambda i: (2 * i, 0)),
        ],
        out_specs=[
            pl.BlockSpec((tm, d), lambda i: (i, 0)),
            pl.BlockSpec((tm, kv_lora), lambda i: (i, 0)),
            pl.BlockSpec((tm, 128), lambda i: (i, 0)),
            pl.BlockSpec((N_HEADS, tm, HEAD_SLOT), lambda i: (0, i, 0)),
            pl.BlockSpec((2 * front, d), lambda i: (i, 0)),
        ],
        out_shape=[
            jax.ShapeDtypeStruct((t, d), BF16),
            jax.ShapeDtypeStruct((t, kv_lora), F32),
            jax.ShapeDtypeStruct((t, 128), F32),
            jax.ShapeDtypeStruct((N_HEADS, t, HEAD_SLOT), BF16),
            jax.ShapeDtypeStruct(((n_wblk + 1) * front, d), BF16),
        ],
        compiler_params=_params(1),
        name="pre_proj",
    )(xc, xl, g_attn, mod6, mod6, w_small_t, g_qa, g_kv, cos_t, sin_t, wq_p, wq_sw, w_in_t, w_in_t)


def _wideproj_kernel(h_ref, wa_ref, wb_ref, o_ref, *, tm, bn, n_gelu_steps):
    j = pl.program_id(0)
    h = h_ref[...]

    def run(act):
        for c, w_ref in enumerate((wa_ref, wb_ref)):
            y = _bdot_t(h, w_ref[...])
            for r0 in range(0, tm, 256):
                o_ref[r0:r0 + 256, c * bn:(c + 1) * bn] = act(y[r0:r0 + 256, :]).astype(BF16)

    @pl.when(j < n_gelu_steps)
    def _():
        run(_gelu_tanh)

    @pl.when(j >= n_gelu_steps)
    def _():
        run(lambda y: y)


def _wide_proj(h, w_all_t, *, n_big, n_gelu, tm, bn):
    t, d = h.shape
    blk0 = (w_all_t.shape[0] - n_big) // bn
    return pl.pallas_call(
        functools.partial(_wideproj_kernel, tm=tm, bn=bn, n_gelu_steps=n_gelu // (2 * bn)),
        grid=(n_big // (2 * bn), t // tm),
        in_specs=[
            pl.BlockSpec((tm, d), lambda j, i: (i, 0)),
            pl.BlockSpec((bn, d), lambda j, i: (blk0 + 2 * j, 0)),
            pl.BlockSpec((bn, d), lambda j, i: (blk0 + 2 * j + 1, 0)),
        ],
        out_specs=pl.BlockSpec((tm, 2 * bn), lambda j, i: (i, j)),
        out_shape=jax.ShapeDtypeStruct((t, n_big), BF16),
        compiler_params=_params(2),
        name="wide_proj",
    )(h, w_all_t, w_all_t)


def _attn_kernel(*refs, n_own, n_cache, heads_per_iter, n_side):
    refs = list(refs)
    q_ref, ckv_ref, kr_ref = refs[:3]
    del refs[:3]
    if n_cache:
        cckv_ref, ckr_ref = refs[:2]
        del refs[:2]
    wuk_ref, wuv_ref = refs[:2]
    del refs[:2]
    if n_side:
        c_ref, wada_ref, bada_ref = refs[:3]
        side_in = refs[3:3 + n_side]
        del refs[:3 + n_side]
    o_ref = refs.pop(0)
    if n_side:
        mod_ref = refs.pop(0)
        side_out = refs[:n_side]
        del refs[:n_side]
        mod_ref[...] = _bdot(_silu(c_ref[...]).astype(BF16), wada_ref[...].astype(BF16)) + bada_ref[...]
        for src, dst in zip(side_in, side_out):
            dst[...] = src[...].astype(BF16)
    kpad, vexp, kall, krp, o_scr = refs
    qi = pl.program_id(1)

    @pl.when(qi == 0)
    def _():
        kall[0:n_own, :] = ckv_ref[...].astype(BF16)
        krp[0:n_own, :] = kr_ref[...].astype(BF16)
        if n_cache:
            kall[n_own:n_own + n_cache, :] = cckv_ref[...].astype(BF16)
            krp[n_own:n_own + n_cache, 0:D_ROPE] = ckr_ref[...].astype(BF16)
            krp[n_own:n_own + n_cache, D_ROPE:128] = jnp.zeros((n_cache, 128 - D_ROPE), BF16)

        def expand(hp, carry):
            kn = _bdot(kall[...], wuk_ref[hp]).astype(BF16)
            vv = _bdot(kall[...], wuv_ref[hp]).astype(BF16)
            for s in range(2):
                kpad[2 * hp + s, :, 0:128] = kn[:, s * 128:(s + 1) * 128]
                kpad[2 * hp + s, :, 128:256] = krp[...]
                vexp[2 * hp + s] = vv[:, s * 128:(s + 1) * 128]
            return carry

        if heads_per_iter == N_HEADS:
            for hp in range(N_HEADS // 2):
                expand(hp, 0)
        else:
            lax.fori_loop(0, N_HEADS // 2, expand, 0)

    def one_head(h):
        s = _bdot_t(q_ref[h], kpad[h])
        p = jnp.exp(s - jnp.max(s, axis=-1, keepdims=True))
        l = jnp.sum(p, axis=-1, keepdims=True)
        o = _bdot(p.astype(BF16), vexp[h])
        return (o / l).astype(BF16)

    if heads_per_iter == N_HEADS:
        for h in range(N_HEADS):
            o_ref[:, h * D_V:(h + 1) * D_V] = one_head(h)
    else:
        def head_group(hg, carry):
            for g in range(heads_per_iter):
                h = hg * heads_per_iter + g
                o_scr[h] = one_head(h)
            return carry

        lax.fori_loop(0, N_HEADS // heads_per_iter, head_group, 0)
        for h in range(N_HEADS):
            o_ref[:, h * D_V:(h + 1) * D_V] = o_scr[h]


def _attention(q_pad, ckv, kr, w_uk, w_uv, cache_ckv, cache_kr, *, row0, n_req, n_own, tq, heads_per_iter,
               side=None):
    kv_lora = ckv.shape[1]
    n_cache = 0 if cache_ckv is None else cache_ckv.shape[1]
    kn = n_own + n_cache
    qb = n_own // tq
    in_specs = [
        pl.BlockSpec((N_HEADS, tq, HEAD_SLOT), lambda b, qi: (0, row0 // tq + b * qb + qi, 0)),
        pl.BlockSpec((n_own, kv_lora), lambda b, qi: (row0 // n_own + b, 0)),
        pl.BlockSpec((n_own, 128), lambda b, qi: (row0 // n_own + b, 0)),
    ]
    args = [q_pad, ckv, kr]
    if n_cache:
        in_specs += [
            pl.BlockSpec((None, n_cache, kv_lora), lambda b, qi: (b, 0, 0)),
            pl.BlockSpec((None, n_cache, D_ROPE), lambda b, qi: (b, 0, 0)),
        ]
        args += [cache_ckv, cache_kr]
    in_specs += [_const_spec(w_uk.shape), _const_spec(w_uv.shape)]
    args += [w_uk, w_uv]
    out_specs = [pl.BlockSpec((tq, N_HEADS * D_V), lambda b, qi: (b * qb + qi, 0))]
    out_shape = [jax.ShapeDtypeStruct((n_req * n_own, N_HEADS * D_V), BF16)]
    n_side = 0
    if side is not None:
        assert qb == 1
        c_rows, w_ada, b_ada, col0, weights = side
        d_model, n_mod = w_ada.shape
        mcols = (n_mod - col0) // n_req
        assert mcols * n_req == n_mod - col0 and mcols % 128 == 0 and col0 % mcols == 0
        in_specs += [
            _const_spec(c_rows.shape),
            pl.BlockSpec((d_model, mcols), lambda b, qi: (0, col0 // mcols + b)),
            pl.BlockSpec((1, mcols), lambda b, qi: (0, col0 // mcols + b)),
        ]
        args += [c_rows, w_ada, b_ada]
        out_specs.append(pl.BlockSpec((MOD_ROWS, mcols), lambda b, qi: (0, b)))
        out_shape.append(jax.ShapeDtypeStruct((MOD_ROWS, n_mod - col0), F32))
        n_side = len(weights)
        assert n_side > 0
        for w in weights:
            rows = w.shape[0] // n_req
            assert rows * n_req == w.shape[0] and rows % 16 == 0
            in_specs.append(pl.BlockSpec((rows, w.shape[1]), lambda b, qi: (b, 0)))
            args.append(w)
            out_specs.append(pl.BlockSpec((rows, w.shape[1]), lambda b, qi: (b, 0)))
            out_shape.append(jax.ShapeDtypeStruct(w.shape, BF16))
    res = pl.pallas_call(
        functools.partial(_attn_kernel, n_own=n_own, n_cache=n_cache, heads_per_iter=heads_per_iter,
                          n_side=n_side),
        grid=(n_req, qb),
        in_specs=in_specs,
        out_specs=out_specs,
        out_shape=out_shape,
        scratch_shapes=[
            pltpu.VMEM((N_HEADS, kn, HEAD_SLOT), BF16),
            pltpu.VMEM((N_HEADS, kn, D_V), BF16),
            pltpu.VMEM((kn, kv_lora), BF16),
            pltpu.VMEM((kn, 128), BF16),
            pltpu.VMEM((N_HEADS, tq, D_V), BF16),
        ],
        compiler_params=_params(2),
        name="mla_attn_cache" if n_cache else "mla_attn",
    )(*args)
    if side is None:
        return res[0]
    return res[0], res[1], res[2:]


def _oproj_kernel(xc_ref, xl_ref, w_ref, o_ref, *, n_ctx_tiles):
    i = pl.program_id(0)

    @pl.when(i < n_ctx_tiles)
    def _():
        o_ref[...] = _bdot(xc_ref[...], w_ref[...]).astype(o_ref.dtype)

    @pl.when(i >= n_ctx_tiles)
    def _():
        o_ref[...] = _bdot(xl_ref[...], w_ref[...]).astype(o_ref.dtype)


def _o_proj(xc, xl, w, *, tm):
    n_ctx, k = xc.shape
    t = n_ctx + xl.shape[0]
    n = w.shape[1]
    n_ctx_tiles = n_ctx // tm
    return pl.pallas_call(
        functools.partial(_oproj_kernel, n_ctx_tiles=n_ctx_tiles),
        grid=(t // tm,),
        in_specs=[
            pl.BlockSpec((tm, k), lambda i: (jnp.minimum(i, n_ctx_tiles - 1), 0)),
            pl.BlockSpec((tm, k), lambda i: (jnp.maximum(i - n_ctx_tiles, 0), 0)),
            _const_spec(w.shape),
        ],
        out_specs=pl.BlockSpec((tm, n), lambda i: (i, 0)),
        out_shape=jax.ShapeDtypeStruct((t, n), BF16),
        compiler_params=_params(1),
        name="o_proj",
    )(xc, xl, w)


def _sgu_merge_kernel(gu_ref, gv_ref, ga_ref, gb_ref, oa_ref, g_ref, ws_ref, bs_ref, wo_ref, m_ref, z_scr,
                      *, tm, gc):
    s = pl.program_id(0)

    @pl.when(s == 0)
    def _():
        z_scr[1] = jnp.zeros(z_scr.shape[1:], BF16)

    o_b = _bdot(z_scr[(s + 1) % 2], wo_ref[...])
    for r0 in range(0, tm, CHUNK):
        rows = slice(r0, r0 + CHUNK)
        oa = oa_ref[rows, :]
        ob = o_b[rows, :].astype(BF16)
        ta = _half_tanh_half(ga_ref[rows, :])
        tb = _half_tanh_half(gb_ref[rows, :])
        m_ref[rows, :] = 0.5 * ((oa + ob) + (ta * oa + tb * ob))

    slot = s % 2
    g_bf = g_ref[...].astype(BF16)
    for r0 in range(0, tm, CHUNK):
        v = gv_ref[r0:r0 + CHUNK, :]
        v32 = v.astype(F32)
        inv = lax.rsqrt(jnp.mean(v32 * v32, axis=-1, keepdims=True) + EPS)
        vn = v * inv.astype(BF16) * g_bf
        for g in range(GM_GROUPS):
            c0 = g * gc
            mix = _bdot(ws_ref[g], vn[:, c0:c0 + gc]) + bs_ref[:, g:g + 1]
            u = gu_ref[r0:r0 + CHUNK, c0:c0 + gc]
            z_scr[slot, r0:r0 + CHUNK, c0:c0 + gc] = u * mix.astype(BF16)


def _sgu_merge(big, o_a, g_sgu, w_s, b_s_t, w_o_gm, *, tm):
    t = big.shape[0]
    width = g_sgu.shape[1]
    d = w_o_gm.shape[1]
    gc = width // GM_GROUPS
    nt = t // tm

    def fill(s):
        return jnp.minimum(s, nt - 1)

    def drain(s):
        return jnp.maximum(s - 1, 0)

    return pl.pallas_call(
        functools.partial(_sgu_merge_kernel, tm=tm, gc=gc),
        grid=(nt + 1,),
        in_specs=[
            pl.BlockSpec((tm, width), lambda s: (fill(s), 0)),
            pl.BlockSpec((tm, width), lambda s: (fill(s), 1)),
            pl.BlockSpec((tm, d), lambda s: (drain(s), 2)),
            pl.BlockSpec((tm, d), lambda s: (drain(s), 3)),
            pl.BlockSpec((tm, d), lambda s: (drain(s), 0)),
            _const_spec((1, width)),
            _const_spec(w_s.shape),
            _const_spec(b_s_t.shape),
            _const_spec(w_o_gm.shape),
        ],
        out_specs=pl.BlockSpec((tm, d), lambda s: (drain(s), 0)),
        out_shape=jax.ShapeDtypeStruct((t, d), BF16),
        scratch_shapes=[pltpu.VMEM((2, tm, width), BF16)],
        compiler_params=_params(1),
        name="sgu_merge",
    )(big, big, big, big, o_a, g_sgu, w_s, b_s_t, w_o_gm)


def _outproj_kernel(m_ref, xc_ref, xl_ref, gate_ref, sc_ref, sh_ref, g_ref, wo_ref, wr_ref,
                    x1_ref, h2_ref, lg_ref, *, tm, n_ctx_tiles):
    i = pl.program_id(0)

    def body(x_ref):
        half = tm // 2
        for b0 in range(0, tm, half):
            r = _bdot(m_ref[b0:b0 + half, :], wo_ref[...])
            for r0 in range(0, half, CHUNK):
                rows = slice(b0 + r0, b0 + r0 + CHUNK)
                x1 = x_ref[rows, :] + gate_ref[...] * r[r0:r0 + CHUNK, :]
                x1_ref[rows, :] = x1
                h2 = (_rms_rows(x1, g_ref[...]) * (1.0 + sc_ref[...]) + sh_ref[...]).astype(BF16)
                h2_ref[rows, :] = h2
                lg_ref[rows, :] = _bdot(h2, wr_ref[...])

    @pl.when(i < n_ctx_tiles)
    def _():
        body(xc_ref)

    @pl.when(i >= n_ctx_tiles)
    def _():
        body(xl_ref)


def _out_proj(merged, xc, xl, mod6, g_ffn, w_out, w_router_p, *, n_lat_seq, tm):
    n_ctx, d = xc.shape
    t = n_ctx + xl.shape[0]
    n_ctx_tiles = n_ctx // tm
    tiles_per_seq = n_lat_seq // tm

    def mod_row(i):
        return jnp.where(i < n_ctx_tiles, MOD_ROWS // 2, (i - n_ctx_tiles) // tiles_per_seq)

    return pl.pallas_call(
        functools.partial(_outproj_kernel, tm=tm, n_ctx_tiles=n_ctx_tiles),
        grid=(t // tm,),
        in_specs=[
            pl.BlockSpec((tm, d), lambda i: (i, 0)),
            pl.BlockSpec((tm, d), lambda i: (jnp.minimum(i, n_ctx_tiles - 1), 0)),
            pl.BlockSpec((tm, d), lambda i: (jnp.maximum(i - n_ctx_tiles, 0), 0)),
            pl.BlockSpec((None, None, 1, d), lambda i: (2, mod_row(i), 0, 0)),
            pl.BlockSpec((None, None, 1, d), lambda i: (4, mod_row(i), 0, 0)),
            pl.BlockSpec((None, None, 1, d), lambda i: (3, mod_row(i), 0, 0)),
            _const_spec((1, d)),
            _const_spec(w_out.shape),
            _const_spec(w_router_p.shape),
        ],
        out_specs=[
            pl.BlockSpec((tm, d), lambda i: (i, 0)),
            pl.BlockSpec((tm, d), lambda i: (i, 0)),
            pl.BlockSpec((tm, LOGIT_LANES), lambda i: (i, 0)),
        ],
        out_shape=[
            jax.ShapeDtypeStruct((t, d), F32),
            jax.ShapeDtypeStruct((t, d), BF16),
            jax.ShapeDtypeStruct((t, LOGIT_LANES), F32),
        ],
        compiler_params=_params(1),
        name="out_proj",
    )(merged, xc, xl, mod6, mod6, mod6, g_ffn, w_out, w_router_p)


BISECT_STEPS = 48
MIN_NORMAL_F32 = float(np.finfo(np.float32).tiny)


def _route_kernel(lg_ref, key_ref, aff_ref, tri_scr, *, n_sets, n, cap):
    for r0 in range(0, n, 128):
        r = lax.broadcasted_iota(jnp.int32, (128, n), 0) + r0
        c = lax.broadcasted_iota(jnp.int32, (128, n), 1)
        tri_scr[r0:r0 + 128, :] = jnp.where(r < c, 1.0, 0.0).astype(BF16)

    for s in range(n_sets):
        logits = lg_ref[s * n:(s + 1) * n, :].T[0:N_EXPERTS, :]
        e = jnp.exp(logits - jnp.max(logits, axis=0, keepdims=True))
        aff_ref[s * N_EXPERTS:(s + 1) * N_EXPERTS, :] = e / jnp.sum(e, axis=0, keepdims=True)
    aff = aff_ref[...]
    rows = n_sets * N_EXPERTS

    def count_ge(thr):
        return jnp.sum(jnp.where(aff >= thr, 1.0, 0.0), axis=1, keepdims=True)

    def bisect(_, carry):
        lo, hi = carry
        mid = jnp.sqrt(lo) * jnp.sqrt(hi)
        ok = count_ge(mid) >= cap
        return jnp.where(ok, mid, lo), jnp.where(ok, hi, mid)

    lo0 = jnp.full((rows, 1), MIN_NORMAL_F32, F32)
    hi0 = jnp.full((rows, 1), 2.0, F32)
    lo, hi = lax.fori_loop(0, BISECT_STEPS, bisect, (lo0, hi0))
    lo = jnp.where(count_ge(lo) >= cap, lo, 0.0)

    above = aff >= hi
    band = (aff >= lo) & jnp.logical_not(above)
    need = cap - jnp.sum(jnp.where(above, 1.0, 0.0), axis=1, keepdims=True)
    tri = tri_scr[...]
    band_before = _bdot(jnp.where(band, 1.0, 0.0).astype(BF16), tri)
    sel = above | (band & (band_before < need))
    pos = _bdot(jnp.where(sel, 1.0, 0.0).astype(BF16), tri)
    key_ref[...] = jnp.where(sel, pos, -1.0)


def _route(logits, *, row0, n_sets, n):
    cap = EC_FACTOR * n // N_EXPERTS
    rows = n_sets * N_EXPERTS
    blk = row0 // (n_sets * n)
    return pl.pallas_call(
        functools.partial(_route_kernel, n_sets=n_sets, n=n, cap=cap),
        grid=(1,),
        in_specs=[pl.BlockSpec((n_sets * n, LOGIT_LANES), lambda g: (blk, 0))],
        out_specs=[pl.BlockSpec((rows, n), lambda g: (0, 0)), pl.BlockSpec((rows, n), lambda g: (0, 0))],
        out_shape=[jax.ShapeDtypeStruct((rows, n), F32), jax.ShapeDtypeStruct((rows, n), F32)],
        scratch_shapes=[pltpu.VMEM((n, n), BF16)],
        compiler_params=_params(1),
        name=f"route_{n}",
    )(logits)


def _gather_kernel(key_ref, aff_ref, h2_ref, pt_ref, gate_ref, xg_ref, p_scr, *, sps, n, cap, d, nb):
    slot = lax.broadcasted_iota(jnp.int32, (cap, n), 0).astype(F32)
    per_group = 128 // cap
    for s in range(sps):
        key = key_ref[s * N_EXPERTS:(s + 1) * N_EXPERTS, :]
        aff = aff_ref[s * N_EXPERTS:(s + 1) * N_EXPERTS, :]
        for grp in range(N_EXPERTS // per_group):
            pieces = []
            for ex in range(grp * per_group, (grp + 1) * per_group):
                hit = slot == key[ex:ex + 1, :]
                gate_ref[ex, s] = jnp.sum(jnp.where(hit, aff[ex:ex + 1, :], 0.0), axis=1, keepdims=True)
                pieces.append(jnp.where(hit, 1.0, 0.0))
            hit128 = pieces[0] if per_group == 1 else jnp.concatenate(pieces, axis=0)
            p_scr[s, grp * 128:(grp + 1) * 128, :] = hit128.astype(BF16)
            pt_ref[s, :, grp * 128:(grp + 1) * 128] = hit128.T.astype(BF16)

        p = p_scr[s]
        for c in range(d // nb):
            xg = _bdot(p, h2_ref[s * n:(s + 1) * n, c * nb:(c + 1) * nb]).astype(BF16)
            for ex in range(N_EXPERTS):
                xg_ref[ex, s, :, c * nb:(c + 1) * nb] = xg[ex * cap:(ex + 1) * cap, :]


def _dispatch_gather(key, aff, h2, *, row0, n_sets, n, sps):
    d = h2.shape[1]
    cap = EC_FACTOR * n // N_EXPERTS
    slots = N_EXPERTS * cap
    blk0 = row0 // (sps * n)
    return pl.pallas_call(
        functools.partial(_gather_kernel, sps=sps, n=n, cap=cap, d=d, nb=256),
        grid=(n_sets // sps,),
        in_specs=[
            pl.BlockSpec((sps * N_EXPERTS, n), lambda b: (b, 0)),
            pl.BlockSpec((sps * N_EXPERTS, n), lambda b: (b, 0)),
            pl.BlockSpec((sps * n, d), lambda b: (blk0 + b, 0)),
        ],
        out_specs=[
            pl.BlockSpec((sps, n, slots), lambda b: (b, 0, 0)),
            pl.BlockSpec((N_EXPERTS, sps, cap, 1), lambda b: (0, b, 0, 0)),
            pl.BlockSpec((N_EXPERTS, sps, cap, d), lambda b: (0, b, 0, 0)),
        ],
        out_shape=[
            jax.ShapeDtypeStruct((n_sets, n, slots), BF16),
            jax.ShapeDtypeStruct((N_EXPERTS, n_sets, cap, 1), F32),
            jax.ShapeDtypeStruct((N_EXPERTS, n_sets, cap, d), BF16),
        ],
        scratch_shapes=[pltpu.VMEM((sps, slots, n), BF16)],
        compiler_params=_params(1),
        name=f"dispatch_gather_{n}",
    )(key, aff, h2)


def _expert_kernel(xc_ref, xl_ref, gc_ref, gl_ref, w1_ref, w3_ref, w2_ref, yc_ref, yl_ref, hc_scr, hl_scr,
                   *, n_f, fc, rows_c, rows_l, d):
    k = pl.program_id(1)

    @pl.when(k < n_f)
    def _():
        w1 = w1_ref[...].astype(BF16)
        w3 = w3_ref[...].astype(BF16)
        for x_ref, h_scr, rows in ((xc_ref, hc_scr, rows_c), (xl_ref, hl_scr, rows_l)):
            x = x_ref[...].reshape(rows, d)
            a = _bdot(x, w1)
            g = _bdot(x, w3)
            h_scr[k] = (_silu(a) * g).astype(BF16)

    @pl.when(k >= n_f)
    def _():
        w2 = w2_ref[...].astype(BF16)
        for h_scr, g_ref, y_ref, rows in ((hc_scr, gc_ref, yc_ref, rows_c), (hl_scr, gl_ref, yl_ref, rows_l)):
            y = _bdot(h_scr[0], w2[0:fc, :])
            for kk in range(1, n_f):
                y = y + _bdot(h_scr[kk], w2[kk * fc:(kk + 1) * fc, :])
            y_ref[...] = (y * g_ref[...].reshape(rows, 1)).astype(BF16).reshape(y_ref.shape)


def _experts(xg_c, xg_l, gate_c, gate_l, w_e1, w_e3, w_e2, *, fc=512, nc=512):
    n_e, sets_c, cap_c, d = xg_c.shape
    _, sets_l, cap_l, _ = xg_l.shape
    d_ff = w_e1.shape[2]
    n_f = d_ff // fc
    n_c = d // nc
    rows_c, rows_l = sets_c * cap_c, sets_l * cap_l

    def e_in(e, k):
        return jnp.minimum(e + (k >= n_f).astype(jnp.int32), n_e - 1)

    def f_idx(k):
        return jnp.where(k < n_f, k, 0)

    def c_idx(k):
        return jnp.maximum(k - n_f, 0)

    kern = functools.partial(_expert_kernel, n_f=n_f, fc=fc, rows_c=rows_c, rows_l=rows_l, d=d)
    return pl.pallas_call(
        kern,
        grid=(n_e, n_f + n_c),
        in_specs=[
            pl.BlockSpec((None, sets_c, cap_c, d), lambda e, k: (e_in(e, k), 0, 0, 0)),
            pl.BlockSpec((None, sets_l, cap_l, d), lambda e, k: (e_in(e, k), 0, 0, 0)),
            pl.BlockSpec((None, sets_c, cap_c, 1), lambda e, k: (e, 0, 0, 0)),
            pl.BlockSpec((None, sets_l, cap_l, 1), lambda e, k: (e, 0, 0, 0)),
            pl.BlockSpec((None, d, fc), lambda e, k: (e_in(e, k), 0, f_idx(k))),
            pl.BlockSpec((None, d, fc), lambda e, k: (e_in(e, k), 0, f_idx(k))),
            pl.BlockSpec((None, d_ff, nc), lambda e, k: (e, 0, c_idx(k))),
        ],
        out_specs=[
            pl.BlockSpec((None, sets_c, cap_c, nc), lambda e, k: (e, 0, 0, c_idx(k))),
            pl.BlockSpec((None, sets_l, cap_l, nc), lambda e, k: (e, 0, 0, c_idx(k))),
        ],
        out_shape=[
            jax.ShapeDtypeStruct(xg_c.shape, BF16),
            jax.ShapeDtypeStruct(xg_l.shape, BF16),
        ],
        scratch_shapes=[pltpu.VMEM((n_f, rows_c, fc), BF16), pltpu.VMEM((n_f, rows_l, fc), BF16)],
        compiler_params=_params(2),
        name="experts",
    )(xg_c, xg_l, gate_c, gate_l, w_e1, w_e3, w_e2)


def _combine_kernel(pt_ref, y_ref, x1_ref, gate_ref, g_ref, o_ref, acc_scr, *, sps, tn, slots, d, nb):
    for s in range(sps):
        pt = pt_ref[s]
        rows = slice(s * tn, (s + 1) * tn)
        for c in range(d // nb):
            cols = slice(c * nb, (c + 1) * nb)
            moe = _bdot(pt, y_ref[:, s, :, cols].reshape(slots, nb))
            acc_scr[:, cols] = x1_ref[rows, cols] + gate_ref[:, cols] * moe
        o_ref[rows, :] = _rms_rows(acc_scr[...], g_ref[...])


def _combine(pt, y, x1, mod6, g_final, *, row0, n_sets, n, sps, mod_row_fn):
    d = x1.shape[1]
    cap = EC_FACTOR * n // N_EXPERTS
    slots = N_EXPERTS * cap
    tn = min(n, 512)
    nt = n // tn
    assert sps == 1 or nt == 1
    blk0 = row0 // (sps * tn)
    return pl.pallas_call(
        functools.partial(_combine_kernel, sps=sps, tn=tn, slots=slots, d=d, nb=512),
        grid=(n_sets // sps, nt),
        in_specs=[
            pl.BlockSpec((sps, tn, slots), lambda b, r: (b, r, 0)),
            pl.BlockSpec((N_EXPERTS, sps, cap, d), lambda b, r: (0, b, 0, 0)),
            pl.BlockSpec((sps * tn, d), lambda b, r: (blk0 + b * nt + r, 0)),
            pl.BlockSpec((None, None, 1, d), lambda b, r: (5, mod_row_fn(b), 0, 0)),
            _const_spec((1, d)),
        ],
        out_specs=pl.BlockSpec((sps * tn, d), lambda b, r: (b * nt + r, 0)),
        out_shape=jax.ShapeDtypeStruct((n_sets * n, d), F32),
        scratch_shapes=[pltpu.VMEM((tn, d), F32)],
        compiler_params=_params(2),
        name=f"combine_{n}",
    )(pt, y, x1, mod6, g_final)


def _rope_tables(n):
    tpos = jnp.arange(n, dtype=jnp.int32)
    row = (tpos // GRID_W).astype(F32)
    col = (tpos % GRID_W).astype(F32)
    inv = 1.0 / (ROPE_BASE ** (jnp.arange(ROPE_FREQS, dtype=F32) / ROPE_FREQS))
    ang = jnp.stack([row[:, None] * inv, col[:, None] * inv], axis=1)
    cos = jnp.cos(ang)[:, :, None, :]
    sin = jnp.sin(ang)[:, :, None, :]
    cos = jnp.broadcast_to(cos, (n, 2, 2, ROPE_FREQS)).reshape(n, D_ROPE)
    sin = jnp.concatenate([-sin, sin], axis=2).reshape(n, D_ROPE)
    pad = jnp.zeros((n, 128 - D_ROPE), F32)
    return jnp.concatenate([cos, pad], axis=1), jnp.concatenate([sin, pad], axis=1)


def _swap_rotary_halves(w):
    perm = np.arange(D_ROPE) ^ ROPE_FREQS
    return w[..., perm]


def kernel(x_prompt, x_sample, c, cache_ckv, cache_krope, c_ctx, g_attn, g_ffn, w_ada, b_ada, w_in, g_qa,
           w_qb, g_kv, w_uk, w_uv, w_o_mla, g_sgu, w_s, b_s, w_o_gm, w_out, w_router, w_e1, w_e3, w_e2,
           g_final):
    batch, seq, d = x_prompt.shape
    dec_batch, dec_seq, _ = x_sample.shape
    depth = g_attn.shape[0]
    assert depth == 1
    q_lora, kv_lora = g_qa.shape[1], g_kv.shape[1]
    gm_width = g_sgu.shape[1]
    n_ctx, n_lat = batch * seq, dec_batch * dec_seq
    assert dec_batch < MOD_ROWS // 2 + 1

    xc = x_prompt.reshape(n_ctx, d)
    xl = x_sample.reshape(n_lat, d)

    c_rows = jnp.zeros((MOD_ROWS, d), F32).at[:dec_batch].set(c).at[MOD_ROWS // 2].set(c_ctx)
    n_mod_early = 2 * d
    mod_a = _modulation(c_rows, w_ada[0], b_ada[0][None, :], n_cols=n_mod_early)
    mod6_a = mod_a.reshape(MOD_ROWS, 2, 1, d).transpose(1, 0, 2, 3)

    cos_t, sin_t = _rope_tables(dec_seq)

    w_in_t = w_in[0].T
    o_kr = q_lora + kv_lora
    w_kr_t = w_in_t[o_kr:o_kr + D_ROPE]
    zpad = jnp.zeros((128 - D_ROPE, d), F32)
    w_small_t = jnp.concatenate(
        [w_in_t[:o_kr], w_kr_t, zpad, w_kr_t[np.arange(D_ROPE) ^ ROPE_FREQS], zpad], axis=0).astype(BF16)
    tm, bn_in = 512, 1024
    front = -(o_kr + D_ROPE) % bn_in

    wq3 = w_qb[0].reshape(q_lora, N_HEADS, D_NOPE + D_ROPE)
    wq_nope, wq_rope = wq3[:, :, :D_NOPE], wq3[:, :, D_NOPE:]
    zq = jnp.zeros((q_lora, N_HEADS, HEAD_SLOT - D_NOPE - D_ROPE), F32)
    wq_p = jnp.concatenate([wq_nope, wq_rope, zq], axis=2).reshape(q_lora, N_HEADS * HEAD_SLOT).astype(BF16)
    wq_sw = jnp.concatenate([_swap_rotary_halves(wq_rope), zq], axis=2).reshape(
        q_lora, N_HEADS * 128).astype(BF16)

    h, ckv, kr, q_pad, w_all_t = _pre_proj(xc, xl, g_attn, mod6_a, w_small_t, g_qa, g_kv, cos_t, sin_t,
                                           wq_p, wq_sw, w_in_t, front=front, tm=tm)
    big = _wide_proj(h, w_all_t, n_big=w_in_t.shape[0] - o_kr - D_ROPE, n_gelu=2 * gm_width, tm=1024,
                     bn=bn_in)

    w_uk2 = w_uk[0].reshape(kv_lora, N_HEADS // 2, 2 * D_NOPE).transpose(1, 0, 2).astype(BF16)
    w_uv2 = w_uv[0].reshape(kv_lora, N_HEADS // 2, 2 * D_V).transpose(1, 0, 2).astype(BF16)
    side = (c_rows, w_ada[0], b_ada[0][None, :], n_mod_early, (w_o_mla[0], w_o_gm[0], w_out[0]))
    o_ctx, mod_b, (w_o_mla_b, w_o_gm_b, w_out_b) = _attention(
        q_pad, ckv, kr, w_uk2, w_uv2, None, None,
        row0=0, n_req=batch, n_own=seq, tq=seq, heads_per_iter=N_HEADS, side=side)
    o_lat = _attention(q_pad, ckv, kr, w_uk2, w_uv2, cache_ckv[:, 0], cache_krope[:, 0],
                       row0=n_ctx, n_req=dec_batch, n_own=dec_seq, tq=512, heads_per_iter=4)
    mod6 = jnp.concatenate([mod6_a, mod_b.reshape(MOD_ROWS, 4, 1, d).transpose(1, 0, 2, 3)], axis=0)

    o_a = _o_proj(o_ctx, o_lat, w_o_mla_b, tm=1024)
    merged = _sgu_merge(big, o_a, g_sgu, w_s[0].astype(BF16), b_s[0].T, w_o_gm_b, tm=tm)

    w_router_p = jnp.concatenate(
        [w_router[0], jnp.zeros((d, LOGIT_LANES - N_EXPERTS), F32)], axis=1).astype(BF16)
    x1, h2, logits = _out_proj(merged, xc, xl, mod6, g_ffn, w_out_b, w_router_p,
                               n_lat_seq=dec_seq, tm=tm)

    key_c, aff_c = _route(logits, row0=0, n_sets=batch, n=seq)
    key_l, aff_l = _route(logits, row0=n_ctx, n_sets=dec_batch, n=dec_seq)
    p_c, gate_c, xg_c = _dispatch_gather(key_c, aff_c, h2, row0=0, n_sets=batch, n=seq, sps=4)
    p_l, gate_l, xg_l = _dispatch_gather(key_l, aff_l, h2, row0=n_ctx, n_sets=dec_batch, n=dec_seq, sps=1)

    y_c, y_l = _experts(xg_c, xg_l, gate_c, gate_l, w_e1[0], w_e3[0], w_e2[0])

    y_prompt = _combine(p_c, y_c, x1, mod6, g_final[None, :], row0=0, n_sets=batch, n=seq, sps=2,
                        mod_row_fn=lambda b: MOD_ROWS // 2)
    y_sample = _combine(p_l, y_l, x1, mod6, g_final[None, :], row0=n_ctx, n_sets=dec_batch, n=dec_seq, sps=1,
                        mod_row_fn=lambda b: b)

    new_ckv = ckv[:n_ctx].reshape(batch, 1, seq, kv_lora)
    new_krope = kr[:n_ctx, :D_ROPE].reshape(batch, 1, seq, D_ROPE)
    return (y_prompt.reshape(batch, seq, d), y_sample.reshape(dec_batch, dec_seq, d), new_ckv, new_krope)
```

```python
import functools

import numpy as np
import jax
import jax.numpy as jnp
from jax import lax
from jax.experimental import pallas as pl
from jax.experimental.pallas import tpu as pltpu

F32 = jnp.float32
BF16 = jnp.bfloat16

N_HEADS = 16
D_NOPE = 128
D_ROPE = 64
D_V = 128
HEAD_SLOT = 256
ROPE_FREQS = D_ROPE // 4
ROPE_BASE = 10000.0
GRID_W = 64
CHUNK = 128
GM_GROUPS = 8
N_EXPERTS = 16
EC_FACTOR = 2
EPS = 1e-6
MOD_ROWS = 8
LOGIT_LANES = 128

VMEM_LIMIT_V7X = 56 * 1024 * 1024

ROW_TILE = 512
MATMUL_ROW_TILE = 1024
WIDE_WEIGHT_BLOCK = 1024
ATTN_Q_TILE = 512
ATTN_HEADS_PER_STEP = 4
CTX_SETS_PER_GATHER_STEP = 4
CTX_SETS_PER_COMBINE_STEP = 2


def _params(n_axes):
    return pltpu.CompilerParams(
        dimension_semantics=("arbitrary",) * n_axes, vmem_limit_bytes=VMEM_LIMIT_V7X)


def _const_spec(shape):
    nd = len(shape)
    return pl.BlockSpec(shape, lambda *_: (0,) * nd, pipeline_mode=pl.Buffered(1))


def _half_tanh_half(x):
    return jnp.tanh(0.5 * x)


def _silu(x):
    h = 0.5 * x
    return h * jnp.tanh(h) + h


def _gelu_tanh(x):
    assert x.dtype == F32
    c = float(np.sqrt(2.0 / np.pi))
    return x * (0.5 * (1.0 + jnp.tanh(c * (x + 0.044715 * (x * x * x)))))


def _rms_rows(x, g):
    return x * lax.rsqrt(jnp.mean(x * x, axis=-1, keepdims=True) + EPS) * g


def _bdot(a, b):
    return jnp.dot(a, b, preferred_element_type=F32)


def _bdot_t(a, bt):
    return lax.dot_general(a, bt, (((1,), (1,)), ((), ())), preferred_element_type=F32)


def _mod_kernel(c_ref, w_ref, b_ref, o_ref):
    c = c_ref[...]
    s = _silu(c).astype(BF16)
    o_ref[...] = _bdot(s, w_ref[...].astype(BF16)) + b_ref[...]


def _modulation(c_rows, w_ada, b_ada, *, n_cols, bn=1024):
    d, n = w_ada.shape[0], n_cols
    return pl.pallas_call(
        _mod_kernel,
        grid=(n // bn,),
        in_specs=[
            _const_spec((MOD_ROWS, d)),
            pl.BlockSpec((d, bn), lambda j: (0, j)),
            pl.BlockSpec((1, bn), lambda j: (0, j)),
        ],
        out_specs=pl.BlockSpec((MOD_ROWS, bn), lambda j: (0, j)),
        out_shape=jax.ShapeDtypeStruct((MOD_ROWS, n), F32),
        compiler_params=_params(1),
        name="adaln_mod",
    )(c_rows, w_ada, b_ada)


def _preproj_kernel(xc_ref, xl_ref, g_ref, sc_ref, sh_ref, ws_ref, gqa_ref, gkv_ref, cos_ref, sin_ref,
                    wq_ref, wsw_ref, wlo_ref, whi_ref, h_ref, ckv_ref, kr_ref, q_ref, wcast_ref,
                    new_ckv_ref, new_kr_ref, *, n_ctx_tiles, q_lora, kv_lora, scale):
    i = pl.program_id(0)

    rb = wlo_ref.shape[0]
    lo = wlo_ref[...].astype(BF16)
    wcast_ref[0:rb, :] = jnp.where(i == 0, jnp.zeros_like(lo), lo)
    wcast_ref[rb:2 * rb, :] = whi_ref[...].astype(BF16)

    def body(x_ref, rotary):
        h = _rms_rows(x_ref[...], g_ref[...]) * (1.0 + sc_ref[...]) + sh_ref[...]
        hb = h.astype(BF16)
        h_ref[...] = hb
        small = _bdot_t(hb, ws_ref[...])
        qa = _rms_rows(small[:, :q_lora], gqa_ref[...]).astype(BF16)
        ckv = _rms_rows(small[:, q_lora:q_lora + kv_lora], gkv_ref[...])
        ckv_ref[...] = ckv
        o = q_lora + kv_lora
        kr = small[:, o:o + 128]
        if not rotary:
            kr_ref[...] = kr
            new_ckv_ref[...] = ckv
            new_kr_ref[...] = kr[:, :D_ROPE]
            for hd in range(N_HEADS):
                q = _bdot(qa, wq_ref[:, hd * HEAD_SLOT:(hd + 1) * HEAD_SLOT])
                q_ref[hd] = (q * scale).astype(BF16)
            return
        cos = cos_ref[...]
        sin = sin_ref[...]
        kr_sw = small[:, o + 128:o + 256]
        kr_ref[...] = kr * cos + kr_sw * sin
        for hp in range(N_HEADS // 2):
            q_sw = _bdot(qa, wsw_ref[:, hp * 256:(hp + 1) * 256])
            for s in range(2):
                hd = 2 * hp + s
                q = _bdot(qa, wq_ref[:, hd * HEAD_SLOT:(hd + 1) * HEAD_SLOT])
                q_ref[hd, :, 0:128] = (q[:, 0:128] * scale).astype(BF16)
                rot = q[:, 128:256] * cos + q_sw[:, s * 128:(s + 1) * 128] * sin
                q_ref[hd, :, 128:256] = (rot * scale).astype(BF16)

    @pl.when(i < n_ctx_tiles)
    def _():
        body(xc_ref, False)

    @pl.when(i >= n_ctx_tiles)
    def _():
        body(xl_ref, True)


def _pre_proj(xc, xl, g_attn, mod6, w_small_t, g_qa, g_kv, cos_t, sin_t, wq_p, wq_sw, w_in_t, *, front, tm):
    n_ctx, d = xc.shape
    n_lat_seq = cos_t.shape[0]
    t = n_ctx + xl.shape[0]
    q_lora, kv_lora = g_qa.shape[1], g_kv.shape[1]
    n_ctx_tiles = n_ctx // tm
    tiles_per_seq = n_lat_seq // tm
    n_wblk = w_in_t.shape[0] // front
    assert w_in_t.shape[0] == n_wblk * front and front % 16 == 0
    assert n_wblk + 1 == 2 * (t // tm), "weight row blocks must pair up with the token steps"

    def mod_row(i):
        return jnp.where(i < n_ctx_tiles, MOD_ROWS // 2, (i - n_ctx_tiles) // tiles_per_seq)

    def rope_blk(i):
        return jnp.maximum(i - n_ctx_tiles, 0) % tiles_per_seq

    scale = np.float32((D_NOPE + D_ROPE) ** -0.5)
    kern = functools.partial(_preproj_kernel, n_ctx_tiles=n_ctx_tiles, q_lora=q_lora, kv_lora=kv_lora,
                             scale=scale)
    return pl.pallas_call(
        kern,
        grid=(t // tm,),
        in_specs=[
            pl.BlockSpec((tm, d), lambda i: (jnp.minimum(i, n_ctx_tiles - 1), 0)),
            pl.BlockSpec((tm, d), lambda i: (jnp.maximum(i - n_ctx_tiles, 0), 0)),
            _const_spec((1, d)),
            pl.BlockSpec((None, None, 1, d), lambda i: (1, mod_row(i), 0, 0)),
            pl.BlockSpec((None, None, 1, d), lambda i: (0, mod_row(i), 0, 0)),
            _const_spec(w_small_t.shape),
            _const_spec((1, q_lora)),
            _const_spec((1, kv_lora)),
            pl.BlockSpec((tm, 128), lambda i: (rope_blk(i), 0)),
            pl.BlockSpec((tm, 128), lambda i: (rope_blk(i), 0)),
            _const_spec(wq_p.shape),
            _const_spec(wq_sw.shape),
            pl.BlockSpec((front, d), lambda i: (jnp.maximum(2 * i - 1, 0), 0)),
            pl.BlockSpec((front, d), lambda i: (2 * i, 0)),
        ],
        out_specs=[
            pl.BlockSpec((tm, d), lambda i: (i, 0)),
            pl.BlockSpec((tm, kv_lora), lambda i: (i, 0)),
            pl.BlockSpec((tm, 128), lambda i: (i, 0)),
            pl.BlockSpec((N_HEADS, tm, HEAD_SLOT), lambda i: (0, i, 0)),
            pl.BlockSpec((2 * front, d), lambda i: (i, 0)),
            pl.BlockSpec((tm, kv_lora), lambda i: (jnp.minimum(i, n_ctx_tiles - 1), 0)),
            pl.BlockSpec((tm, D_ROPE), lambda i: (jnp.minimum(i, n_ctx_tiles - 1), 0)),
        ],
        out_shape=[
            jax.ShapeDtypeStruct((t, d), BF16),
            jax.ShapeDtypeStruct((t, kv_lora), F32),
            jax.ShapeDtypeStruct((t, 128), F32),
            jax.ShapeDtypeStruct((N_HEADS, t, HEAD_SLOT), BF16),
            jax.ShapeDtypeStruct(((n_wblk + 1) * front, d), BF16),
            jax.ShapeDtypeStruct((n_ctx, kv_lora), F32),
            jax.ShapeDtypeStruct((n_ctx, D_ROPE), F32),
        ],
        compiler_params=_params(1),
        name="pre_proj",
    )(xc, xl, g_attn, mod6, mod6, w_small_t, g_qa, g_kv, cos_t, sin_t, wq_p, wq_sw, w_in_t, w_in_t)


def _wideproj_kernel(h_ref, wa_ref, wb_ref, o_ref, *, tm, bn, n_gelu_steps):
    j = pl.program_id(0)
    h = h_ref[...]

    def run(act):
        for c, w_ref in enumerate((wa_ref, wb_ref)):
            y = _bdot_t(h, w_ref[...])
            for r0 in range(0, tm, 256):
                o_ref[r0:r0 + 256, c * bn:(c + 1) * bn] = act(y[r0:r0 + 256, :]).astype(BF16)

    @pl.when(j < n_gelu_steps)
    def _():
        run(_gelu_tanh)

    @pl.when(j >= n_gelu_steps)
    def _():
        run(lambda y: y)


def _wide_proj(h, w_all_t, *, n_big, n_gelu, tm, bn):
    t, d = h.shape
    blk0 = (w_all_t.shape[0] - n_big) // bn
    return pl.pallas_call(
        functools.partial(_wideproj_kernel, tm=tm, bn=bn, n_gelu_steps=n_gelu // (2 * bn)),
        grid=(n_big // (2 * bn), t // tm),
        in_specs=[
            pl.BlockSpec((tm, d), lambda j, i: (i, 0)),
            pl.BlockSpec((bn, d), lambda j, i: (blk0 + 2 * j, 0)),
            pl.BlockSpec((bn, d), lambda j, i: (blk0 + 2 * j + 1, 0)),
        ],
        out_specs=pl.BlockSpec((tm, 2 * bn), lambda j, i: (i, j)),
        out_shape=jax.ShapeDtypeStruct((t, n_big), BF16),
        compiler_params=_params(2),
        name="wide_proj",
    )(h, w_all_t, w_all_t)


def _attn_kernel(*refs, n_own, n_cache, heads_per_iter, n_side):
    refs = list(refs)
    q_ref, ckv_ref, kr_ref = refs[:3]
    del refs[:3]
    if n_cache:
        cckv_ref, ckr_ref = refs[:2]
        del refs[:2]
    wuk_ref, wuv_ref = refs[:2]
    del refs[:2]
    if n_side:
        c_ref, wada_ref, bada_ref = refs[:3]
        side_in = refs[3:3 + n_side]
        del refs[:3 + n_side]
    o_ref = refs.pop(0)
    if n_side:
        mod_ref = refs.pop(0)
        side_out = refs[:n_side]
        del refs[:n_side]
        mod_ref[...] = _bdot(_silu(c_ref[...]).astype(BF16), wada_ref[...].astype(BF16)) + bada_ref[...]
        for src, dst in zip(side_in, side_out):
            dst[...] = src[...].astype(BF16)
    kpad, vexp, kall, krp, o_scr = refs
    qi = pl.program_id(1)

    @pl.when(qi == 0)
    def _():
        kall[0:n_own, :] = ckv_ref[...].astype(BF16)
        krp[0:n_own, :] = kr_ref[...].astype(BF16)
        if n_cache:
            kall[n_own:n_own + n_cache, :] = cckv_ref[...].astype(BF16)
            krp[n_own:n_own + n_cache, 0:D_ROPE] = ckr_ref[...].astype(BF16)
            krp[n_own:n_own + n_cache, D_ROPE:128] = jnp.zeros((n_cache, 128 - D_ROPE), BF16)

        def expand(hp, carry):
            kn = _bdot(kall[...], wuk_ref[hp]).astype(BF16)
            vv = _bdot(kall[...], wuv_ref[hp]).astype(BF16)
            for s in range(2):
                kpad[2 * hp + s, :, 0:128] = kn[:, s * 128:(s + 1) * 128]
                kpad[2 * hp + s, :, 128:256] = krp[...]
                vexp[2 * hp + s] = vv[:, s * 128:(s + 1) * 128]
            return carry

        if heads_per_iter == N_HEADS:
            for hp in range(N_HEADS // 2):
                expand(hp, 0)
        else:
            lax.fori_loop(0, N_HEADS // 2, expand, 0)

    def one_head(h):
        s = _bdot_t(q_ref[h], kpad[h])
        p = jnp.exp(s - jnp.max(s, axis=-1, keepdims=True))
        l = jnp.sum(p, axis=-1, keepdims=True)
        o = _bdot(p.astype(BF16), vexp[h])
        return (o / l).astype(BF16)

    if heads_per_iter == N_HEADS:
        for h in range(N_HEADS):
            o_ref[:, h * D_V:(h + 1) * D_V] = one_head(h)
    else:
        def head_group(hg, carry):
            for g in range(heads_per_iter):
                h = hg * heads_per_iter + g
                o_scr[h] = one_head(h)
            return carry

        lax.fori_loop(0, N_HEADS // heads_per_iter, head_group, 0)
        for h in range(N_HEADS):
            o_ref[:, h * D_V:(h + 1) * D_V] = o_scr[h]


def _attention(q_pad, ckv, kr, w_uk, w_uv, cache_ckv, cache_kr, *, row0, n_req, n_own, tq, heads_per_iter,
               side=None):
    kv_lora = ckv.shape[1]
    n_cache = 0 if cache_ckv is None else cache_ckv.shape[1]
    kn = n_own + n_cache
    qb = n_own // tq
    in_specs = [
        pl.BlockSpec((N_HEADS, tq, HEAD_SLOT), lambda b, qi: (0, row0 // tq + b * qb + qi, 0)),
        pl.BlockSpec((n_own, kv_lora), lambda b, qi: (row0 // n_own + b, 0)),
        pl.BlockSpec((n_own, 128), lambda b, qi: (row0 // n_own + b, 0)),
    ]
    args = [q_pad, ckv, kr]
    if n_cache:
        in_specs += [
            pl.BlockSpec((None, n_cache, kv_lora), lambda b, qi: (b, 0, 0)),
            pl.BlockSpec((None, n_cache, D_ROPE), lambda b, qi: (b, 0, 0)),
        ]
        args += [cache_ckv, cache_kr]
    in_specs += [_const_spec(w_uk.shape), _const_spec(w_uv.shape)]
    args += [w_uk, w_uv]
    out_specs = [pl.BlockSpec((tq, N_HEADS * D_V), lambda b, qi: (b * qb + qi, 0))]
    out_shape = [jax.ShapeDtypeStruct((n_req * n_own, N_HEADS * D_V), BF16)]
    n_side = 0
    if side is not None:
        assert qb == 1
        c_rows, w_ada, b_ada, col0, weights = side
        d_model, n_mod = w_ada.shape
        mcols = (n_mod - col0) // n_req
        assert mcols * n_req == n_mod - col0 and mcols % 128 == 0 and col0 % mcols == 0
        in_specs += [
            _const_spec(c_rows.shape),
            pl.BlockSpec((d_model, mcols), lambda b, qi: (0, col0 // mcols + b)),
            pl.BlockSpec((1, mcols), lambda b, qi: (0, col0 // mcols + b)),
        ]
        args += [c_rows, w_ada, b_ada]
        out_specs.append(pl.BlockSpec((MOD_ROWS, mcols), lambda b, qi: (0, b)))
        out_shape.append(jax.ShapeDtypeStruct((MOD_ROWS, n_mod - col0), F32))
        n_side = len(weights)
        assert n_side > 0
        for w in weights:
            rows = w.shape[0] // n_req
            assert rows * n_req == w.shape[0] and rows % 16 == 0
            in_specs.append(pl.BlockSpec((rows, w.shape[1]), lambda b, qi: (b, 0)))
            args.append(w)
            out_specs.append(pl.BlockSpec((rows, w.shape[1]), lambda b, qi: (b, 0)))
            out_shape.append(jax.ShapeDtypeStruct(w.shape, BF16))
    res = pl.pallas_call(
        functools.partial(_attn_kernel, n_own=n_own, n_cache=n_cache, heads_per_iter=heads_per_iter,
                          n_side=n_side),
        grid=(n_req, qb),
        in_specs=in_specs,
        out_specs=out_specs,
        out_shape=out_shape,
        scratch_shapes=[
            pltpu.VMEM((N_HEADS, kn, HEAD_SLOT), BF16),
            pltpu.VMEM((N_HEADS, kn, D_V), BF16),
            pltpu.VMEM((kn, kv_lora), BF16),
            pltpu.VMEM((kn, 128), BF16),
            pltpu.VMEM((N_HEADS, tq, D_V), BF16),
        ],
        compiler_params=_params(2),
        name="mla_attn_cache" if n_cache else "mla_attn",
    )(*args)
    if side is None:
        return res[0]
    return res[0], res[1], res[2:]


def _oproj_kernel(xc_ref, xl_ref, w_ref, o_ref, *, n_ctx_tiles):
    i = pl.program_id(0)

    @pl.when(i < n_ctx_tiles)
    def _():
        o_ref[...] = _bdot(xc_ref[...], w_ref[...]).astype(o_ref.dtype)

    @pl.when(i >= n_ctx_tiles)
    def _():
        o_ref[...] = _bdot(xl_ref[...], w_ref[...]).astype(o_ref.dtype)


def _o_proj(xc, xl, w, *, tm):
    n_ctx, k = xc.shape
    t = n_ctx + xl.shape[0]
    n = w.shape[1]
    n_ctx_tiles = n_ctx // tm
    return pl.pallas_call(
        functools.partial(_oproj_kernel, n_ctx_tiles=n_ctx_tiles),
        grid=(t // tm,),
        in_specs=[
            pl.BlockSpec((tm, k), lambda i: (jnp.minimum(i, n_ctx_tiles - 1), 0)),
            pl.BlockSpec((tm, k), lambda i: (jnp.maximum(i - n_ctx_tiles, 0), 0)),
            _const_spec(w.shape),
        ],
        out_specs=pl.BlockSpec((tm, n), lambda i: (i, 0)),
        out_shape=jax.ShapeDtypeStruct((t, n), BF16),
        compiler_params=_params(1),
        name="o_proj",
    )(xc, xl, w)


def _sgu_merge_kernel(gu_ref, gv_ref, ga_ref, gb_ref, oa_ref, g_ref, ws_ref, bs_ref, wo_ref, m_ref, z_scr,
                      *, tm, gc):
    s = pl.program_id(0)

    @pl.when(s == 0)
    def _():
        z_scr[1] = jnp.zeros(z_scr.shape[1:], BF16)

    o_b = _bdot(z_scr[(s + 1) % 2], wo_ref[...])
    for r0 in range(0, tm, CHUNK):
        rows = slice(r0, r0 + CHUNK)
        oa = oa_ref[rows, :]
        ob = o_b[rows, :].astype(BF16)
        ta = _half_tanh_half(ga_ref[rows, :])
        tb = _half_tanh_half(gb_ref[rows, :])
        m_ref[rows, :] = 0.5 * ((oa + ob) + (ta * oa + tb * ob))

    slot = s % 2
    g_bf = g_ref[...].astype(BF16)
    for r0 in range(0, tm, CHUNK):
        v = gv_ref[r0:r0 + CHUNK, :]
        v32 = v.astype(F32)
        inv = lax.rsqrt(jnp.mean(v32 * v32, axis=-1, keepdims=True) + EPS)
        vn = v * inv.astype(BF16) * g_bf
        for g in range(GM_GROUPS):
            c0 = g * gc
            mix = _bdot(ws_ref[g], vn[:, c0:c0 + gc]) + bs_ref[:, g:g + 1]
            u = gu_ref[r0:r0 + CHUNK, c0:c0 + gc]
            z_scr[slot, r0:r0 + CHUNK, c0:c0 + gc] = u * mix.astype(BF16)


def _sgu_merge(big, o_a, g_sgu, w_s, b_s_t, w_o_gm, *, tm):
    t = big.shape[0]
    width = g_sgu.shape[1]
    d = w_o_gm.shape[1]
    gc = width // GM_GROUPS
    nt = t // tm

    def fill(s):
        return jnp.minimum(s, nt - 1)

    def drain(s):
        return jnp.maximum(s - 1, 0)

    return pl.pallas_call(
        functools.partial(_sgu_merge_kernel, tm=tm, gc=gc),
        grid=(nt + 1,),
        in_specs=[
            pl.BlockSpec((tm, width), lambda s: (fill(s), 0)),
            pl.BlockSpec((tm, width), lambda s: (fill(s), 1)),
            pl.BlockSpec((tm, d), lambda s: (drain(s), 2)),
            pl.BlockSpec((tm, d), lambda s: (drain(s), 3)),
            pl.BlockSpec((tm, d), lambda s: (drain(s), 0)),
            _const_spec((1, width)),
            _const_spec(w_s.shape),
            _const_spec(b_s_t.shape),
            _const_spec(w_o_gm.shape),
        ],
        out_specs=pl.BlockSpec((tm, d), lambda s: (drain(s), 0)),
        out_shape=jax.ShapeDtypeStruct((t, d), BF16),
        scratch_shapes=[pltpu.VMEM((2, tm, width), BF16)],
        compiler_params=_params(1),
        name="sgu_merge",
    )(big, big, big, big, o_a, g_sgu, w_s, b_s_t, w_o_gm)


def _outproj_kernel(m_ref, xc_ref, xl_ref, gate_ref, sc_ref, sh_ref, g_ref, wo_ref, wr_ref,
                    x1_ref, h2_ref, lg_ref, *, tm, n_ctx_tiles):
    i = pl.program_id(0)

    def body(x_ref):
        half = tm // 2
        for b0 in range(0, tm, half):
            r = _bdot(m_ref[b0:b0 + half, :], wo_ref[...])
            for r0 in range(0, half, CHUNK):
                rows = slice(b0 + r0, b0 + r0 + CHUNK)
                x1 = x_ref[rows, :] + gate_ref[...] * r[r0:r0 + CHUNK, :]
                x1_ref[rows, :] = x1
                h2 = (_rms_rows(x1, g_ref[...]) * (1.0 + sc_ref[...]) + sh_ref[...]).astype(BF16)
                h2_ref[rows, :] = h2
                lg_ref[rows, :] = _bdot(h2, wr_ref[...])

    @pl.when(i < n_ctx_tiles)
    def _():
        body(xc_ref)

    @pl.when(i >= n_ctx_tiles)
    def _():
        body(xl_ref)


def _out_proj(merged, xc, xl, mod6, g_ffn, w_out, w_router_p, *, n_lat_seq, tm):
    n_ctx, d = xc.shape
    t = n_ctx + xl.shape[0]
    n_ctx_tiles = n_ctx // tm
    tiles_per_seq = n_lat_seq // tm

    def mod_row(i):
        return jnp.where(i < n_ctx_tiles, MOD_ROWS // 2, (i - n_ctx_tiles) // tiles_per_seq)

    return pl.pallas_call(
        functools.partial(_outproj_kernel, tm=tm, n_ctx_tiles=n_ctx_tiles),
        grid=(t // tm,),
        in_specs=[
            pl.BlockSpec((tm, d), lambda i: (i, 0)),
            pl.BlockSpec((tm, d), lambda i: (jnp.minimum(i, n_ctx_tiles - 1), 0)),
            pl.BlockSpec((tm, d), lambda i: (jnp.maximum(i - n_ctx_tiles, 0), 0)),
            pl.BlockSpec((None, None, 1, d), lambda i: (2, mod_row(i), 0, 0)),
            pl.BlockSpec((None, None, 1, d), lambda i: (4, mod_row(i), 0, 0)),
            pl.BlockSpec((None, None, 1, d), lambda i: (3, mod_row(i), 0, 0)),
            _const_spec((1, d)),
            _const_spec(w_out.shape),
            _const_spec(w_router_p.shape),
        ],
        out_specs=[
            pl.BlockSpec((tm, d), lambda i: (i, 0)),
            pl.BlockSpec((tm, d), lambda i: (i, 0)),
            pl.BlockSpec((tm, LOGIT_LANES), lambda i: (i, 0)),
        ],
        out_shape=[
            jax.ShapeDtypeStruct((t, d), F32),
            jax.ShapeDtypeStruct((t, d), BF16),
            jax.ShapeDtypeStruct((t, LOGIT_LANES), F32),
        ],
        compiler_params=_params(1),
        name="out_proj",
    )(merged, xc, xl, mod6, mod6, mod6, g_ffn, w_out, w_router_p)


BISECT_STEPS = 48
MIN_NORMAL_F32 = float(np.finfo(np.float32).tiny)


def _route_kernel(lg_ref, key_ref, aff_ref, tri_scr, *, n_sets, n, cap):
    for r0 in range(0, n, 128):
        r = lax.broadcasted_iota(jnp.int32, (128, n), 0) + r0
        c = lax.broadcasted_iota(jnp.int32, (128, n), 1)
        tri_scr[r0:r0 + 128, :] = jnp.where(r < c, 1.0, 0.0).astype(BF16)

    for s in range(n_sets):
        logits = lg_ref[s * n:(s + 1) * n, :].T[0:N_EXPERTS, :]
        e = jnp.exp(logits - jnp.max(logits, axis=0, keepdims=True))
        aff_ref[s * N_EXPERTS:(s + 1) * N_EXPERTS, :] = e / jnp.sum(e, axis=0, keepdims=True)
    aff = aff_ref[...]
    rows = n_sets * N_EXPERTS

    def count_ge(thr):
        return jnp.sum(jnp.where(aff >= thr, 1.0, 0.0), axis=1, keepdims=True)

    def bisect(_, carry):
        lo, hi = carry
        mid = jnp.sqrt(lo) * jnp.sqrt(hi)
        ok = count_ge(mid) >= cap
        return jnp.where(ok, mid, lo), jnp.where(ok, hi, mid)

    lo0 = jnp.full((rows, 1), MIN_NORMAL_F32, F32)
    hi0 = jnp.full((rows, 1), 2.0, F32)
    lo, hi = lax.fori_loop(0, BISECT_STEPS, bisect, (lo0, hi0))
    lo = jnp.where(count_ge(lo) >= cap, lo, 0.0)

    above = aff >= hi
    band = (aff >= lo) & jnp.logical_not(above)
    need = cap - jnp.sum(jnp.where(above, 1.0, 0.0), axis=1, keepdims=True)
    tri = tri_scr[...]
    band_before = _bdot(jnp.where(band, 1.0, 0.0).astype(BF16), tri)
    sel = above | (band & (band_before < need))
    pos = _bdot(jnp.where(sel, 1.0, 0.0).astype(BF16), tri)
    key_ref[...] = jnp.where(sel, pos, -1.0)


def _route(logits, *, row0, n_sets, n):
    cap = EC_FACTOR * n // N_EXPERTS
    rows = n_sets * N_EXPERTS
    blk = row0 // (n_sets * n)
    return pl.pallas_call(
        functools.partial(_route_kernel, n_sets=n_sets, n=n, cap=cap),
        grid=(1,),
        in_specs=[pl.BlockSpec((n_sets * n, LOGIT_LANES), lambda g: (blk, 0))],
        out_specs=[pl.BlockSpec((rows, n), lambda g: (0, 0)), pl.BlockSpec((rows, n), lambda g: (0, 0))],
        out_shape=[jax.ShapeDtypeStruct((rows, n), F32), jax.ShapeDtypeStruct((rows, n), F32)],
        scratch_shapes=[pltpu.VMEM((n, n), BF16)],
        compiler_params=_params(1),
        name=f"route_{n}",
    )(logits)


def _gather_kernel(key_ref, aff_ref, h2_ref, pt_ref, gate_ref, xg_ref, p_scr, *, sps, n, cap, d, nb):
    slot = lax.broadcasted_iota(jnp.int32, (cap, n), 0).astype(F32)
    per_group = 128 // cap
    for s in range(sps):
        key = key_ref[s * N_EXPERTS:(s + 1) * N_EXPERTS, :]
        aff = aff_ref[s * N_EXPERTS:(s + 1) * N_EXPERTS, :]
        for grp in range(N_EXPERTS // per_group):
            pieces = []
            for ex in range(grp * per_group, (grp + 1) * per_group):
                hit = slot == key[ex:ex + 1, :]
                gate_ref[ex, s] = jnp.sum(jnp.where(hit, aff[ex:ex + 1, :], 0.0), axis=1, keepdims=True)
                pieces.append(jnp.where(hit, 1.0, 0.0))
            hit128 = pieces[0] if per_group == 1 else jnp.concatenate(pieces, axis=0)
            p_scr[s, grp * 128:(grp + 1) * 128, :] = hit128.astype(BF16)
            pt_ref[s, :, grp * 128:(grp + 1) * 128] = hit128.T.astype(BF16)

        p = p_scr[s]
        for c in range(d // nb):
            xg = _bdot(p, h2_ref[s * n:(s + 1) * n, c * nb:(c + 1) * nb]).astype(BF16)
            for ex in range(N_EXPERTS):
                xg_ref[ex, s, :, c * nb:(c + 1) * nb] = xg[ex * cap:(ex + 1) * cap, :]


def _dispatch_gather(key, aff, h2, *, row0, n_sets, n, sps):
    d = h2.shape[1]
    cap = EC_FACTOR * n // N_EXPERTS
    slots = N_EXPERTS * cap
    blk0 = row0 // (sps * n)
    return pl.pallas_call(
        functools.partial(_gather_kernel, sps=sps, n=n, cap=cap, d=d, nb=256),
        grid=(n_sets // sps,),
        in_specs=[
            pl.BlockSpec((sps * N_EXPERTS, n), lambda b: (b, 0)),
            pl.BlockSpec((sps * N_EXPERTS, n), lambda b: (b, 0)),
            pl.BlockSpec((sps * n, d), lambda b: (blk0 + b, 0)),
        ],
        out_specs=[
            pl.BlockSpec((sps, n, slots), lambda b: (b, 0, 0)),
            pl.BlockSpec((N_EXPERTS, sps, cap, 1), lambda b: (0, b, 0, 0)),
            pl.BlockSpec((N_EXPERTS, sps, cap, d), lambda b: (0, b, 0, 0)),
        ],
        out_shape=[
            jax.ShapeDtypeStruct((n_sets, n, slots), BF16),
            jax.ShapeDtypeStruct((N_EXPERTS, n_sets, cap, 1), F32),
            jax.ShapeDtypeStruct((N_EXPERTS, n_sets, cap, d), BF16),
        ],
        scratch_shapes=[pltpu.VMEM((sps, slots, n), BF16)],
        compiler_params=_params(1),
        name=f"dispatch_gather_{n}",
    )(key, aff, h2)


def _expert_kernel(xc_ref, xl_ref, gc_ref, gl_ref, w1_ref, w3_ref, w2_ref, yc_ref, yl_ref, hc_scr, hl_scr,
                   *, n_f, fc, rows_c, rows_l, d):
    k = pl.program_id(1)

    @pl.when(k < n_f)
    def _():
        w1 = w1_ref[...].astype(BF16)
        w3 = w3_ref[...].astype(BF16)
        for x_ref, h_scr, rows in ((xc_ref, hc_scr, rows_c), (xl_ref, hl_scr, rows_l)):
            x = x_ref[...].reshape(rows, d)
            a = _bdot(x, w1)
            g = _bdot(x, w3)
            h_scr[k] = (_silu(a) * g).astype(BF16)

    @pl.when(k >= n_f)
    def _():
        w2 = w2_ref[...].astype(BF16)
        for h_scr, g_ref, y_ref, rows in ((hc_scr, gc_ref, yc_ref, rows_c), (hl_scr, gl_ref, yl_ref, rows_l)):
            y = _bdot(h_scr[0], w2[0:fc, :])
            for kk in range(1, n_f):
                y = y + _bdot(h_scr[kk], w2[kk * fc:(kk + 1) * fc, :])
            y_ref[...] = (y * g_ref[...].reshape(rows, 1)).astype(BF16).reshape(y_ref.shape)


def _experts(xg_c, xg_l, gate_c, gate_l, w_e1, w_e3, w_e2, *, fc=512, nc=512):
    n_e, sets_c, cap_c, d = xg_c.shape
    _, sets_l, cap_l, _ = xg_l.shape
    d_ff = w_e1.shape[2]
    n_f = d_ff // fc
    n_c = d // nc
    rows_c, rows_l = sets_c * cap_c, sets_l * cap_l

    def e_in(e, k):
        return jnp.minimum(e + (k >= n_f).astype(jnp.int32), n_e - 1)

    def f_idx(k):
        return jnp.where(k < n_f, k, 0)

    def c_idx(k):
        return jnp.maximum(k - n_f, 0)

    kern = functools.partial(_expert_kernel, n_f=n_f, fc=fc, rows_c=rows_c, rows_l=rows_l, d=d)
    return pl.pallas_call(
        kern,
        grid=(n_e, n_f + n_c),
        in_specs=[
            pl.BlockSpec((None, sets_c, cap_c, d), lambda e, k: (e_in(e, k), 0, 0, 0)),
            pl.BlockSpec((None, sets_l, cap_l, d), lambda e, k: (e_in(e, k), 0, 0, 0)),
            pl.BlockSpec((None, sets_c, cap_c, 1), lambda e, k: (e, 0, 0, 0)),
            pl.BlockSpec((None, sets_l, cap_l, 1), lambda e, k: (e, 0, 0, 0)),
            pl.BlockSpec((None, d, fc), lambda e, k: (e_in(e, k), 0, f_idx(k))),
            pl.BlockSpec((None, d, fc), lambda e, k: (e_in(e, k), 0, f_idx(k))),
            pl.BlockSpec((None, d_ff, nc), lambda e, k: (e, 0, c_idx(k))),
        ],
        out_specs=[
            pl.BlockSpec((None, sets_c, cap_c, nc), lambda e, k: (e, 0, 0, c_idx(k))),
            pl.BlockSpec((None, sets_l, cap_l, nc), lambda e, k: (e, 0, 0, c_idx(k))),
        ],
        out_shape=[
            jax.ShapeDtypeStruct(xg_c.shape, BF16),
            jax.ShapeDtypeStruct(xg_l.shape, BF16),
        ],
        scratch_shapes=[pltpu.VMEM((n_f, rows_c, fc), BF16), pltpu.VMEM((n_f, rows_l, fc), BF16)],
        compiler_params=_params(2),
        name="experts",
    )(xg_c, xg_l, gate_c, gate_l, w_e1, w_e3, w_e2)


def _combine_kernel(pt_ref, y_ref, x1_ref, gate_ref, g_ref, o_ref, acc_scr, *, sps, tn, slots, d, nb):
    for s in range(sps):
        pt = pt_ref[s]
        rows = slice(s * tn, (s + 1) * tn)
        for c in range(d // nb):
            cols = slice(c * nb, (c + 1) * nb)
            moe = _bdot(pt, y_ref[:, s, :, cols].reshape(slots, nb))
            acc_scr[:, cols] = x1_ref[rows, cols] + gate_ref[:, cols] * moe
        o_ref[rows, :] = _rms_rows(acc_scr[...], g_ref[...])


def _combine(pt, y, x1, mod6, g_final, *, row0, n_sets, n, sps, mod_row_fn):
    d = x1.shape[1]
    cap = EC_FACTOR * n // N_EXPERTS
    slots = N_EXPERTS * cap
    tn = min(n, 512)
    nt = n // tn
    assert sps == 1 or nt == 1
    blk0 = row0 // (sps * tn)
    return pl.pallas_call(
        functools.partial(_combine_kernel, sps=sps, tn=tn, slots=slots, d=d, nb=512),
        grid=(n_sets // sps, nt),
        in_specs=[
            pl.BlockSpec((sps, tn, slots), lambda b, r: (b, r, 0)),
            pl.BlockSpec((N_EXPERTS, sps, cap, d), lambda b, r: (0, b, 0, 0)),
            pl.BlockSpec((sps * tn, d), lambda b, r: (blk0 + b * nt + r, 0)),
            pl.BlockSpec((None, None, 1, d), lambda b, r: (5, mod_row_fn(b), 0, 0)),
            _const_spec((1, d)),
        ],
        out_specs=pl.BlockSpec((sps * tn, d), lambda b, r: (b * nt + r, 0)),
        out_shape=jax.ShapeDtypeStruct((n_sets * n, d), F32),
        scratch_shapes=[pltpu.VMEM((tn, d), F32)],
        compiler_params=_params(2),
        name=f"combine_{n}",
    )(pt, y, x1, mod6, g_final)


def _rope_tables(n):
    tpos = jnp.arange(n, dtype=jnp.int32)
    row = (tpos // GRID_W).astype(F32)
    col = (tpos % GRID_W).astype(F32)
    inv = 1.0 / (ROPE_BASE ** (jnp.arange(ROPE_FREQS, dtype=F32) / ROPE_FREQS))
    ang = jnp.stack([row[:, None] * inv, col[:, None] * inv], axis=1)
    cos = jnp.cos(ang)[:, :, None, :]
    sin = jnp.sin(ang)[:, :, None, :]
    cos = jnp.broadcast_to(cos, (n, 2, 2, ROPE_FREQS)).reshape(n, D_ROPE)
    sin = jnp.concatenate([-sin, sin], axis=2).reshape(n, D_ROPE)
    pad = jnp.zeros((n, 128 - D_ROPE), F32)
    return jnp.concatenate([cos, pad], axis=1), jnp.concatenate([sin, pad], axis=1)


def _swap_rotary_halves(w):
    perm = np.arange(D_ROPE) ^ ROPE_FREQS
    return w[..., perm]


def kernel(x_prompt, x_sample, c, cache_ckv, cache_krope, c_ctx, g_attn, g_ffn, w_ada, b_ada, w_in, g_qa,
           w_qb, g_kv, w_uk, w_uv, w_o_mla, g_sgu, w_s, b_s, w_o_gm, w_out, w_router, w_e1, w_e3, w_e2,
           g_final):
    batch, seq, d = x_prompt.shape
    dec_batch, dec_seq, _ = x_sample.shape
    depth = g_attn.shape[0]
    assert depth == 1
    q_lora, kv_lora = g_qa.shape[1], g_kv.shape[1]
    gm_width = g_sgu.shape[1]
    n_ctx, n_lat = batch * seq, dec_batch * dec_seq
    assert dec_batch < MOD_ROWS // 2 + 1

    xc = x_prompt.reshape(n_ctx, d)
    xl = x_sample.reshape(n_lat, d)

    c_rows = jnp.zeros((MOD_ROWS, d), F32).at[:dec_batch].set(c).at[MOD_ROWS // 2].set(c_ctx)
    n_mod_early = 2 * d
    mod_a = _modulation(c_rows, w_ada[0], b_ada[0][None, :], n_cols=n_mod_early)
    mod6_a = mod_a.reshape(MOD_ROWS, 2, 1, d).transpose(1, 0, 2, 3)

    cos_t, sin_t = _rope_tables(dec_seq)

    w_in_t = w_in[0].T
    o_kr = q_lora + kv_lora
    w_kr_t = w_in_t[o_kr:o_kr + D_ROPE]
    zpad = jnp.zeros((128 - D_ROPE, d), F32)
    w_small_t = jnp.concatenate(
        [w_in_t[:o_kr], w_kr_t, zpad, w_kr_t[np.arange(D_ROPE) ^ ROPE_FREQS], zpad], axis=0).astype(BF16)
    tm = ROW_TILE
    front = -(o_kr + D_ROPE) % WIDE_WEIGHT_BLOCK

    wq3 = w_qb[0].reshape(q_lora, N_HEADS, D_NOPE + D_ROPE)
    wq_nope, wq_rope = wq3[:, :, :D_NOPE], wq3[:, :, D_NOPE:]
    zq = jnp.zeros((q_lora, N_HEADS, HEAD_SLOT - D_NOPE - D_ROPE), F32)
    wq_p = jnp.concatenate([wq_nope, wq_rope, zq], axis=2).reshape(q_lora, N_HEADS * HEAD_SLOT).astype(BF16)
    wq_sw = jnp.concatenate([_swap_rotary_halves(wq_rope), zq], axis=2).reshape(
        q_lora, N_HEADS * 128).astype(BF16)

    h, ckv, kr, q_pad, w_all_t, ckv_ctx, kr_ctx = _pre_proj(
        xc, xl, g_attn, mod6_a, w_small_t, g_qa, g_kv, cos_t, sin_t, wq_p, wq_sw, w_in_t, front=front, tm=tm)
    big = _wide_proj(h, w_all_t, n_big=w_in_t.shape[0] - o_kr - D_ROPE, n_gelu=2 * gm_width,
                     tm=MATMUL_ROW_TILE, bn=WIDE_WEIGHT_BLOCK)

    w_uk2 = w_uk[0].reshape(kv_lora, N_HEADS // 2, 2 * D_NOPE).transpose(1, 0, 2).astype(BF16)
    w_uv2 = w_uv[0].reshape(kv_lora, N_HEADS // 2, 2 * D_V).transpose(1, 0, 2).astype(BF16)
    side = (c_rows, w_ada[0], b_ada[0][None, :], n_mod_early, (w_o_mla[0], w_o_gm[0], w_out[0]))
    o_ctx, mod_b, (w_o_mla_b, w_o_gm_b, w_out_b) = _attention(
        q_pad, ckv, kr, w_uk2, w_uv2, None, None,
        row0=0, n_req=batch, n_own=seq, tq=seq, heads_per_iter=N_HEADS, side=side)
    o_lat = _attention(q_pad, ckv, kr, w_uk2, w_uv2, cache_ckv[:, 0], cache_krope[:, 0],
                       row0=n_ctx, n_req=dec_batch, n_own=dec_seq, tq=ATTN_Q_TILE,
                       heads_per_iter=ATTN_HEADS_PER_STEP)
    mod6 = jnp.concatenate([mod6_a, mod_b.reshape(MOD_ROWS, 4, 1, d).transpose(1, 0, 2, 3)], axis=0)

    o_a = _o_proj(o_ctx, o_lat, w_o_mla_b, tm=MATMUL_ROW_TILE)
    merged = _sgu_merge(big, o_a, g_sgu, w_s[0].astype(BF16), b_s[0].T, w_o_gm_b, tm=tm)

    w_router_p = jnp.concatenate(
        [w_router[0], jnp.zeros((d, LOGIT_LANES - N_EXPERTS), F32)], axis=1).astype(BF16)
    x1, h2, logits = _out_proj(merged, xc, xl, mod6, g_ffn, w_out_b, w_router_p,
                               n_lat_seq=dec_seq, tm=tm)

    key_c, aff_c = _route(logits, row0=0, n_sets=batch, n=seq)
    key_l, aff_l = _route(logits, row0=n_ctx, n_sets=dec_batch, n=dec_seq)
    p_c, gate_c, xg_c = _dispatch_gather(key_c, aff_c, h2, row0=0, n_sets=batch, n=seq,
                                         sps=CTX_SETS_PER_GATHER_STEP)
    p_l, gate_l, xg_l = _dispatch_gather(key_l, aff_l, h2, row0=n_ctx, n_sets=dec_batch, n=dec_seq, sps=1)

    y_c, y_l = _experts(xg_c, xg_l, gate_c, gate_l, w_e1[0], w_e3[0], w_e2[0])

    y_prompt = _combine(p_c, y_c, x1, mod6, g_final[None, :], row0=0, n_sets=batch, n=seq,
                        sps=CTX_SETS_PER_COMBINE_STEP, mod_row_fn=lambda b: MOD_ROWS // 2)
    y_sample = _combine(p_l, y_l, x1, mod6, g_final[None, :], row0=n_ctx, n_sets=dec_batch, n=dec_seq, sps=1,
                        mod_row_fn=lambda b: b)

    new_ckv = ckv_ctx.reshape(batch, 1, seq, kv_lora)
    new_krope = kr_ctx.reshape(batch, 1, seq, D_ROPE)
    return (y_prompt.reshape(batch, seq, d), y_sample.reshape(dec_batch, dec_seq, d), new_ckv, new_krope)
```

```python
import functools

import numpy as np
import jax
import jax.numpy as jnp
from jax import lax
from jax.experimental import pallas as pl
from jax.experimental.pallas import tpu as pltpu

F32 = jnp.float32
BF16 = jnp.bfloat16

N_HEADS = 16
D_NOPE = 128
D_ROPE = 64
D_V = 128
HEAD_SLOT = 256
ROPE_FREQS = D_ROPE // 4
ROPE_BASE = 10000.0
GRID_W = 64
CHUNK = 128
GM_GROUPS = 8
N_EXPERTS = 16
EC_FACTOR = 2
EPS = 1e-6
MOD_ROWS = 8
LOGIT_LANES = 128

VMEM_LIMIT_V7X = 56 * 1024 * 1024

ROW_TILE = 512
MATMUL_ROW_TILE = 1024
WIDE_WEIGHT_BLOCK = 1024
ATTN_Q_TILE = 512
ATTN_HEADS_PER_STEP = 4
CTX_SETS_PER_GATHER_STEP = 4
CTX_SETS_PER_COMBINE_STEP = 2


def _params(n_axes):
    return pltpu.CompilerParams(
        dimension_semantics=("arbitrary",) * n_axes, vmem_limit_bytes=VMEM_LIMIT_V7X)


def _const_spec(shape):
    nd = len(shape)
    return pl.BlockSpec(shape, lambda *_: (0,) * nd, pipeline_mode=pl.Buffered(1))


def _half_tanh_half(x):
    return jnp.tanh(0.5 * x)


def _silu(x):
    h = 0.5 * x
    return h * jnp.tanh(h) + h


def _gelu_tanh(x):
    assert x.dtype == F32
    c = float(np.sqrt(2.0 / np.pi))
    return x * (0.5 * (1.0 + jnp.tanh(c * (x + 0.044715 * (x * x * x)))))


def _rms_rows(x, g):
    return x * lax.rsqrt(jnp.mean(x * x, axis=-1, keepdims=True) + EPS) * g


def _bdot(a, b):
    return jnp.dot(a, b, preferred_element_type=F32)


def _bdot_t(a, bt):
    return lax.dot_general(a, bt, (((1,), (1,)), ((), ())), preferred_element_type=F32)


def _mod_kernel(c_ref, w_ref, b_ref, o_ref):
    c = c_ref[...]
    s = _silu(c).astype(BF16)
    o_ref[...] = _bdot(s, w_ref[...].astype(BF16)) + b_ref[...]


def _modulation(c_rows, w_ada, b_ada, *, n_cols, bn=1024):
    d, n = w_ada.shape[0], n_cols
    return pl.pallas_call(
        _mod_kernel,
        grid=(n // bn,),
        in_specs=[
            _const_spec((MOD_ROWS, d)),
            pl.BlockSpec((d, bn), lambda j: (0, j)),
            pl.BlockSpec((1, bn), lambda j: (0, j)),
        ],
        out_specs=pl.BlockSpec((MOD_ROWS, bn), lambda j: (0, j)),
        out_shape=jax.ShapeDtypeStruct((MOD_ROWS, n), F32),
        compiler_params=_params(1),
        name="adaln_mod",
    )(c_rows, w_ada, b_ada)


def _preproj_kernel(xc_ref, xl_ref, g_ref, sc_ref, sh_ref, ws_ref, gqa_ref, gkv_ref, cos_ref, sin_ref,
                    wq_ref, wsw_ref, wlo_ref, whi_ref, h_ref, ckv_ref, kr_ref, q_ref, wcast_ref,
                    new_ckv_ref, new_kr_ref, *, n_ctx_tiles, q_lora, kv_lora, scale):
    i = pl.program_id(0)

    rb = wlo_ref.shape[0]
    lo = wlo_ref[...].astype(BF16)
    wcast_ref[0:rb, :] = jnp.where(i == 0, jnp.zeros_like(lo), lo)
    wcast_ref[rb:2 * rb, :] = whi_ref[...].astype(BF16)

    def body(x_ref, rotary):
        h = _rms_rows(x_ref[...], g_ref[...]) * (1.0 + sc_ref[...]) + sh_ref[...]
        hb = h.astype(BF16)
        h_ref[...] = hb
        small = _bdot_t(hb, ws_ref[...])
        qa = _rms_rows(small[:, :q_lora], gqa_ref[...]).astype(BF16)
        ckv = _rms_rows(small[:, q_lora:q_lora + kv_lora], gkv_ref[...])
        ckv_ref[...] = ckv
        o = q_lora + kv_lora
        kr = small[:, o:o + 128]
        if not rotary:
            kr_ref[...] = kr
            new_ckv_ref[...] = ckv
            new_kr_ref[...] = kr[:, :D_ROPE]
            for hd in range(N_HEADS):
                q = _bdot(qa, wq_ref[:, hd * HEAD_SLOT:(hd + 1) * HEAD_SLOT])
                q_ref[hd] = (q * scale).astype(BF16)
            return
        cos = cos_ref[...]
        sin = sin_ref[...]
        lane = lax.broadcasted_iota(jnp.int32, kr.shape, 1)
        first_half = (lane & (2 * ROPE_FREQS - 1)) < ROPE_FREQS
        kr_sw = jnp.where(first_half, pltpu.roll(kr, 128 - ROPE_FREQS, 1), pltpu.roll(kr, ROPE_FREQS, 1))
        kr_ref[...] = kr * cos + kr_sw * sin
        for hp in range(N_HEADS // 2):
            q_sw = _bdot(qa, wsw_ref[:, hp * 256:(hp + 1) * 256])
            for s in range(2):
                hd = 2 * hp + s
                q = _bdot(qa, wq_ref[:, hd * HEAD_SLOT:(hd + 1) * HEAD_SLOT])
                q_ref[hd, :, 0:128] = (q[:, 0:128] * scale).astype(BF16)
                rot = q[:, 128:256] * cos + q_sw[:, s * 128:(s + 1) * 128] * sin
                q_ref[hd, :, 128:256] = (rot * scale).astype(BF16)

    @pl.when(i < n_ctx_tiles)
    def _():
        body(xc_ref, False)

    @pl.when(i >= n_ctx_tiles)
    def _():
        body(xl_ref, True)


def _pre_proj(xc, xl, g_attn, mod6, w_small_t, g_qa, g_kv, cos_t, sin_t, wq_p, wq_sw, w_in_t, *, front, tm):
    n_ctx, d = xc.shape
    n_lat_seq = cos_t.shape[0]
    t = n_ctx + xl.shape[0]
    q_lora, kv_lora = g_qa.shape[1], g_kv.shape[1]
    n_ctx_tiles = n_ctx // tm
    tiles_per_seq = n_lat_seq // tm
    n_wblk = w_in_t.shape[0] // front
    assert w_in_t.shape[0] == n_wblk * front and front % 16 == 0
    assert n_wblk + 1 == 2 * (t // tm), "weight row blocks must pair up with the token steps"

    def mod_row(i):
        return jnp.where(i < n_ctx_tiles, MOD_ROWS // 2, (i - n_ctx_tiles) // tiles_per_seq)

    def rope_blk(i):
        return jnp.maximum(i - n_ctx_tiles, 0) % tiles_per_seq

    scale = np.float32((D_NOPE + D_ROPE) ** -0.5)
    kern = functools.partial(_preproj_kernel, n_ctx_tiles=n_ctx_tiles, q_lora=q_lora, kv_lora=kv_lora,
                             scale=scale)
    return pl.pallas_call(
        kern,
        grid=(t // tm,),
        in_specs=[
            pl.BlockSpec((tm, d), lambda i: (jnp.minimum(i, n_ctx_tiles - 1), 0)),
            pl.BlockSpec((tm, d), lambda i: (jnp.maximum(i - n_ctx_tiles, 0), 0)),
            _const_spec((1, d)),
            pl.BlockSpec((None, None, 1, d), lambda i: (1, mod_row(i), 0, 0)),
            pl.BlockSpec((None, None, 1, d), lambda i: (0, mod_row(i), 0, 0)),
            _const_spec(w_small_t.shape),
            _const_spec((1, q_lora)),
            _const_spec((1, kv_lora)),
            pl.BlockSpec((tm, 128), lambda i: (rope_blk(i), 0)),
            pl.BlockSpec((tm, 128), lambda i: (rope_blk(i), 0)),
            _const_spec(wq_p.shape),
            _const_spec(wq_sw.shape),
            pl.BlockSpec((front, d), lambda i: (jnp.maximum(2 * i - 1, 0), 0)),
            pl.BlockSpec((front, d), lambda i: (2 * i, 0)),
        ],
        out_specs=[
            pl.BlockSpec((tm, d), lambda i: (i, 0)),
            pl.BlockSpec((tm, kv_lora), lambda i: (i, 0)),
            pl.BlockSpec((tm, 128), lambda i: (i, 0)),
            pl.BlockSpec((N_HEADS, tm, HEAD_SLOT), lambda i: (0, i, 0)),
            pl.BlockSpec((2 * front, d), lambda i: (i, 0)),
            pl.BlockSpec((tm, kv_lora), lambda i: (jnp.minimum(i, n_ctx_tiles - 1), 0)),
            pl.BlockSpec((tm, D_ROPE), lambda i: (jnp.minimum(i, n_ctx_tiles - 1), 0)),
        ],
        out_shape=[
            jax.ShapeDtypeStruct((t, d), BF16),
            jax.ShapeDtypeStruct((t, kv_lora), F32),
            jax.ShapeDtypeStruct((t, 128), F32),
            jax.ShapeDtypeStruct((N_HEADS, t, HEAD_SLOT), BF16),
            jax.ShapeDtypeStruct(((n_wblk + 1) * front, d), BF16),
            jax.ShapeDtypeStruct((n_ctx, kv_lora), F32),
            jax.ShapeDtypeStruct((n_ctx, D_ROPE), F32),
        ],
        compiler_params=_params(1),
        name="pre_proj",
    )(xc, xl, g_attn, mod6, mod6, w_small_t, g_qa, g_kv, cos_t, sin_t, wq_p, wq_sw, w_in_t, w_in_t)


def _wideproj_kernel(h_ref, wa_ref, wb_ref, o_ref, *, tm, bn, n_gelu_steps):
    j = pl.program_id(0)
    h = h_ref[...]

    def run(act):
        for c, w_ref in enumerate((wa_ref, wb_ref)):
            y = _bdot_t(h, w_ref[...])
            for r0 in range(0, tm, 256):
                o_ref[r0:r0 + 256, c * bn:(c + 1) * bn] = act(y[r0:r0 + 256, :]).astype(BF16)

    @pl.when(j < n_gelu_steps)
    def _():
        run(_gelu_tanh)

    @pl.when(j >= n_gelu_steps)
    def _():
        run(lambda y: y)


def _wide_proj(h, w_all_t, *, n_big, n_gelu, tm, bn):
    t, d = h.shape
    blk0 = (w_all_t.shape[0] - n_big) // bn
    return pl.pallas_call(
        functools.partial(_wideproj_kernel, tm=tm, bn=bn, n_gelu_steps=n_gelu // (2 * bn)),
        grid=(n_big // (2 * bn), t // tm),
        in_specs=[
            pl.BlockSpec((tm, d), lambda j, i: (i, 0)),
            pl.BlockSpec((bn, d), lambda j, i: (blk0 + 2 * j, 0)),
            pl.BlockSpec((bn, d), lambda j, i: (blk0 + 2 * j + 1, 0)),
        ],
        out_specs=pl.BlockSpec((tm, 2 * bn), lambda j, i: (i, j)),
        out_shape=jax.ShapeDtypeStruct((t, n_big), BF16),
        compiler_params=_params(2),
        name="wide_proj",
    )(h, w_all_t, w_all_t)


def _attn_kernel(*refs, n_own, n_cache, heads_per_iter, n_side):
    refs = list(refs)
    q_ref, ckv_ref, kr_ref = refs[:3]
    del refs[:3]
    if n_cache:
        cckv_ref, ckr_ref = refs[:2]
        del refs[:2]
    wuk_ref, wuv_ref = refs[:2]
    del refs[:2]
    if n_side:
        c_ref, wada_ref, bada_ref = refs[:3]
        side_in = refs[3:3 + n_side]
        del refs[:3 + n_side]
    o_ref = refs.pop(0)
    if n_side:
        mod_ref = refs.pop(0)
        side_out = refs[:n_side]
        del refs[:n_side]
        mod_ref[...] = _bdot(_silu(c_ref[...]).astype(BF16), wada_ref[...].astype(BF16)) + bada_ref[...]
        for src, dst in zip(side_in, side_out):
            dst[...] = src[...].astype(BF16)
    kpad, vexp, kall, krp, o_scr = refs
    qi = pl.program_id(1)

    @pl.when(qi == 0)
    def _():
        kall[0:n_own, :] = ckv_ref[...].astype(BF16)
        krp[0:n_own, :] = kr_ref[...].astype(BF16)
        if n_cache:
            kall[n_own:n_own + n_cache, :] = cckv_ref[...].astype(BF16)
            krp[n_own:n_own + n_cache, 0:D_ROPE] = ckr_ref[...].astype(BF16)
            krp[n_own:n_own + n_cache, D_ROPE:128] = jnp.zeros((n_cache, 128 - D_ROPE), BF16)

        def expand(hp, carry):
            kn = _bdot(kall[...], wuk_ref[hp]).astype(BF16)
            vv = _bdot(kall[...], wuv_ref[hp]).astype(BF16)
            for s in range(2):
                kpad[2 * hp + s, :, 0:128] = kn[:, s * 128:(s + 1) * 128]
                kpad[2 * hp + s, :, 128:256] = krp[...]
                vexp[2 * hp + s] = vv[:, s * 128:(s + 1) * 128]
            return carry

        if heads_per_iter == N_HEADS:
            for hp in range(N_HEADS // 2):
                expand(hp, 0)
        else:
            lax.fori_loop(0, N_HEADS // 2, expand, 0)

    def one_head(h):
        s = _bdot_t(q_ref[h], kpad[h])
        p = jnp.exp(s - jnp.max(s, axis=-1, keepdims=True))
        l = jnp.sum(p, axis=-1, keepdims=True)
        o = _bdot(p.astype(BF16), vexp[h])
        return (o / l).astype(BF16)

    if heads_per_iter == N_HEADS:
        for h in range(N_HEADS):
            o_ref[:, h * D_V:(h + 1) * D_V] = one_head(h)
    else:
        def head_group(hg, carry):
            for g in range(heads_per_iter):
                h = hg * heads_per_iter + g
                o_scr[h] = one_head(h)
            return carry

        lax.fori_loop(0, N_HEADS // heads_per_iter, head_group, 0)
        for h in range(N_HEADS):
            o_ref[:, h * D_V:(h + 1) * D_V] = o_scr[h]


def _attention(q_pad, ckv, kr, w_uk, w_uv, cache_ckv, cache_kr, *, row0, n_req, n_own, tq, heads_per_iter,
               side=None):
    kv_lora = ckv.shape[1]
    n_cache = 0 if cache_ckv is None else cache_ckv.shape[1]
    kn = n_own + n_cache
    qb = n_own // tq
    in_specs = [
        pl.BlockSpec((N_HEADS, tq, HEAD_SLOT), lambda b, qi: (0, row0 // tq + b * qb + qi, 0)),
        pl.BlockSpec((n_own, kv_lora), lambda b, qi: (row0 // n_own + b, 0)),
        pl.BlockSpec((n_own, 128), lambda b, qi: (row0 // n_own + b, 0)),
    ]
    args = [q_pad, ckv, kr]
    if n_cache:
        in_specs += [
            pl.BlockSpec((None, n_cache, kv_lora), lambda b, qi: (b, 0, 0)),
            pl.BlockSpec((None, n_cache, D_ROPE), lambda b, qi: (b, 0, 0)),
        ]
        args += [cache_ckv, cache_kr]
    in_specs += [_const_spec(w_uk.shape), _const_spec(w_uv.shape)]
    args += [w_uk, w_uv]
    out_specs = [pl.BlockSpec((tq, N_HEADS * D_V), lambda b, qi: (b * qb + qi, 0))]
    out_shape = [jax.ShapeDtypeStruct((n_req * n_own, N_HEADS * D_V), BF16)]
    n_side = 0
    if side is not None:
        assert qb == 1
        c_rows, w_ada, b_ada, col0, weights = side
        d_model, n_mod = w_ada.shape
        mcols = (n_mod - col0) // n_req
        assert mcols * n_req == n_mod - col0 and mcols % 128 == 0 and col0 % mcols == 0
        in_specs += [
            _const_spec(c_rows.shape),
            pl.BlockSpec((d_model, mcols), lambda b, qi: (0, col0 // mcols + b)),
            pl.BlockSpec((1, mcols), lambda b, qi: (0, col0 // mcols + b)),
        ]
        args += [c_rows, w_ada, b_ada]
        out_specs.append(pl.BlockSpec((MOD_ROWS, mcols), lambda b, qi: (0, b)))
        out_shape.append(jax.ShapeDtypeStruct((MOD_ROWS, n_mod - col0), F32))
        n_side = len(weights)
        assert n_side > 0
        for w in weights:
            rows = w.shape[0] // n_req
            assert rows * n_req == w.shape[0] and rows % 16 == 0
            in_specs.append(pl.BlockSpec((rows, w.shape[1]), lambda b, qi: (b, 0)))
            args.append(w)
            out_specs.append(pl.BlockSpec((rows, w.shape[1]), lambda b, qi: (b, 0)))
            out_shape.append(jax.ShapeDtypeStruct(w.shape, BF16))
    res = pl.pallas_call(
        functools.partial(_attn_kernel, n_own=n_own, n_cache=n_cache, heads_per_iter=heads_per_iter,
                          n_side=n_side),
        grid=(n_req, qb),
        in_specs=in_specs,
        out_specs=out_specs,
        out_shape=out_shape,
        scratch_shapes=[
            pltpu.VMEM((N_HEADS, kn, HEAD_SLOT), BF16),
            pltpu.VMEM((N_HEADS, kn, D_V), BF16),
            pltpu.VMEM((kn, kv_lora), BF16),
            pltpu.VMEM((kn, 128), BF16),
            pltpu.VMEM((N_HEADS, tq, D_V), BF16),
        ],
        compiler_params=_params(2),
        name="mla_attn_cache" if n_cache else "mla_attn",
    )(*args)
    if side is None:
        return res[0]
    return res[0], res[1], res[2:]


def _oproj_kernel(xc_ref, xl_ref, w_ref, o_ref, *, n_ctx_tiles):
    i = pl.program_id(0)

    @pl.when(i < n_ctx_tiles)
    def _():
        o_ref[...] = _bdot(xc_ref[...], w_ref[...]).astype(o_ref.dtype)

    @pl.when(i >= n_ctx_tiles)
    def _():
        o_ref[...] = _bdot(xl_ref[...], w_ref[...]).astype(o_ref.dtype)


def _o_proj(xc, xl, w, *, tm):
    n_ctx, k = xc.shape
    t = n_ctx + xl.shape[0]
    n = w.shape[1]
    n_ctx_tiles = n_ctx // tm
    return pl.pallas_call(
        functools.partial(_oproj_kernel, n_ctx_tiles=n_ctx_tiles),
        grid=(t // tm,),
        in_specs=[
            pl.BlockSpec((tm, k), lambda i: (jnp.minimum(i, n_ctx_tiles - 1), 0)),
            pl.BlockSpec((tm, k), lambda i: (jnp.maximum(i - n_ctx_tiles, 0), 0)),
            _const_spec(w.shape),
        ],
        out_specs=pl.BlockSpec((tm, n), lambda i: (i, 0)),
        out_shape=jax.ShapeDtypeStruct((t, n), BF16),
        compiler_params=_params(1),
        name="o_proj",
    )(xc, xl, w)


def _sgu_merge_kernel(gu_ref, gv_ref, ga_ref, gb_ref, oa_ref, g_ref, ws_ref, bs_ref, wo_ref, m_ref, z_scr,
                      *, tm, gc):
    s = pl.program_id(0)

    @pl.when(s == 0)
    def _():
        z_scr[1] = jnp.zeros(z_scr.shape[1:], BF16)

    o_b = _bdot(z_scr[(s + 1) % 2], wo_ref[...])
    for r0 in range(0, tm, CHUNK):
        rows = slice(r0, r0 + CHUNK)
        oa = oa_ref[rows, :]
        ob = o_b[rows, :].astype(BF16)
        ta = _half_tanh_half(ga_ref[rows, :])
        tb = _half_tanh_half(gb_ref[rows, :])
        m_ref[rows, :] = 0.5 * ((oa + ob) + (ta * oa + tb * ob))

    slot = s % 2
    g_bf = g_ref[...].astype(BF16)
    for r0 in range(0, tm, CHUNK):
        v = gv_ref[r0:r0 + CHUNK, :]
        v32 = v.astype(F32)
        inv = lax.rsqrt(jnp.mean(v32 * v32, axis=-1, keepdims=True) + EPS)
        vn = v * inv.astype(BF16) * g_bf
        for g in range(GM_GROUPS):
            c0 = g * gc
            mix = _bdot(ws_ref[g], vn[:, c0:c0 + gc]) + bs_ref[:, g:g + 1]
            u = gu_ref[r0:r0 + CHUNK, c0:c0 + gc]
            z_scr[slot, r0:r0 + CHUNK, c0:c0 + gc] = u * mix.astype(BF16)


def _sgu_merge(big, o_a, g_sgu, w_s, b_s_t, w_o_gm, *, tm):
    t = big.shape[0]
    width = g_sgu.shape[1]
    d = w_o_gm.shape[1]
    gc = width // GM_GROUPS
    nt = t // tm

    def fill(s):
        return jnp.minimum(s, nt - 1)

    def drain(s):
        return jnp.maximum(s - 1, 0)

    return pl.pallas_call(
        functools.partial(_sgu_merge_kernel, tm=tm, gc=gc),
        grid=(nt + 1,),
        in_specs=[
            pl.BlockSpec((tm, width), lambda s: (fill(s), 0)),
            pl.BlockSpec((tm, width), lambda s: (fill(s), 1)),
            pl.BlockSpec((tm, d), lambda s: (drain(s), 2)),
            pl.BlockSpec((tm, d), lambda s: (drain(s), 3)),
            pl.BlockSpec((tm, d), lambda s: (drain(s), 0)),
            _const_spec((1, width)),
            _const_spec(w_s.shape),
            _const_spec(b_s_t.shape),
            _const_spec(w_o_gm.shape),
        ],
        out_specs=pl.BlockSpec((tm, d), lambda s: (drain(s), 0)),
        out_shape=jax.ShapeDtypeStruct((t, d), BF16),
        scratch_shapes=[pltpu.VMEM((2, tm, width), BF16)],
        compiler_params=_params(1),
        name="sgu_merge",
    )(big, big, big, big, o_a, g_sgu, w_s, b_s_t, w_o_gm)


def _outproj_kernel(m_ref, xc_ref, xl_ref, gate_ref, sc_ref, sh_ref, g_ref, wo_ref, wr_ref,
                    x1_ref, h2_ref, lg_ref, *, tm, n_ctx_tiles):
    i = pl.program_id(0)

    def body(x_ref):
        half = tm // 2
        for b0 in range(0, tm, half):
            r = _bdot(m_ref[b0:b0 + half, :], wo_ref[...])
            for r0 in range(0, half, CHUNK):
                rows = slice(b0 + r0, b0 + r0 + CHUNK)
                x1 = x_ref[rows, :] + gate_ref[...] * r[r0:r0 + CHUNK, :]
                x1_ref[rows, :] = x1
                h2 = (_rms_rows(x1, g_ref[...]) * (1.0 + sc_ref[...]) + sh_ref[...]).astype(BF16)
                h2_ref[rows, :] = h2
                lg_ref[rows, :] = _bdot(h2, wr_ref[...])

    @pl.when(i < n_ctx_tiles)
    def _():
        body(xc_ref)

    @pl.when(i >= n_ctx_tiles)
    def _():
        body(xl_ref)


def _out_proj(merged, xc, xl, mod6, g_ffn, w_out, w_router_p, *, n_lat_seq, tm):
    n_ctx, d = xc.shape
    t = n_ctx + xl.shape[0]
    n_ctx_tiles = n_ctx // tm
    tiles_per_seq = n_lat_seq // tm

    def mod_row(i):
        return jnp.where(i < n_ctx_tiles, MOD_ROWS // 2, (i - n_ctx_tiles) // tiles_per_seq)

    return pl.pallas_call(
        functools.partial(_outproj_kernel, tm=tm, n_ctx_tiles=n_ctx_tiles),
        grid=(t // tm,),
        in_specs=[
            pl.BlockSpec((tm, d), lambda i: (i, 0)),
            pl.BlockSpec((tm, d), lambda i: (jnp.minimum(i, n_ctx_tiles - 1), 0)),
            pl.BlockSpec((tm, d), lambda i: (jnp.maximum(i - n_ctx_tiles, 0), 0)),
            pl.BlockSpec((None, None, 1, d), lambda i: (2, mod_row(i), 0, 0)),
            pl.BlockSpec((None, None, 1, d), lambda i: (4, mod_row(i), 0, 0)),
            pl.BlockSpec((None, None, 1, d), lambda i: (3, mod_row(i), 0, 0)),
            _const_spec((1, d)),
            _const_spec(w_out.shape),
            _const_spec(w_router_p.shape),
        ],
        out_specs=[
            pl.BlockSpec((tm, d), lambda i: (i, 0)),
            pl.BlockSpec((tm, d), lambda i: (i, 0)),
            pl.BlockSpec((tm, LOGIT_LANES), lambda i: (i, 0)),
        ],
        out_shape=[
            jax.ShapeDtypeStruct((t, d), F32),
            jax.ShapeDtypeStruct((t, d), BF16),
            jax.ShapeDtypeStruct((t, LOGIT_LANES), F32),
        ],
        compiler_params=_params(1),
        name="out_proj",
    )(merged, xc, xl, mod6, mod6, mod6, g_ffn, w_out, w_router_p)


BISECT_STEPS = 48
MIN_NORMAL_F32 = float(np.finfo(np.float32).tiny)


def _route_kernel(lg_ref, key_ref, aff_ref, tri_scr, *, n_sets, n, cap):
    for r0 in range(0, n, 128):
        r = lax.broadcasted_iota(jnp.int32, (128, n), 0) + r0
        c = lax.broadcasted_iota(jnp.int32, (128, n), 1)
        tri_scr[r0:r0 + 128, :] = jnp.where(r < c, 1.0, 0.0).astype(BF16)

    for s in range(n_sets):
        logits = lg_ref[s * n:(s + 1) * n, :].T[0:N_EXPERTS, :]
        e = jnp.exp(logits - jnp.max(logits, axis=0, keepdims=True))
        aff_ref[s * N_EXPERTS:(s + 1) * N_EXPERTS, :] = e / jnp.sum(e, axis=0, keepdims=True)
    aff = aff_ref[...]
    rows = n_sets * N_EXPERTS

    def count_ge(thr):
        return jnp.sum(jnp.where(aff >= thr, 1.0, 0.0), axis=1, keepdims=True)

    def bisect(_, carry):
        lo, hi = carry
        mid = jnp.sqrt(lo) * jnp.sqrt(hi)
        ok = count_ge(mid) >= cap
        return jnp.where(ok, mid, lo), jnp.where(ok, hi, mid)

    lo0 = jnp.full((rows, 1), MIN_NORMAL_F32, F32)
    hi0 = jnp.full((rows, 1), 2.0, F32)
    lo, hi = lax.fori_loop(0, BISECT_STEPS, bisect, (lo0, hi0))
    lo = jnp.where(count_ge(lo) >= cap, lo, 0.0)

    above = aff >= hi
    band = (aff >= lo) & jnp.logical_not(above)
    need = cap - jnp.sum(jnp.where(above, 1.0, 0.0), axis=1, keepdims=True)
    tri = tri_scr[...]
    band_before = _bdot(jnp.where(band, 1.0, 0.0).astype(BF16), tri)
    sel = above | (band & (band_before < need))
    pos = _bdot(jnp.where(sel, 1.0, 0.0).astype(BF16), tri)
    key_ref[...] = jnp.where(sel, pos, -1.0)


def _route(logits, *, row0, n_sets, n):
    cap = EC_FACTOR * n // N_EXPERTS
    rows = n_sets * N_EXPERTS
    blk = row0 // (n_sets * n)
    return pl.pallas_call(
        functools.partial(_route_kernel, n_sets=n_sets, n=n, cap=cap),
        grid=(1,),
        in_specs=[pl.BlockSpec((n_sets * n, LOGIT_LANES), lambda g: (blk, 0))],
        out_specs=[pl.BlockSpec((rows, n), lambda g: (0, 0)), pl.BlockSpec((rows, n), lambda g: (0, 0))],
        out_shape=[jax.ShapeDtypeStruct((rows, n), F32), jax.ShapeDtypeStruct((rows, n), F32)],
        scratch_shapes=[pltpu.VMEM((n, n), BF16)],
        compiler_params=_params(1),
        name=f"route_{n}",
    )(logits)


def _gather_kernel(key_ref, aff_ref, h2_ref, pt_ref, gate_ref, xg_ref, p_scr, *, sps, n, cap, d, nb):
    slot = lax.broadcasted_iota(jnp.int32, (cap, n), 0).astype(F32)
    per_group = 128 // cap
    for s in range(sps):
        key = key_ref[s * N_EXPERTS:(s + 1) * N_EXPERTS, :]
        aff = aff_ref[s * N_EXPERTS:(s + 1) * N_EXPERTS, :]
        for grp in range(N_EXPERTS // per_group):
            pieces = []
            for ex in range(grp * per_group, (grp + 1) * per_group):
                hit = slot == key[ex:ex + 1, :]
                gate_ref[ex, s] = jnp.sum(jnp.where(hit, aff[ex:ex + 1, :], 0.0), axis=1, keepdims=True)
                pieces.append(jnp.where(hit, 1.0, 0.0))
            hit128 = pieces[0] if per_group == 1 else jnp.concatenate(pieces, axis=0)
            p_scr[s, grp * 128:(grp + 1) * 128, :] = hit128.astype(BF16)
            pt_ref[s, :, grp * 128:(grp + 1) * 128] = hit128.T.astype(BF16)

        p = p_scr[s]
        for c in range(d // nb):
            xg = _bdot(p, h2_ref[s * n:(s + 1) * n, c * nb:(c + 1) * nb]).astype(BF16)
            for ex in range(N_EXPERTS):
                xg_ref[ex, s, :, c * nb:(c + 1) * nb] = xg[ex * cap:(ex + 1) * cap, :]


def _dispatch_gather(key, aff, h2, *, row0, n_sets, n, sps):
    d = h2.shape[1]
    cap = EC_FACTOR * n // N_EXPERTS
    slots = N_EXPERTS * cap
    blk0 = row0 // (sps * n)
    return pl.pallas_call(
        functools.partial(_gather_kernel, sps=sps, n=n, cap=cap, d=d, nb=256),
        grid=(n_sets // sps,),
        in_specs=[
            pl.BlockSpec((sps * N_EXPERTS, n), lambda b: (b, 0)),
            pl.BlockSpec((sps * N_EXPERTS, n), lambda b: (b, 0)),
            pl.BlockSpec((sps * n, d), lambda b: (blk0 + b, 0)),
        ],
        out_specs=[
            pl.BlockSpec((sps, n, slots), lambda b: (b, 0, 0)),
            pl.BlockSpec((N_EXPERTS, sps, cap, 1), lambda b: (0, b, 0, 0)),
            pl.BlockSpec((N_EXPERTS, sps, cap, d), lambda b: (0, b, 0, 0)),
        ],
        out_shape=[
            jax.ShapeDtypeStruct((n_sets, n, slots), BF16),
            jax.ShapeDtypeStruct((N_EXPERTS, n_sets, cap, 1), F32),
            jax.ShapeDtypeStruct((N_EXPERTS, n_sets, cap, d), BF16),
        ],
        scratch_shapes=[pltpu.VMEM((sps, slots, n), BF16)],
        compiler_params=_params(1),
        name=f"dispatch_gather_{n}",
    )(key, aff, h2)


def _expert_kernel(xc_ref, xl_ref, gc_ref, gl_ref, w1_ref, w3_ref, w2_ref, yc_ref, yl_ref, hc_scr, hl_scr,
                   *, n_f, fc, rows_c, rows_l, d):
    k = pl.program_id(1)

    @pl.when(k < n_f)
    def _():
        w1 = w1_ref[...].astype(BF16)
        w3 = w3_ref[...].astype(BF16)
        for x_ref, h_scr, rows in ((xc_ref, hc_scr, rows_c), (xl_ref, hl_scr, rows_l)):
            x = x_ref[...].reshape(rows, d)
            a = _bdot(x, w1)
            g = _bdot(x, w3)
            h_scr[k] = (_silu(a) * g).astype(BF16)

    @pl.when(k >= n_f)
    def _():
        w2 = w2_ref[...].astype(BF16)
        for h_scr, g_ref, y_ref, rows in ((hc_scr, gc_ref, yc_ref, rows_c), (hl_scr, gl_ref, yl_ref, rows_l)):
            y = _bdot(h_scr[0], w2[0:fc, :])
            for kk in range(1, n_f):
                y = y + _bdot(h_scr[kk], w2[kk * fc:(kk + 1) * fc, :])
            y_ref[...] = (y * g_ref[...].reshape(rows, 1)).astype(BF16).reshape(y_ref.shape)


def _experts(xg_c, xg_l, gate_c, gate_l, w_e1, w_e3, w_e2, *, fc=512, nc=512):
    n_e, sets_c, cap_c, d = xg_c.shape
    _, sets_l, cap_l, _ = xg_l.shape
    d_ff = w_e1.shape[2]
    n_f = d_ff // fc
    n_c = d // nc
    rows_c, rows_l = sets_c * cap_c, sets_l * cap_l

    def e_in(e, k):
        return jnp.minimum(e + (k >= n_f).astype(jnp.int32), n_e - 1)

    def f_idx(k):
        return jnp.where(k < n_f, k, 0)

    def c_idx(k):
        return jnp.maximum(k - n_f, 0)

    kern = functools.partial(_expert_kernel, n_f=n_f, fc=fc, rows_c=rows_c, rows_l=rows_l, d=d)
    return pl.pallas_call(
        kern,
        grid=(n_e, n_f + n_c),
        in_specs=[
            pl.BlockSpec((None, sets_c, cap_c, d), lambda e, k: (e_in(e, k), 0, 0, 0)),
            pl.BlockSpec((None, sets_l, cap_l, d), lambda e, k: (e_in(e, k), 0, 0, 0)),
            pl.BlockSpec((None, sets_c, cap_c, 1), lambda e, k: (e, 0, 0, 0)),
            pl.BlockSpec((None, sets_l, cap_l, 1), lambda e, k: (e, 0, 0, 0)),
            pl.BlockSpec((None, d, fc), lambda e, k: (e_in(e, k), 0, f_idx(k))),
            pl.BlockSpec((None, d, fc), lambda e, k: (e_in(e, k), 0, f_idx(k))),
            pl.BlockSpec((None, d_ff, nc), lambda e, k: (e, 0, c_idx(k))),
        ],
        out_specs=[
            pl.BlockSpec((None, sets_c, cap_c, nc), lambda e, k: (e, 0, 0, c_idx(k))),
            pl.BlockSpec((None, sets_l, cap_l, nc), lambda e, k: (e, 0, 0, c_idx(k))),
        ],
        out_shape=[
            jax.ShapeDtypeStruct(xg_c.shape, BF16),
            jax.ShapeDtypeStruct(xg_l.shape, BF16),
        ],
        scratch_shapes=[pltpu.VMEM((n_f, rows_c, fc), BF16), pltpu.VMEM((n_f, rows_l, fc), BF16)],
        compiler_params=_params(2),
        name="experts",
    )(xg_c, xg_l, gate_c, gate_l, w_e1, w_e3, w_e2)


def _combine_kernel(pt_ref, y_ref, x1_ref, gate_ref, g_ref, o_ref, acc_scr, *, sps, tn, slots, d, nb):
    for s in range(sps):
        pt = pt_ref[s]
        rows = slice(s * tn, (s + 1) * tn)
        for c in range(d // nb):
            cols = slice(c * nb, (c + 1) * nb)
            moe = _bdot(pt, y_ref[:, s, :, cols].reshape(slots, nb))
            acc_scr[:, cols] = x1_ref[rows, cols] + gate_ref[:, cols] * moe
        o_ref[rows, :] = _rms_rows(acc_scr[...], g_ref[...])


def _combine(pt, y, x1, mod6, g_final, *, row0, n_sets, n, sps, mod_row_fn):
    d = x1.shape[1]
    cap = EC_FACTOR * n // N_EXPERTS
    slots = N_EXPERTS * cap
    tn = min(n, 512)
    nt = n // tn
    assert sps == 1 or nt == 1
    blk0 = row0 // (sps * tn)
    return pl.pallas_call(
        functools.partial(_combine_kernel, sps=sps, tn=tn, slots=slots, d=d, nb=512),
        grid=(n_sets // sps, nt),
        in_specs=[
            pl.BlockSpec((sps, tn, slots), lambda b, r: (b, r, 0)),
            pl.BlockSpec((N_EXPERTS, sps, cap, d), lambda b, r: (0, b, 0, 0)),
            pl.BlockSpec((sps * tn, d), lambda b, r: (blk0 + b * nt + r, 0)),
            pl.BlockSpec((None, None, 1, d), lambda b, r: (5, mod_row_fn(b), 0, 0)),
            _const_spec((1, d)),
        ],
        out_specs=pl.BlockSpec((sps * tn, d), lambda b, r: (b * nt + r, 0)),
        out_shape=jax.ShapeDtypeStruct((n_sets * n, d), F32),
        scratch_shapes=[pltpu.VMEM((tn, d), F32)],
        compiler_params=_params(2),
        name=f"combine_{n}",
    )(pt, y, x1, mod6, g_final)


def _rope_tables(n):
    tpos = jnp.arange(n, dtype=jnp.int32)
    row = (tpos // GRID_W).astype(F32)
    col = (tpos % GRID_W).astype(F32)
    inv = 1.0 / (ROPE_BASE ** (jnp.arange(ROPE_FREQS, dtype=F32) / ROPE_FREQS))
    ang = jnp.stack([row[:, None] * inv, col[:, None] * inv], axis=1)
    cos = jnp.cos(ang)[:, :, None, :]
    sin = jnp.sin(ang)[:, :, None, :]
    cos = jnp.broadcast_to(cos, (n, 2, 2, ROPE_FREQS)).reshape(n, D_ROPE)
    sin = jnp.concatenate([-sin, sin], axis=2).reshape(n, D_ROPE)
    pad = jnp.zeros((n, 128 - D_ROPE), F32)
    return jnp.concatenate([cos, pad], axis=1), jnp.concatenate([sin, pad], axis=1)


def _swap_rotary_halves(w):
    perm = np.arange(D_ROPE) ^ ROPE_FREQS
    return w[..., perm]


def kernel(x_prompt, x_sample, c, cache_ckv, cache_krope, c_ctx, g_attn, g_ffn, w_ada, b_ada, w_in, g_qa,
           w_qb, g_kv, w_uk, w_uv, w_o_mla, g_sgu, w_s, b_s, w_o_gm, w_out, w_router, w_e1, w_e3, w_e2,
           g_final):
    batch, seq, d = x_prompt.shape
    dec_batch, dec_seq, _ = x_sample.shape
    depth = g_attn.shape[0]
    assert depth == 1
    q_lora, kv_lora = g_qa.shape[1], g_kv.shape[1]
    gm_width = g_sgu.shape[1]
    n_ctx, n_lat = batch * seq, dec_batch * dec_seq
    assert dec_batch < MOD_ROWS // 2 + 1

    xc = x_prompt.reshape(n_ctx, d)
    xl = x_sample.reshape(n_lat, d)

    c_rows = jnp.zeros((MOD_ROWS, d), F32).at[:dec_batch].set(c).at[MOD_ROWS // 2].set(c_ctx)
    n_mod_early = 2 * d
    mod_a = _modulation(c_rows, w_ada[0], b_ada[0][None, :], n_cols=n_mod_early)
    mod6_a = mod_a.reshape(MOD_ROWS, 2, 1, d).transpose(1, 0, 2, 3)

    cos_t, sin_t = _rope_tables(dec_seq)

    w_in_t = w_in[0].T
    o_kr = q_lora + kv_lora
    w_small_t = jnp.pad(w_in_t[:o_kr + D_ROPE], ((0, 128 - D_ROPE), (0, 0))).astype(BF16)
    tm = ROW_TILE
    front = -(o_kr + D_ROPE) % WIDE_WEIGHT_BLOCK

    wq3 = w_qb[0].reshape(q_lora, N_HEADS, D_NOPE + D_ROPE)
    wq_nope, wq_rope = wq3[:, :, :D_NOPE], wq3[:, :, D_NOPE:]
    zq = jnp.zeros((q_lora, N_HEADS, HEAD_SLOT - D_NOPE - D_ROPE), F32)
    wq_p = jnp.concatenate([wq_nope, wq_rope, zq], axis=2).reshape(q_lora, N_HEADS * HEAD_SLOT).astype(BF16)
    wq_sw = jnp.concatenate([_swap_rotary_halves(wq_rope), zq], axis=2).reshape(
        q_lora, N_HEADS * 128).astype(BF16)

    h, ckv, kr, q_pad, w_all_t, ckv_ctx, kr_ctx = _pre_proj(
        xc, xl, g_attn, mod6_a, w_small_t, g_qa, g_kv, cos_t, sin_t, wq_p, wq_sw, w_in_t, front=front, tm=tm)
    big = _wide_proj(h, w_all_t, n_big=w_in_t.shape[0] - o_kr - D_ROPE, n_gelu=2 * gm_width,
                     tm=MATMUL_ROW_TILE, bn=WIDE_WEIGHT_BLOCK)

    w_uk2 = w_uk[0].reshape(kv_lora, N_HEADS // 2, 2 * D_NOPE).transpose(1, 0, 2).astype(BF16)
    w_uv2 = w_uv[0].reshape(kv_lora, N_HEADS // 2, 2 * D_V).transpose(1, 0, 2).astype(BF16)
    side = (c_rows, w_ada[0], b_ada[0][None, :], n_mod_early, (w_o_mla[0], w_o_gm[0], w_out[0]))
    o_ctx, mod_b, (w_o_mla_b, w_o_gm_b, w_out_b) = _attention(
        q_pad, ckv, kr, w_uk2, w_uv2, None, None,
        row0=0, n_req=batch, n_own=seq, tq=seq, heads_per_iter=N_HEADS, side=side)
    o_lat = _attention(q_pad, ckv, kr, w_uk2, w_uv2, cache_ckv[:, 0], cache_krope[:, 0],
                       row0=n_ctx, n_req=dec_batch, n_own=dec_seq, tq=ATTN_Q_TILE,
                       heads_per_iter=ATTN_HEADS_PER_STEP)
    mod6 = jnp.concatenate([mod6_a, mod_b.reshape(MOD_ROWS, 4, 1, d).transpose(1, 0, 2, 3)], axis=0)

    o_a = _o_proj(o_ctx, o_lat, w_o_mla_b, tm=MATMUL_ROW_TILE)
    merged = _sgu_merge(big, o_a, g_sgu, w_s[0].astype(BF16), b_s[0].T, w_o_gm_b, tm=tm)

    w_router_p = jnp.concatenate(
        [w_router[0], jnp.zeros((d, LOGIT_LANES - N_EXPERTS), F32)], axis=1).astype(BF16)
    x1, h2, logits = _out_proj(merged, xc, xl, mod6, g_ffn, w_out_b, w_router_p,
                               n_lat_seq=dec_seq, tm=tm)

    key_c, aff_c = _route(logits, row0=0, n_sets=batch, n=seq)
    key_l, aff_l = _route(logits, row0=n_ctx, n_sets=dec_batch, n=dec_seq)
    p_c, gate_c, xg_c = _dispatch_gather(key_c, aff_c, h2, row0=0, n_sets=batch, n=seq,
                                         sps=CTX_SETS_PER_GATHER_STEP)
    p_l, gate_l, xg_l = _dispatch_gather(key_l, aff_l, h2, row0=n_ctx, n_sets=dec_batch, n=dec_seq, sps=1)

    y_c, y_l = _experts(xg_c, xg_l, gate_c, gate_l, w_e1[0], w_e3[0], w_e2[0])

    y_prompt = _combine(p_c, y_c, x1, mod6, g_final[None, :], row0=0, n_sets=batch, n=seq,
                        sps=CTX_SETS_PER_COMBINE_STEP, mod_row_fn=lambda b: MOD_ROWS // 2)
    y_sample = _combine(p_l, y_l, x1, mod6, g_final[None, :], row0=n_ctx, n_sets=dec_batch, n=dec_seq, sps=1,
                        mod_row_fn=lambda b: b)

    new_ckv = ckv_ctx.reshape(batch, 1, seq, kv_lora)
    new_krope = kr_ctx.reshape(batch, 1, seq, D_ROPE)
    return (y_prompt.reshape(batch, seq, d), y_sample.reshape(dec_batch, dec_seq, d), new_ckv, new_krope)
```

```python
import functools

import numpy as np
import jax
import jax.numpy as jnp
from jax import lax
from jax.experimental import pallas as pl
from jax.experimental.pallas import tpu as pltpu

F32 = jnp.float32
BF16 = jnp.bfloat16

LANES = 128
MXU_WIDTH = 256

N_HEADS = 16
D_NOPE = 128
D_ROPE = 64
D_V = 128
ROPE_SLOT = LANES
HEAD_SLOT = D_NOPE + ROPE_SLOT
ROPE_FREQS = D_ROPE // 4
ROPE_BASE = 10000.0
GRID_W = 64
CHUNK = 128
GM_GROUPS = 8
N_EXPERTS = 16
EC_FACTOR = 2
EPS = 1e-6
MOD_ROWS = 8
LOGIT_LANES = LANES

VMEM_LIMIT_V7X = 56 * 1024 * 1024

ROW_TILE = 512
MATMUL_ROW_TILE = 1024
WIDE_WEIGHT_BLOCK = 1024
ATTN_Q_TILE = 512
ATTN_HEADS_PER_STEP = 4
CTX_SETS_PER_GATHER_STEP = 4
CTX_SETS_PER_COMBINE_STEP = 2


def _params(n_axes):
    return pltpu.CompilerParams(
        dimension_semantics=("arbitrary",) * n_axes, vmem_limit_bytes=VMEM_LIMIT_V7X)


def _const_spec(shape):
    nd = len(shape)
    return pl.BlockSpec(shape, lambda *_: (0,) * nd, pipeline_mode=pl.Buffered(1))


def _half_tanh_half(x):
    return jnp.tanh(0.5 * x)


def _silu(x):
    h = 0.5 * x
    return h * jnp.tanh(h) + h


def _gelu_tanh(x):
    assert x.dtype == F32
    c = float(np.sqrt(2.0 / np.pi))
    return x * (0.5 * (1.0 + jnp.tanh(c * (x + 0.044715 * (x * x * x)))))


def _rms_rows(x, g):
    return x * lax.rsqrt(jnp.mean(x * x, axis=-1, keepdims=True) + EPS) * g


def _bdot(a, b):
    return jnp.dot(a, b, preferred_element_type=F32)


def _bdot_t(a, bt):
    return lax.dot_general(a, bt, (((1,), (1,)), ((), ())), preferred_element_type=F32)


def _mod_kernel(c_ref, w_ref, b_ref, o_ref):
    c = c_ref[...]
    s = _silu(c).astype(BF16)
    o_ref[...] = _bdot(s, w_ref[...].astype(BF16)) + b_ref[...]


def _modulation(c_rows, w_ada, b_ada, *, n_cols, bn=1024):
    d, n = w_ada.shape[0], n_cols
    return pl.pallas_call(
        _mod_kernel,
        grid=(n // bn,),
        in_specs=[
            _const_spec((MOD_ROWS, d)),
            pl.BlockSpec((d, bn), lambda j: (0, j)),
            pl.BlockSpec((1, bn), lambda j: (0, j)),
        ],
        out_specs=pl.BlockSpec((MOD_ROWS, bn), lambda j: (0, j)),
        out_shape=jax.ShapeDtypeStruct((MOD_ROWS, n), F32),
        compiler_params=_params(1),
        name="adaln_mod",
    )(c_rows, w_ada, b_ada)


def _preproj_kernel(xc_ref, xl_ref, g_ref, sc_ref, sh_ref, ws_ref, gqa_ref, gkv_ref, cos_ref, sin_ref,
                    wq_ref, wlo_ref, whi_ref, h_ref, ckv_ref, kr_ref, q_ref, wcast_ref,
                    new_ckv_ref, new_kr_ref, *, n_ctx_tiles, q_lora, kv_lora, scale):
    i = pl.program_id(0)

    rb = wlo_ref.shape[0]
    lo = wlo_ref[...].astype(BF16)
    wcast_ref[0:rb, :] = jnp.where(i == 0, jnp.zeros_like(lo), lo)
    wcast_ref[rb:2 * rb, :] = whi_ref[...].astype(BF16)

    def body(x_ref, rotary):
        h = _rms_rows(x_ref[...], g_ref[...]) * (1.0 + sc_ref[...]) + sh_ref[...]
        hb = h.astype(BF16)
        h_ref[...] = hb
        small = _bdot_t(hb, ws_ref[...])
        qa = _rms_rows(small[:, :q_lora], gqa_ref[...]).astype(BF16)
        ckv = _rms_rows(small[:, q_lora:q_lora + kv_lora], gkv_ref[...])
        ckv_ref[...] = ckv
        o = q_lora + kv_lora
        kr = small[:, o:o + ROPE_SLOT]
        if not rotary:
            kr_ref[...] = kr
            new_ckv_ref[...] = ckv
            new_kr_ref[...] = kr[:, :D_ROPE]
            for hd in range(N_HEADS):
                q = _bdot(qa, wq_ref[:, hd * HEAD_SLOT:(hd + 1) * HEAD_SLOT])
                q_ref[hd] = (q * scale).astype(BF16)
            return
        cos = cos_ref[...]
        sin = sin_ref[...]
        lane = lax.broadcasted_iota(jnp.int32, kr.shape, 1)
        first_half = (lane & (2 * ROPE_FREQS - 1)) < ROPE_FREQS

        def rotate(r):
            partner = jnp.where(first_half, pltpu.roll(r, ROPE_SLOT - ROPE_FREQS, 1), pltpu.roll(r, ROPE_FREQS, 1))
            return r * cos + partner * sin

        kr_ref[...] = rotate(kr)
        for hd in range(N_HEADS):
            q = _bdot(qa, wq_ref[:, hd * HEAD_SLOT:(hd + 1) * HEAD_SLOT])
            q_ref[hd, :, 0:D_NOPE] = (q[:, 0:D_NOPE] * scale).astype(BF16)
            q_ref[hd, :, D_NOPE:HEAD_SLOT] = (rotate(q[:, D_NOPE:HEAD_SLOT]) * scale).astype(BF16)

    @pl.when(i < n_ctx_tiles)
    def _():
        body(xc_ref, False)

    @pl.when(i >= n_ctx_tiles)
    def _():
        body(xl_ref, True)


def _pre_proj(xc, xl, g_attn, mod6, w_small_t, g_qa, g_kv, cos_t, sin_t, wq_p, w_in_t, *, front, tm):
    n_ctx, d = xc.shape
    n_lat_seq = cos_t.shape[0]
    t = n_ctx + xl.shape[0]
    q_lora, kv_lora = g_qa.shape[1], g_kv.shape[1]
    n_ctx_tiles = n_ctx // tm
    tiles_per_seq = n_lat_seq // tm
    n_wblk = w_in_t.shape[0] // front
    assert w_in_t.shape[0] == n_wblk * front and front % 16 == 0
    assert n_wblk + 1 == 2 * (t // tm), "weight row blocks must pair up with the token steps"

    def mod_row(i):
        return jnp.where(i < n_ctx_tiles, MOD_ROWS // 2, (i - n_ctx_tiles) // tiles_per_seq)

    def rope_blk(i):
        return jnp.maximum(i - n_ctx_tiles, 0) % tiles_per_seq

    scale = np.float32((D_NOPE + D_ROPE) ** -0.5)
    kern = functools.partial(_preproj_kernel, n_ctx_tiles=n_ctx_tiles, q_lora=q_lora, kv_lora=kv_lora,
                             scale=scale)
    return pl.pallas_call(
        kern,
        grid=(t // tm,),
        in_specs=[
            pl.BlockSpec((tm, d), lambda i: (jnp.minimum(i, n_ctx_tiles - 1), 0)),
            pl.BlockSpec((tm, d), lambda i: (jnp.maximum(i - n_ctx_tiles, 0), 0)),
            _const_spec((1, d)),
            pl.BlockSpec((None, None, 1, d), lambda i: (1, mod_row(i), 0, 0)),
            pl.BlockSpec((None, None, 1, d), lambda i: (0, mod_row(i), 0, 0)),
            _const_spec(w_small_t.shape),
            _const_spec((1, q_lora)),
            _const_spec((1, kv_lora)),
            pl.BlockSpec((tm, ROPE_SLOT), lambda i:(rope_blk(i), 0)),
            pl.BlockSpec((tm, ROPE_SLOT), lambda i:(rope_blk(i), 0)),
            _const_spec(wq_p.shape),
            pl.BlockSpec((front, d), lambda i: (jnp.maximum(2 * i - 1, 0), 0)),
            pl.BlockSpec((front, d), lambda i: (2 * i, 0)),
        ],
        out_specs=[
            pl.BlockSpec((tm, d), lambda i: (i, 0)),
            pl.BlockSpec((tm, kv_lora), lambda i: (i, 0)),
            pl.BlockSpec((tm, ROPE_SLOT), lambda i:(i, 0)),
            pl.BlockSpec((N_HEADS, tm, HEAD_SLOT), lambda i: (0, i, 0)),
            pl.BlockSpec((2 * front, d), lambda i: (i, 0)),
            pl.BlockSpec((tm, kv_lora), lambda i: (jnp.minimum(i, n_ctx_tiles - 1), 0)),
            pl.BlockSpec((tm, D_ROPE), lambda i: (jnp.minimum(i, n_ctx_tiles - 1), 0)),
        ],
        out_shape=[
            jax.ShapeDtypeStruct((t, d), BF16),
            jax.ShapeDtypeStruct((t, kv_lora), F32),
            jax.ShapeDtypeStruct((t, ROPE_SLOT), F32),
            jax.ShapeDtypeStruct((N_HEADS, t, HEAD_SLOT), BF16),
            jax.ShapeDtypeStruct(((n_wblk + 1) * front, d), BF16),
            jax.ShapeDtypeStruct((n_ctx, kv_lora), F32),
            jax.ShapeDtypeStruct((n_ctx, D_ROPE), F32),
        ],
        compiler_params=_params(1),
        name="pre_proj",
    )(xc, xl, g_attn, mod6, mod6, w_small_t, g_qa, g_kv, cos_t, sin_t, wq_p, w_in_t, w_in_t)


def _wideproj_kernel(h_ref, wa_ref, wb_ref, o_ref, *, tm, bn, n_gelu_steps):
    j = pl.program_id(0)
    h = h_ref[...]

    def run(act):
        for c, w_ref in enumerate((wa_ref, wb_ref)):
            y = _bdot_t(h, w_ref[...])
            for r0 in range(0, tm, 256):
                o_ref[r0:r0 + 256, c * bn:(c + 1) * bn] = act(y[r0:r0 + 256, :]).astype(BF16)

    @pl.when(j < n_gelu_steps)
    def _():
        run(_gelu_tanh)

    @pl.when(j >= n_gelu_steps)
    def _():
        run(lambda y: y)


def _wide_proj(h, w_all_t, *, n_big, n_gelu, tm, bn):
    t, d = h.shape
    blk0 = (w_all_t.shape[0] - n_big) // bn
    return pl.pallas_call(
        functools.partial(_wideproj_kernel, tm=tm, bn=bn, n_gelu_steps=n_gelu // (2 * bn)),
        grid=(n_big // (2 * bn), t // tm),
        in_specs=[
            pl.BlockSpec((tm, d), lambda j, i: (i, 0)),
            pl.BlockSpec((bn, d), lambda j, i: (blk0 + 2 * j, 0)),
            pl.BlockSpec((bn, d), lambda j, i: (blk0 + 2 * j + 1, 0)),
        ],
        out_specs=pl.BlockSpec((tm, 2 * bn), lambda j, i: (i, j)),
        out_shape=jax.ShapeDtypeStruct((t, n_big), BF16),
        compiler_params=_params(2),
        name="wide_proj",
    )(h, w_all_t, w_all_t)


def _attn_kernel(*refs, n_own, n_cache, heads_per_iter, n_side):
    refs = list(refs)
    q_ref, ckv_ref, kr_ref = refs[:3]
    del refs[:3]
    if n_cache:
        cckv_ref, ckr_ref = refs[:2]
        del refs[:2]
    wuk_ref, wuv_ref = refs[:2]
    del refs[:2]
    if n_side:
        c_ref, wada_ref, bada_ref = refs[:3]
        side_in = refs[3:3 + n_side]
        del refs[:3 + n_side]
    o_ref = refs.pop(0)
    if n_side:
        mod_ref = refs.pop(0)
        side_out = refs[:n_side]
        del refs[:n_side]
        mod_ref[...] = _bdot(_silu(c_ref[...]).astype(BF16), wada_ref[...].astype(BF16)) + bada_ref[...]
        for src, dst in zip(side_in, side_out):
            dst[...] = src[...].astype(BF16)
    kpad, vexp, kall, krp, o_scr = refs
    qi = pl.program_id(1)

    @pl.when(qi == 0)
    def _():
        kall[0:n_own, :] = ckv_ref[...].astype(BF16)
        krp[0:n_own, :] = kr_ref[...].astype(BF16)
        if n_cache:
            kall[n_own:n_own + n_cache, :] = cckv_ref[...].astype(BF16)
            krp[n_own:n_own + n_cache, 0:D_ROPE] = ckr_ref[...].astype(BF16)
            krp[n_own:n_own + n_cache, D_ROPE:ROPE_SLOT] = jnp.zeros((n_cache, ROPE_SLOT - D_ROPE), BF16)

        def expand(hp, carry):
            kn = _bdot(kall[...], wuk_ref[hp]).astype(BF16)
            vv = _bdot(kall[...], wuv_ref[hp]).astype(BF16)
            for s in range(2):
                kpad[2 * hp + s, :, 0:D_NOPE] = kn[:, s * D_NOPE:(s + 1) * D_NOPE]
                kpad[2 * hp + s, :, D_NOPE:HEAD_SLOT] = krp[...]
                vexp[2 * hp + s] = vv[:, s * D_V:(s + 1) * D_V]
            return carry

        if heads_per_iter == N_HEADS:
            for hp in range(N_HEADS // 2):
                expand(hp, 0)
        else:
            lax.fori_loop(0, N_HEADS // 2, expand, 0)

    def one_head(h):
        s = _bdot_t(q_ref[h], kpad[h])
        p = jnp.exp(s - jnp.max(s, axis=-1, keepdims=True))
        l = jnp.sum(p, axis=-1, keepdims=True)
        o = _bdot(p.astype(BF16), vexp[h])
        return (o / l).astype(BF16)

    if heads_per_iter == N_HEADS:
        for h in range(N_HEADS):
            o_ref[:, h * D_V:(h + 1) * D_V] = one_head(h)
    else:
        def head_group(hg, carry):
            for g in range(heads_per_iter):
                h = hg * heads_per_iter + g
                o_scr[h] = one_head(h)
            return carry

        lax.fori_loop(0, N_HEADS // heads_per_iter, head_group, 0)
        for h in range(N_HEADS):
            o_ref[:, h * D_V:(h + 1) * D_V] = o_scr[h]


def _attention(q_pad, ckv, kr, w_uk, w_uv, cache_ckv, cache_kr, *, row0, n_req, n_own, tq, heads_per_iter,
               side=None):
    kv_lora = ckv.shape[1]
    n_cache = 0 if cache_ckv is None else cache_ckv.shape[1]
    kn = n_own + n_cache
    qb = n_own // tq
    in_specs = [
        pl.BlockSpec((N_HEADS, tq, HEAD_SLOT), lambda b, qi: (0, row0 // tq + b * qb + qi, 0)),
        pl.BlockSpec((n_own, kv_lora), lambda b, qi: (row0 // n_own + b, 0)),
        pl.BlockSpec((n_own, ROPE_SLOT), lambda b, qi: (row0 // n_own + b, 0)),
    ]
    args = [q_pad, ckv, kr]
    if n_cache:
        in_specs += [
            pl.BlockSpec((None, n_cache, kv_lora), lambda b, qi: (b, 0, 0)),
            pl.BlockSpec((None, n_cache, D_ROPE), lambda b, qi: (b, 0, 0)),
        ]
        args += [cache_ckv, cache_kr]
    in_specs += [_const_spec(w_uk.shape), _const_spec(w_uv.shape)]
    args += [w_uk, w_uv]
    out_specs = [pl.BlockSpec((tq, N_HEADS * D_V), lambda b, qi: (b * qb + qi, 0))]
    out_shape = [jax.ShapeDtypeStruct((n_req * n_own, N_HEADS * D_V), BF16)]
    n_side = 0
    if side is not None:
        assert qb == 1
        c_rows, w_ada, b_ada, col0, weights = side
        d_model, n_mod = w_ada.shape
        mcols = (n_mod - col0) // n_req
        assert mcols * n_req == n_mod - col0 and mcols % LANES == 0 and col0 % mcols == 0
        in_specs += [
            _const_spec(c_rows.shape),
            pl.BlockSpec((d_model, mcols), lambda b, qi: (0, col0 // mcols + b)),
            pl.BlockSpec((1, mcols), lambda b, qi: (0, col0 // mcols + b)),
        ]
        args += [c_rows, w_ada, b_ada]
        out_specs.append(pl.BlockSpec((MOD_ROWS, mcols), lambda b, qi: (0, b)))
        out_shape.append(jax.ShapeDtypeStruct((MOD_ROWS, n_mod - col0), F32))
        n_side = len(weights)
        assert n_side > 0
        for w in weights:
            rows = w.shape[0] // n_req
            assert rows * n_req == w.shape[0] and rows % 16 == 0
            in_specs.append(pl.BlockSpec((rows, w.shape[1]), lambda b, qi: (b, 0)))
            args.append(w)
            out_specs.append(pl.BlockSpec((rows, w.shape[1]), lambda b, qi: (b, 0)))
            out_shape.append(jax.ShapeDtypeStruct(w.shape, BF16))
    res = pl.pallas_call(
        functools.partial(_attn_kernel, n_own=n_own, n_cache=n_cache, heads_per_iter=heads_per_iter,
                          n_side=n_side),
        grid=(n_req, qb),
        in_specs=in_specs,
        out_specs=out_specs,
        out_shape=out_shape,
        scratch_shapes=[
            pltpu.VMEM((N_HEADS, kn, HEAD_SLOT), BF16),
            pltpu.VMEM((N_HEADS, kn, D_V), BF16),
            pltpu.VMEM((kn, kv_lora), BF16),
            pltpu.VMEM((kn, ROPE_SLOT), BF16),
            pltpu.VMEM((N_HEADS, tq, D_V), BF16),
        ],
        compiler_params=_params(2),
        name="mla_attn_cache" if n_cache else "mla_attn",
    )(*args)
    if side is None:
        return res[0]
    return res[0], res[1], res[2:]


def _oproj_kernel(xc_ref, xl_ref, w_ref, o_ref, *, n_ctx_tiles):
    i = pl.program_id(0)

    @pl.when(i < n_ctx_tiles)
    def _():
        o_ref[...] = _bdot(xc_ref[...], w_ref[...]).astype(o_ref.dtype)

    @pl.when(i >= n_ctx_tiles)
    def _():
        o_ref[...] = _bdot(xl_ref[...], w_ref[...]).astype(o_ref.dtype)


def _o_proj(xc, xl, w, *, tm):
    n_ctx, k = xc.shape
    t = n_ctx + xl.shape[0]
    n = w.shape[1]
    n_ctx_tiles = n_ctx // tm
    return pl.pallas_call(
        functools.partial(_oproj_kernel, n_ctx_tiles=n_ctx_tiles),
        grid=(t // tm,),
        in_specs=[
            pl.BlockSpec((tm, k), lambda i: (jnp.minimum(i, n_ctx_tiles - 1), 0)),
            pl.BlockSpec((tm, k), lambda i: (jnp.maximum(i - n_ctx_tiles, 0), 0)),
            _const_spec(w.shape),
        ],
        out_specs=pl.BlockSpec((tm, n), lambda i: (i, 0)),
        out_shape=jax.ShapeDtypeStruct((t, n), BF16),
        compiler_params=_params(1),
        name="o_proj",
    )(xc, xl, w)


def _sgu_merge_kernel(gu_ref, gv_ref, ga_ref, gb_ref, oa_ref, g_ref, ws_ref, bs_ref, wo_ref, m_ref,
                      z_even, z_odd, *, tm, gc):
    s = pl.program_id(0)

    @pl.when(s == 0)
    def _():
        z_odd[...] = jnp.zeros(z_odd.shape, BF16)

    def body(z_fill, z_drain):
        o_b = _bdot(z_drain[...], wo_ref[...])
        for r0 in range(0, tm, CHUNK):
            rows = slice(r0, r0 + CHUNK)
            oa = oa_ref[rows, :]
            ob = o_b[rows, :].astype(BF16)
            ta = _half_tanh_half(ga_ref[rows, :])
            tb = _half_tanh_half(gb_ref[rows, :])
            m_ref[rows, :] = 0.5 * ((oa + ob) + (ta * oa + tb * ob))

        g_bf = g_ref[...].astype(BF16)
        for r0 in range(0, tm, CHUNK):
            v = gv_ref[r0:r0 + CHUNK, :]
            v32 = v.astype(F32)
            inv = lax.rsqrt(jnp.mean(v32 * v32, axis=-1, keepdims=True) + EPS)
            vn = v * inv.astype(BF16) * g_bf
            for g in range(GM_GROUPS):
                c0 = g * gc
                mix = _bdot(ws_ref[g], vn[:, c0:c0 + gc]) + bs_ref[:, g:g + 1]
                u = gu_ref[r0:r0 + CHUNK, c0:c0 + gc]
                z_fill[r0:r0 + CHUNK, c0:c0 + gc] = u * mix.astype(BF16)

    @pl.when(s % 2 == 0)
    def _():
        body(z_even, z_odd)

    @pl.when(s % 2 == 1)
    def _():
        body(z_odd, z_even)


def _sgu_merge(big, o_a, g_sgu, w_s, b_s_t, w_o_gm, *, tm):
    t = big.shape[0]
    width = g_sgu.shape[1]
    d = w_o_gm.shape[1]
    gc = width // GM_GROUPS
    nt = t // tm

    def fill(s):
        return jnp.minimum(s, nt - 1)

    def drain(s):
        return jnp.maximum(s - 1, 0)

    return pl.pallas_call(
        functools.partial(_sgu_merge_kernel, tm=tm, gc=gc),
        grid=(nt + 1,),
        in_specs=[
            pl.BlockSpec((tm, width), lambda s: (fill(s), 0)),
            pl.BlockSpec((tm, width), lambda s: (fill(s), 1)),
            pl.BlockSpec((tm, d), lambda s: (drain(s), 2)),
            pl.BlockSpec((tm, d), lambda s: (drain(s), 3)),
            pl.BlockSpec((tm, d), lambda s: (drain(s), 0)),
            _const_spec((1, width)),
            _const_spec(w_s.shape),
            _const_spec(b_s_t.shape),
            _const_spec(w_o_gm.shape),
        ],
        out_specs=pl.BlockSpec((tm, d), lambda s: (drain(s), 0)),
        out_shape=jax.ShapeDtypeStruct((t, d), BF16),
        scratch_shapes=[pltpu.VMEM((tm, width), BF16), pltpu.VMEM((tm, width), BF16)],
        compiler_params=_params(1),
        name="sgu_merge",
    )(big, big, big, big, o_a, g_sgu, w_s, b_s_t, w_o_gm)


def _outproj_kernel(m_ref, xc_ref, xl_ref, gate_ref, sc_ref, sh_ref, g_ref, wo_ref, wr_ref,
                    x1_ref, h2_ref, lg_ref, *, tm, n_ctx_tiles):
    i = pl.program_id(0)

    def body(x_ref):
        half = tm // 2
        for b0 in range(0, tm, half):
            r = _bdot(m_ref[b0:b0 + half, :], wo_ref[...])
            for r0 in range(0, half, CHUNK):
                rows = slice(b0 + r0, b0 + r0 + CHUNK)
                x1 = x_ref[rows, :] + gate_ref[...] * r[r0:r0 + CHUNK, :]
                x1_ref[rows, :] = x1
                h2 = (_rms_rows(x1, g_ref[...]) * (1.0 + sc_ref[...]) + sh_ref[...]).astype(BF16)
                h2_ref[rows, :] = h2
                lg_ref[rows, :] = _bdot(h2, wr_ref[...])

    @pl.when(i < n_ctx_tiles)
    def _():
        body(xc_ref)

    @pl.when(i >= n_ctx_tiles)
    def _():
        body(xl_ref)


def _out_proj(merged, xc, xl, mod6, g_ffn, w_out, w_router_p, *, n_lat_seq, tm):
    n_ctx, d = xc.shape
    t = n_ctx + xl.shape[0]
    n_ctx_tiles = n_ctx // tm
    tiles_per_seq = n_lat_seq // tm

    def mod_row(i):
        return jnp.where(i < n_ctx_tiles, MOD_ROWS // 2, (i - n_ctx_tiles) // tiles_per_seq)

    return pl.pallas_call(
        functools.partial(_outproj_kernel, tm=tm, n_ctx_tiles=n_ctx_tiles),
        grid=(t // tm,),
        in_specs=[
            pl.BlockSpec((tm, d), lambda i: (i, 0)),
            pl.BlockSpec((tm, d), lambda i: (jnp.minimum(i, n_ctx_tiles - 1), 0)),
            pl.BlockSpec((tm, d), lambda i: (jnp.maximum(i - n_ctx_tiles, 0), 0)),
            pl.BlockSpec((None, None, 1, d), lambda i: (2, mod_row(i), 0, 0)),
            pl.BlockSpec((None, None, 1, d), lambda i: (4, mod_row(i), 0, 0)),
            pl.BlockSpec((None, None, 1, d), lambda i: (3, mod_row(i), 0, 0)),
            _const_spec((1, d)),
            _const_spec(w_out.shape),
            _const_spec(w_router_p.shape),
        ],
        out_specs=[
            pl.BlockSpec((tm, d), lambda i: (i, 0)),
            pl.BlockSpec((tm, d), lambda i: (i, 0)),
            pl.BlockSpec((tm, LOGIT_LANES), lambda i: (i, 0)),
        ],
        out_shape=[
            jax.ShapeDtypeStruct((t, d), F32),
            jax.ShapeDtypeStruct((t, d), BF16),
            jax.ShapeDtypeStruct((t, LOGIT_LANES), F32),
        ],
        compiler_params=_params(1),
        name="out_proj",
    )(merged, xc, xl, mod6, mod6, mod6, g_ffn, w_out, w_router_p)


BISECT_STEPS = 48
MIN_NORMAL_F32 = float(np.finfo(np.float32).tiny)


def _route_kernel(lg_ref, key_ref, aff_ref, tri_scr, *, n_sets, n, cap):
    for r0 in range(0, n, LANES):
        r = lax.broadcasted_iota(jnp.int32, (LANES, n), 0) + r0
        c = lax.broadcasted_iota(jnp.int32, (LANES, n), 1)
        tri_scr[r0:r0 + LANES, :] = jnp.where(r < c, 1.0, 0.0).astype(BF16)

    for s in range(n_sets):
        logits = lg_ref[s * n:(s + 1) * n, :].T[0:N_EXPERTS, :]
        e = jnp.exp(logits - jnp.max(logits, axis=0, keepdims=True))
        aff_ref[s * N_EXPERTS:(s + 1) * N_EXPERTS, :] = e / jnp.sum(e, axis=0, keepdims=True)
    aff = aff_ref[...]
    rows = n_sets * N_EXPERTS

    def count_ge(thr):
        return jnp.sum(jnp.where(aff >= thr, 1.0, 0.0), axis=1, keepdims=True)

    def bisect(_, carry):
        lo, hi = carry
        mid = jnp.sqrt(lo) * jnp.sqrt(hi)
        ok = count_ge(mid) >= cap
        return jnp.where(ok, mid, lo), jnp.where(ok, hi, mid)

    lo0 = jnp.full((rows, 1), MIN_NORMAL_F32, F32)
    hi0 = jnp.full((rows, 1), 2.0, F32)
    lo, hi = lax.fori_loop(0, BISECT_STEPS, bisect, (lo0, hi0))
    lo = jnp.where(count_ge(lo) >= cap, lo, 0.0)

    above = aff >= hi
    band = (aff >= lo) & jnp.logical_not(above)
    need = cap - jnp.sum(jnp.where(above, 1.0, 0.0), axis=1, keepdims=True)
    tri = tri_scr[...]
    band_before = _bdot(jnp.where(band, 1.0, 0.0).astype(BF16), tri)
    sel = above | (band & (band_before < need))
    pos = _bdot(jnp.where(sel, 1.0, 0.0).astype(BF16), tri)
    key_ref[...] = jnp.where(sel, pos, -1.0)


def _route(logits, *, row0, n_sets, n):
    cap = EC_FACTOR * n // N_EXPERTS
    rows = n_sets * N_EXPERTS
    blk = row0 // (n_sets * n)
    return pl.pallas_call(
        functools.partial(_route_kernel, n_sets=n_sets, n=n, cap=cap),
        grid=(1,),
        in_specs=[pl.BlockSpec((n_sets * n, LOGIT_LANES), lambda g: (blk, 0))],
        out_specs=[pl.BlockSpec((rows, n), lambda g: (0, 0)), pl.BlockSpec((rows, n), lambda g: (0, 0))],
        out_shape=[jax.ShapeDtypeStruct((rows, n), F32), jax.ShapeDtypeStruct((rows, n), F32)],
        scratch_shapes=[pltpu.VMEM((n, n), BF16)],
        compiler_params=_params(1),
        name=f"route_{n}",
    )(logits)


def _gather_kernel(key_ref, aff_ref, h2_ref, pt_ref, gate_ref, xg_ref, p_scr, *, sps, n, cap, d, nb):
    slot = lax.broadcasted_iota(jnp.int32, (cap, n), 0).astype(F32)
    per_group = LANES // cap
    for s in range(sps):
        key = key_ref[s * N_EXPERTS:(s + 1) * N_EXPERTS, :]
        aff = aff_ref[s * N_EXPERTS:(s + 1) * N_EXPERTS, :]
        for grp in range(N_EXPERTS // per_group):
            pieces = []
            for ex in range(grp * per_group, (grp + 1) * per_group):
                hit = slot == key[ex:ex + 1, :]
                gate_ref[ex, s] = jnp.sum(jnp.where(hit, aff[ex:ex + 1, :], 0.0), axis=1, keepdims=True)
                pieces.append(jnp.where(hit, 1.0, 0.0))
            hits = pieces[0] if per_group == 1 else jnp.concatenate(pieces, axis=0)
            rows = slice(grp * LANES, (grp + 1) * LANES)
            p_scr[s, rows, :] = hits.astype(BF16)
            pt_ref[s, :, rows] = hits.T.astype(BF16)

        p = p_scr[s]
        for c in range(d // nb):
            xg = _bdot(p, h2_ref[s * n:(s + 1) * n, c * nb:(c + 1) * nb]).astype(BF16)
            for ex in range(N_EXPERTS):
                xg_ref[ex, s, :, c * nb:(c + 1) * nb] = xg[ex * cap:(ex + 1) * cap, :]


def _dispatch_gather(key, aff, h2, *, row0, n_sets, n, sps):
    d = h2.shape[1]
    cap = EC_FACTOR * n // N_EXPERTS
    slots = N_EXPERTS * cap
    blk0 = row0 // (sps * n)
    return pl.pallas_call(
        functools.partial(_gather_kernel, sps=sps, n=n, cap=cap, d=d, nb=MXU_WIDTH),
        grid=(n_sets // sps,),
        in_specs=[
            pl.BlockSpec((sps * N_EXPERTS, n), lambda b: (b, 0)),
            pl.BlockSpec((sps * N_EXPERTS, n), lambda b: (b, 0)),
            pl.BlockSpec((sps * n, d), lambda b: (blk0 + b, 0)),
        ],
        out_specs=[
            pl.BlockSpec((sps, n, slots), lambda b: (b, 0, 0)),
            pl.BlockSpec((N_EXPERTS, sps, cap, 1), lambda b: (0, b, 0, 0)),
            pl.BlockSpec((N_EXPERTS, sps, cap, d), lambda b: (0, b, 0, 0)),
        ],
        out_shape=[
            jax.ShapeDtypeStruct((n_sets, n, slots), BF16),
            jax.ShapeDtypeStruct((N_EXPERTS, n_sets, cap, 1), F32),
            jax.ShapeDtypeStruct((N_EXPERTS, n_sets, cap, d), BF16),
        ],
        scratch_shapes=[pltpu.VMEM((sps, slots, n), BF16)],
        compiler_params=_params(1),
        name=f"dispatch_gather_{n}",
    )(key, aff, h2)


def _expert_kernel(xc_ref, xl_ref, gc_ref, gl_ref, w1_ref, w3_ref, w2_ref, yc_ref, yl_ref, hc_scr, hl_scr,
                   *, n_f, fc, rows_c, rows_l, d):
    k = pl.program_id(1)

    @pl.when(k < n_f)
    def _():
        w1 = w1_ref[...].astype(BF16)
        w3 = w3_ref[...].astype(BF16)
        for x_ref, h_scr, rows in ((xc_ref, hc_scr, rows_c), (xl_ref, hl_scr, rows_l)):
            x = x_ref[...].reshape(rows, d)
            a = _bdot(x, w1)
            g = _bdot(x, w3)
            h_scr[k] = (_silu(a) * g).astype(BF16)

    @pl.when(k >= n_f)
    def _():
        w2 = w2_ref[...].astype(BF16)
        for h_scr, g_ref, y_ref, rows in ((hc_scr, gc_ref, yc_ref, rows_c), (hl_scr, gl_ref, yl_ref, rows_l)):
            y = _bdot(h_scr[0], w2[0:fc, :])
            for kk in range(1, n_f):
                y = y + _bdot(h_scr[kk], w2[kk * fc:(kk + 1) * fc, :])
            y_ref[...] = (y * g_ref[...].reshape(rows, 1)).astype(BF16).reshape(y_ref.shape)


def _experts(xg_c, xg_l, gate_c, gate_l, w_e1, w_e3, w_e2, *, fc=512, nc=512):
    n_e, sets_c, cap_c, d = xg_c.shape
    _, sets_l, cap_l, _ = xg_l.shape
    d_ff = w_e1.shape[2]
    n_f = d_ff // fc
    n_c = d // nc
    rows_c, rows_l = sets_c * cap_c, sets_l * cap_l

    def e_in(e, k):
        return jnp.minimum(e + (k >= n_f).astype(jnp.int32), n_e - 1)

    def f_idx(k):
        return jnp.where(k < n_f, k, 0)

    def c_idx(k):
        return jnp.maximum(k - n_f, 0)

    kern = functools.partial(_expert_kernel, n_f=n_f, fc=fc, rows_c=rows_c, rows_l=rows_l, d=d)
    return pl.pallas_call(
        kern,
        grid=(n_e, n_f + n_c),
        in_specs=[
            pl.BlockSpec((None, sets_c, cap_c, d), lambda e, k: (e_in(e, k), 0, 0, 0)),
            pl.BlockSpec((None, sets_l, cap_l, d), lambda e, k: (e_in(e, k), 0, 0, 0)),
            pl.BlockSpec((None, sets_c, cap_c, 1), lambda e, k: (e, 0, 0, 0)),
            pl.BlockSpec((None, sets_l, cap_l, 1), lambda e, k: (e, 0, 0, 0)),
            pl.BlockSpec((None, d, fc), lambda e, k: (e_in(e, k), 0, f_idx(k))),
            pl.BlockSpec((None, d, fc), lambda e, k: (e_in(e, k), 0, f_idx(k))),
            pl.BlockSpec((None, d_ff, nc), lambda e, k: (e, 0, c_idx(k))),
        ],
        out_specs=[
            pl.BlockSpec((None, sets_c, cap_c, nc), lambda e, k: (e, 0, 0, c_idx(k))),
            pl.BlockSpec((None, sets_l, cap_l, nc), lambda e, k: (e, 0, 0, c_idx(k))),
        ],
        out_shape=[
            jax.ShapeDtypeStruct(xg_c.shape, BF16),
            jax.ShapeDtypeStruct(xg_l.shape, BF16),
        ],
        scratch_shapes=[pltpu.VMEM((n_f, rows_c, fc), BF16), pltpu.VMEM((n_f, rows_l, fc), BF16)],
        compiler_params=_params(2),
        name="experts",
    )(xg_c, xg_l, gate_c, gate_l, w_e1, w_e3, w_e2)


def _combine_kernel(pt_ref, y_ref, x1_ref, gate_ref, g_ref, o_ref, acc_scr, *, sps, tn, slots, d, nb):
    for s in range(sps):
        pt = pt_ref[s]
        rows = slice(s * tn, (s + 1) * tn)
        for c in range(d // nb):
            cols = slice(c * nb, (c + 1) * nb)
            moe = _bdot(pt, y_ref[:, s, :, cols].reshape(slots, nb))
            acc_scr[:, cols] = x1_ref[rows, cols] + gate_ref[:, cols] * moe
        o_ref[rows, :] = _rms_rows(acc_scr[...], g_ref[...])


def _combine(pt, y, x1, mod6, g_final, *, row0, n_sets, n, sps, mod_row_fn):
    d = x1.shape[1]
    cap = EC_FACTOR * n // N_EXPERTS
    slots = N_EXPERTS * cap
    tn = min(n, 512)
    nt = n // tn
    assert sps == 1 or nt == 1
    blk0 = row0 // (sps * tn)
    return pl.pallas_call(
        functools.partial(_combine_kernel, sps=sps, tn=tn, slots=slots, d=d, nb=512),
        grid=(n_sets // sps, nt),
        in_specs=[
            pl.BlockSpec((sps, tn, slots), lambda b, r: (b, r, 0)),
            pl.BlockSpec((N_EXPERTS, sps, cap, d), lambda b, r: (0, b, 0, 0)),
            pl.BlockSpec((sps * tn, d), lambda b, r: (blk0 + b * nt + r, 0)),
            pl.BlockSpec((None, None, 1, d), lambda b, r: (5, mod_row_fn(b), 0, 0)),
            _const_spec((1, d)),
        ],
        out_specs=pl.BlockSpec((sps * tn, d), lambda b, r: (b * nt + r, 0)),
        out_shape=jax.ShapeDtypeStruct((n_sets * n, d), F32),
        scratch_shapes=[pltpu.VMEM((tn, d), F32)],
        compiler_params=_params(2),
        name=f"combine_{n}",
    )(pt, y, x1, mod6, g_final)


def _rope_tables(n):
    tpos = jnp.arange(n, dtype=jnp.int32)
    row = (tpos // GRID_W).astype(F32)
    col = (tpos % GRID_W).astype(F32)
    inv = 1.0 / (ROPE_BASE ** (jnp.arange(ROPE_FREQS, dtype=F32) / ROPE_FREQS))
    ang = jnp.stack([row[:, None] * inv, col[:, None] * inv], axis=1)
    cos = jnp.cos(ang)[:, :, None, :]
    sin = jnp.sin(ang)[:, :, None, :]
    cos = jnp.broadcast_to(cos, (n, 2, 2, ROPE_FREQS)).reshape(n, D_ROPE)
    sin = jnp.concatenate([-sin, sin], axis=2).reshape(n, D_ROPE)
    pad = jnp.zeros((n, ROPE_SLOT - D_ROPE), F32)
    return jnp.concatenate([cos, pad], axis=1), jnp.concatenate([sin, pad], axis=1)


def kernel(x_prompt, x_sample, c, cache_ckv, cache_krope, c_ctx, g_attn, g_ffn, w_ada, b_ada, w_in, g_qa,
           w_qb, g_kv, w_uk, w_uv, w_o_mla, g_sgu, w_s, b_s, w_o_gm, w_out, w_router, w_e1, w_e3, w_e2,
           g_final):
    batch, seq, d = x_prompt.shape
    dec_batch, dec_seq, _ = x_sample.shape
    depth = g_attn.shape[0]
    assert depth == 1
    q_lora, kv_lora = g_qa.shape[1], g_kv.shape[1]
    gm_width = g_sgu.shape[1]
    n_ctx, n_lat = batch * seq, dec_batch * dec_seq
    assert dec_batch < MOD_ROWS // 2 + 1

    xc = x_prompt.reshape(n_ctx, d)
    xl = x_sample.reshape(n_lat, d)

    c_rows = jnp.zeros((MOD_ROWS, d), F32).at[:dec_batch].set(c).at[MOD_ROWS // 2].set(c_ctx)
    n_mod_early = 2 * d
    mod_a = _modulation(c_rows, w_ada[0], b_ada[0][None, :], n_cols=n_mod_early)
    mod6_a = mod_a.reshape(MOD_ROWS, 2, 1, d).transpose(1, 0, 2, 3)

    cos_t, sin_t = _rope_tables(dec_seq)

    w_in_t = w_in[0].T
    o_kr = q_lora + kv_lora
    w_small_t = jnp.pad(w_in_t[:o_kr + D_ROPE], ((0, ROPE_SLOT - D_ROPE), (0, 0))).astype(BF16)
    tm = ROW_TILE
    front = -(o_kr + D_ROPE) % WIDE_WEIGHT_BLOCK

    wq3 = w_qb[0].reshape(q_lora, N_HEADS, D_NOPE + D_ROPE)
    wq_nope, wq_rope = wq3[:, :, :D_NOPE], wq3[:, :, D_NOPE:]
    zq = jnp.zeros((q_lora, N_HEADS, HEAD_SLOT - D_NOPE - D_ROPE), F32)
    wq_p = jnp.concatenate([wq_nope, wq_rope, zq], axis=2).reshape(q_lora, N_HEADS * HEAD_SLOT).astype(BF16)

    h, ckv, kr, q_pad, w_all_t, ckv_ctx, kr_ctx = _pre_proj(
        xc, xl, g_attn, mod6_a, w_small_t, g_qa, g_kv, cos_t, sin_t, wq_p, w_in_t, front=front, tm=tm)
    big = _wide_proj(h, w_all_t, n_big=w_in_t.shape[0] - o_kr - D_ROPE, n_gelu=2 * gm_width,
                     tm=MATMUL_ROW_TILE, bn=WIDE_WEIGHT_BLOCK)

    w_uk2 = w_uk[0].reshape(kv_lora, N_HEADS // 2, 2 * D_NOPE).transpose(1, 0, 2).astype(BF16)
    w_uv2 = w_uv[0].reshape(kv_lora, N_HEADS // 2, 2 * D_V).transpose(1, 0, 2).astype(BF16)
    side = (c_rows, w_ada[0], b_ada[0][None, :], n_mod_early, (w_o_mla[0], w_o_gm[0], w_out[0]))
    o_ctx, mod_b, (w_o_mla_b, w_o_gm_b, w_out_b) = _attention(
        q_pad, ckv, kr, w_uk2, w_uv2, None, None,
        row0=0, n_req=batch, n_own=seq, tq=seq, heads_per_iter=N_HEADS, side=side)
    o_lat = _attention(q_pad, ckv, kr, w_uk2, w_uv2, cache_ckv[:, 0], cache_krope[:, 0],
                       row0=n_ctx, n_req=dec_batch, n_own=dec_seq, tq=ATTN_Q_TILE,
                       heads_per_iter=ATTN_HEADS_PER_STEP)
    mod6 = jnp.concatenate([mod6_a, mod_b.reshape(MOD_ROWS, 4, 1, d).transpose(1, 0, 2, 3)], axis=0)

    o_a = _o_proj(o_ctx, o_lat, w_o_mla_b, tm=MATMUL_ROW_TILE)
    merged = _sgu_merge(big, o_a, g_sgu, w_s[0].astype(BF16), b_s[0].T, w_o_gm_b, tm=tm)

    w_router_p = jnp.concatenate(
        [w_router[0], jnp.zeros((d, LOGIT_LANES - N_EXPERTS), F32)], axis=1).astype(BF16)
    x1, h2, logits = _out_proj(merged, xc, xl, mod6, g_ffn, w_out_b, w_router_p,
                               n_lat_seq=dec_seq, tm=tm)

    key_c, aff_c = _route(logits, row0=0, n_sets=batch, n=seq)
    key_l, aff_l = _route(logits, row0=n_ctx, n_sets=dec_batch, n=dec_seq)
    p_c, gate_c, xg_c = _dispatch_gather(key_c, aff_c, h2, row0=0, n_sets=batch, n=seq,
                                         sps=CTX_SETS_PER_GATHER_STEP)
    p_l, gate_l, xg_l = _dispatch_gather(key_l, aff_l, h2, row0=n_ctx, n_sets=dec_batch, n=dec_seq, sps=1)

    y_c, y_l = _experts(xg_c, xg_l, gate_c, gate_l, w_e1[0], w_e3[0], w_e2[0])

    y_prompt = _combine(p_c, y_c, x1, mod6, g_final[None, :], row0=0, n_sets=batch, n=seq,
                        sps=CTX_SETS_PER_COMBINE_STEP, mod_row_fn=lambda b: MOD_ROWS // 2)
    y_sample = _combine(p_l, y_l, x1, mod6, g_final[None, :], row0=n_ctx, n_sets=dec_batch, n=dec_seq, sps=1,
                        mod_row_fn=lambda b: b)

    new_ckv = ckv_ctx.reshape(batch, 1, seq, kv_lora)
    new_krope = kr_ctx.reshape(batch, 1, seq, D_ROPE)
    return (y_prompt.reshape(batch, seq, d), y_sample.reshape(dec_batch, dec_seq, d), new_ckv, new_krope)
```

```python
import functools

import numpy as np
import jax
import jax.numpy as jnp
from jax import lax
from jax.experimental import pallas as pl
from jax.experimental.pallas import tpu as pltpu

F32 = jnp.float32
BF16 = jnp.bfloat16

LANES = 128
MXU_WIDTH = 256

N_HEADS = 16
D_NOPE = 128
D_ROPE = 64
D_V = 128
ROPE_SLOT = LANES
HEAD_SLOT = D_NOPE + ROPE_SLOT
ROPE_FREQS = D_ROPE // 4
ROPE_BASE = 10000.0
GRID_W = 64
CHUNK = 128
GM_GROUPS = 8
N_EXPERTS = 16
EC_FACTOR = 2
EPS = 1e-6
MOD_ROWS = 8
LOGIT_LANES = LANES

VMEM_LIMIT_V7X = 56 * 1024 * 1024

ROW_TILE = 512
MATMUL_ROW_TILE = 1024
WIDE_WEIGHT_BLOCK = 1024
ATTN_Q_TILE = 512
ATTN_HEADS_PER_STEP = 4
CTX_SETS_PER_GATHER_STEP = 4
CTX_SETS_PER_COMBINE_STEP = 2


def _params(n_axes):
    return pltpu.CompilerParams(
        dimension_semantics=("arbitrary",) * n_axes, vmem_limit_bytes=VMEM_LIMIT_V7X)


def _const_spec(shape):
    nd = len(shape)
    return pl.BlockSpec(shape, lambda *_: (0,) * nd, pipeline_mode=pl.Buffered(1))


def _half_tanh_half(x):
    return jnp.tanh(0.5 * x)


def _silu(x):
    h = 0.5 * x
    return h * jnp.tanh(h) + h


def _gelu_tanh(x):
    assert x.dtype == F32
    c = float(np.sqrt(2.0 / np.pi))
    return x * (0.5 * (1.0 + jnp.tanh(c * (x + 0.044715 * (x * x * x)))))


def _rms_rows(x, g):
    return x * lax.rsqrt(jnp.mean(x * x, axis=-1, keepdims=True) + EPS) * g


def _bdot(a, b):
    return jnp.dot(a, b, preferred_element_type=F32)


def _bdot_t(a, bt):
    return lax.dot_general(a, bt, (((1,), (1,)), ((), ())), preferred_element_type=F32)


def _mod_kernel(c_ref, w_ref, b_ref, o_ref):
    c = c_ref[...]
    s = _silu(c).astype(BF16)
    o_ref[...] = _bdot(s, w_ref[...].astype(BF16)) + b_ref[...]


def _modulation(c_rows, w_ada, b_ada, *, n_cols, bn=1024):
    d, n = w_ada.shape[0], n_cols
    return pl.pallas_call(
        _mod_kernel,
        grid=(n // bn,),
        in_specs=[
            _const_spec((MOD_ROWS, d)),
            pl.BlockSpec((d, bn), lambda j: (0, j)),
            pl.BlockSpec((1, bn), lambda j: (0, j)),
        ],
        out_specs=pl.BlockSpec((MOD_ROWS, bn), lambda j: (0, j)),
        out_shape=jax.ShapeDtypeStruct((MOD_ROWS, n), F32),
        compiler_params=_params(1),
        name="adaln_mod",
    )(c_rows, w_ada, b_ada)


def _preproj_kernel(xc_ref, xl_ref, g_ref, sc_ref, sh_ref, ws_ref, gqa_ref, gkv_ref, cos_ref, sin_ref,
                    wq_ref, wlo_ref, whi_ref, h_ref, ckv_ref, kr_ref, q_ref, wcast_ref,
                    new_ckv_ref, new_kr_ref, *, n_ctx_tiles, q_lora, kv_lora, scale):
    i = pl.program_id(0)

    rb = wlo_ref.shape[0]
    lo = wlo_ref[...].astype(BF16)
    wcast_ref[0:rb, :] = jnp.where(i == 0, jnp.zeros_like(lo), lo)
    wcast_ref[rb:2 * rb, :] = whi_ref[...].astype(BF16)

    def body(x_ref, rotary):
        h = _rms_rows(x_ref[...], g_ref[...]) * (1.0 + sc_ref[...]) + sh_ref[...]
        hb = h.astype(BF16)
        h_ref[...] = hb
        small = _bdot_t(hb, ws_ref[...])
        qa = _rms_rows(small[:, :q_lora], gqa_ref[...]).astype(BF16)
        ckv = _rms_rows(small[:, q_lora:q_lora + kv_lora], gkv_ref[...])
        ckv_ref[...] = ckv
        o = q_lora + kv_lora
        kr = small[:, o:o + ROPE_SLOT]
        if not rotary:
            kr_ref[...] = kr
            new_ckv_ref[...] = ckv
            new_kr_ref[...] = kr[:, :D_ROPE]
            for hd in range(N_HEADS):
                q = _bdot(qa, wq_ref[:, hd * HEAD_SLOT:(hd + 1) * HEAD_SLOT])
                q_ref[hd] = (q * scale).astype(BF16)
            return
        cos = cos_ref[...]
        sin = sin_ref[...]
        lane = lax.broadcasted_iota(jnp.int32, kr.shape, 1)
        first_half = (lane & (2 * ROPE_FREQS - 1)) < ROPE_FREQS

        def rotate(r):
            partner = jnp.where(first_half, pltpu.roll(r, ROPE_SLOT - ROPE_FREQS, 1), pltpu.roll(r, ROPE_FREQS, 1))
            return r * cos + partner * sin

        kr_ref[...] = rotate(kr)
        for hd in range(N_HEADS):
            q = _bdot(qa, wq_ref[:, hd * HEAD_SLOT:(hd + 1) * HEAD_SLOT])
            q_ref[hd, :, 0:D_NOPE] = (q[:, 0:D_NOPE] * scale).astype(BF16)
            q_ref[hd, :, D_NOPE:HEAD_SLOT] = (rotate(q[:, D_NOPE:HEAD_SLOT]) * scale).astype(BF16)

    @pl.when(i < n_ctx_tiles)
    def _():
        body(xc_ref, False)

    @pl.when(i >= n_ctx_tiles)
    def _():
        body(xl_ref, True)


def _pre_proj(xc, xl, g_attn, mod6, w_small_t, g_qa, g_kv, cos_t, sin_t, wq_p, w_in_t, *, front, tm):
    n_ctx, d = xc.shape
    n_lat_seq = cos_t.shape[0]
    t = n_ctx + xl.shape[0]
    q_lora, kv_lora = g_qa.shape[1], g_kv.shape[1]
    n_ctx_tiles = n_ctx // tm
    tiles_per_seq = n_lat_seq // tm
    n_wblk = w_in_t.shape[0] // front
    assert w_in_t.shape[0] == n_wblk * front and front % 16 == 0
    assert n_wblk + 1 == 2 * (t // tm), "weight row blocks must pair up with the token steps"

    def mod_row(i):
        return jnp.where(i < n_ctx_tiles, MOD_ROWS // 2, (i - n_ctx_tiles) // tiles_per_seq)

    def rope_blk(i):
        return jnp.maximum(i - n_ctx_tiles, 0) % tiles_per_seq

    scale = np.float32((D_NOPE + D_ROPE) ** -0.5)
    kern = functools.partial(_preproj_kernel, n_ctx_tiles=n_ctx_tiles, q_lora=q_lora, kv_lora=kv_lora,
                             scale=scale)
    return pl.pallas_call(
        kern,
        grid=(t // tm,),
        in_specs=[
            pl.BlockSpec((tm, d), lambda i: (jnp.minimum(i, n_ctx_tiles - 1), 0)),
            pl.BlockSpec((tm, d), lambda i: (jnp.maximum(i - n_ctx_tiles, 0), 0)),
            _const_spec((1, d)),
            pl.BlockSpec((None, None, 1, d), lambda i: (1, mod_row(i), 0, 0)),
            pl.BlockSpec((None, None, 1, d), lambda i: (0, mod_row(i), 0, 0)),
            _const_spec(w_small_t.shape),
            _const_spec((1, q_lora)),
            _const_spec((1, kv_lora)),
            pl.BlockSpec((tm, ROPE_SLOT), lambda i:(rope_blk(i), 0)),
            pl.BlockSpec((tm, ROPE_SLOT), lambda i:(rope_blk(i), 0)),
            _const_spec(wq_p.shape),
            pl.BlockSpec((front, d), lambda i: (jnp.maximum(2 * i - 1, 0), 0)),
            pl.BlockSpec((front, d), lambda i: (2 * i, 0)),
        ],
        out_specs=[
            pl.BlockSpec((tm, d), lambda i: (i, 0)),
            pl.BlockSpec((tm, kv_lora), lambda i: (i, 0)),
            pl.BlockSpec((tm, ROPE_SLOT), lambda i:(i, 0)),
            pl.BlockSpec((N_HEADS, tm, HEAD_SLOT), lambda i: (0, i, 0)),
            pl.BlockSpec((2 * front, d), lambda i: (i, 0)),
            pl.BlockSpec((tm, kv_lora), lambda i: (jnp.minimum(i, n_ctx_tiles - 1), 0)),
            pl.BlockSpec((tm, D_ROPE), lambda i: (jnp.minimum(i, n_ctx_tiles - 1), 0)),
        ],
        out_shape=[
            jax.ShapeDtypeStruct((t, d), BF16),
            jax.ShapeDtypeStruct((t, kv_lora), F32),
            jax.ShapeDtypeStruct((t, ROPE_SLOT), F32),
            jax.ShapeDtypeStruct((N_HEADS, t, HEAD_SLOT), BF16),
            jax.ShapeDtypeStruct(((n_wblk + 1) * front, d), BF16),
            jax.ShapeDtypeStruct((n_ctx, kv_lora), F32),
            jax.ShapeDtypeStruct((n_ctx, D_ROPE), F32),
        ],
        compiler_params=_params(1),
        name="pre_proj",
    )(xc, xl, g_attn, mod6, mod6, w_small_t, g_qa, g_kv, cos_t, sin_t, wq_p, w_in_t, w_in_t)


def _wideproj_kernel(h_ref, wa_ref, wb_ref, o_ref, *, tm, bn, n_gelu_steps):
    j = pl.program_id(0)
    h = h_ref[...]

    def run(act):
        for c, w_ref in enumerate((wa_ref, wb_ref)):
            y = _bdot_t(h, w_ref[...])
            for r0 in range(0, tm, 256):
                o_ref[r0:r0 + 256, c * bn:(c + 1) * bn] = act(y[r0:r0 + 256, :]).astype(BF16)

    @pl.when(j < n_gelu_steps)
    def _():
        run(_gelu_tanh)

    @pl.when(j >= n_gelu_steps)
    def _():
        run(lambda y: y)


def _wide_proj(h, w_all_t, *, n_big, n_gelu, tm, bn):
    t, d = h.shape
    blk0 = (w_all_t.shape[0] - n_big) // bn
    return pl.pallas_call(
        functools.partial(_wideproj_kernel, tm=tm, bn=bn, n_gelu_steps=n_gelu // (2 * bn)),
        grid=(n_big // (2 * bn), t // tm),
        in_specs=[
            pl.BlockSpec((tm, d), lambda j, i: (i, 0)),
            pl.BlockSpec((bn, d), lambda j, i: (blk0 + 2 * j, 0)),
            pl.BlockSpec((bn, d), lambda j, i: (blk0 + 2 * j + 1, 0)),
        ],
        out_specs=pl.BlockSpec((tm, 2 * bn), lambda j, i: (i, j)),
        out_shape=jax.ShapeDtypeStruct((t, n_big), BF16),
        compiler_params=_params(2),
        name="wide_proj",
    )(h, w_all_t, w_all_t)


MOD_RING = 3


def _attn_kernel(*refs, n_own, n_cache, heads_per_iter, n_side, n_req=0, mod_col0=0, mod_cols=0):
    refs = list(refs)
    q_ref, ckv_ref, kr_ref = refs[:3]
    del refs[:3]
    if n_cache:
        cckv_ref, ckr_ref = refs[:2]
        del refs[:2]
    wuk_ref, wuv_ref = refs[:2]
    del refs[:2]
    if n_side:
        c_ref, wada_hbm, bada_ref = refs[:3]
        side_in = refs[3:3 + n_side]
        del refs[:3 + n_side]
    o_ref = refs.pop(0)
    if n_side:
        mod_ref = refs.pop(0)
        side_out = refs[:n_side]
        del refs[:n_side]
        wada_buf, wada_sem = refs[-2:]
        del refs[-2:]
        b = pl.program_id(0)

        def wada_copy(step):
            slot = step % MOD_RING
            col = pl.multiple_of(mod_col0 + step * mod_cols, LANES)
            return pltpu.make_async_copy(
                wada_hbm.at[:, pl.ds(col, mod_cols)], wada_buf.at[slot], wada_sem.at[slot])

        @pl.when(b == 0)
        def _():
            wada_copy(0).start()
            wada_copy(1).start()

        @pl.when(b + 2 < n_req)
        def _():
            wada_copy(b + 2).start()

        wada_copy(b).wait()
        w_mod = wada_buf[b % MOD_RING].astype(BF16)
        mod_ref[...] = _bdot(_silu(c_ref[...]).astype(BF16), w_mod) + bada_ref[...]
        for src, dst in zip(side_in, side_out):
            dst[...] = src[...].astype(BF16)
    kpad, vexp, kall, krp, o_scr = refs
    qi = pl.program_id(1)

    @pl.when(qi == 0)
    def _():
        kall[0:n_own, :] = ckv_ref[...].astype(BF16)
        krp[0:n_own, :] = kr_ref[...].astype(BF16)
        if n_cache:
            kall[n_own:n_own + n_cache, :] = cckv_ref[...].astype(BF16)
            krp[n_own:n_own + n_cache, 0:D_ROPE] = ckr_ref[...].astype(BF16)
            krp[n_own:n_own + n_cache, D_ROPE:ROPE_SLOT] = jnp.zeros((n_cache, ROPE_SLOT - D_ROPE), BF16)

        def expand(hp, carry):
            kn = _bdot(kall[...], wuk_ref[hp]).astype(BF16)
            vv = _bdot(kall[...], wuv_ref[hp]).astype(BF16)
            for s in range(2):
                kpad[2 * hp + s, :, 0:D_NOPE] = kn[:, s * D_NOPE:(s + 1) * D_NOPE]
                kpad[2 * hp + s, :, D_NOPE:HEAD_SLOT] = krp[...]
                vexp[2 * hp + s] = vv[:, s * D_V:(s + 1) * D_V]
            return carry

        if heads_per_iter == N_HEADS:
            for hp in range(N_HEADS // 2):
                expand(hp, 0)
        else:
            lax.fori_loop(0, N_HEADS // 2, expand, 0)

    def one_head(h):
        s = _bdot_t(q_ref[h], kpad[h])
        p = jnp.exp(s - jnp.max(s, axis=-1, keepdims=True))
        l = jnp.sum(p, axis=-1, keepdims=True)
        o = _bdot(p.astype(BF16), vexp[h])
        return (o / l).astype(BF16)

    if heads_per_iter == N_HEADS:
        for h in range(N_HEADS):
            o_ref[:, h * D_V:(h + 1) * D_V] = one_head(h)
    else:
        def head_group(hg, carry):
            for g in range(heads_per_iter):
                h = hg * heads_per_iter + g
                o_scr[h] = one_head(h)
            return carry

        lax.fori_loop(0, N_HEADS // heads_per_iter, head_group, 0)
        for h in range(N_HEADS):
            o_ref[:, h * D_V:(h + 1) * D_V] = o_scr[h]


def _attention(q_pad, ckv, kr, w_uk, w_uv, cache_ckv, cache_kr, *, row0, n_req, n_own, tq, heads_per_iter,
               side=None):
    kv_lora = ckv.shape[1]
    n_cache = 0 if cache_ckv is None else cache_ckv.shape[1]
    kn = n_own + n_cache
    qb = n_own // tq
    in_specs = [
        pl.BlockSpec((N_HEADS, tq, HEAD_SLOT), lambda b, qi: (0, row0 // tq + b * qb + qi, 0)),
        pl.BlockSpec((n_own, kv_lora), lambda b, qi: (row0 // n_own + b, 0)),
        pl.BlockSpec((n_own, ROPE_SLOT), lambda b, qi: (row0 // n_own + b, 0)),
    ]
    args = [q_pad, ckv, kr]
    if n_cache:
        in_specs += [
            pl.BlockSpec((None, n_cache, kv_lora), lambda b, qi: (b, 0, 0)),
            pl.BlockSpec((None, n_cache, D_ROPE), lambda b, qi: (b, 0, 0)),
        ]
        args += [cache_ckv, cache_kr]
    in_specs += [_const_spec(w_uk.shape), _const_spec(w_uv.shape)]
    args += [w_uk, w_uv]
    out_specs = [pl.BlockSpec((tq, N_HEADS * D_V), lambda b, qi: (b * qb + qi, 0))]
    out_shape = [jax.ShapeDtypeStruct((n_req * n_own, N_HEADS * D_V), BF16)]
    n_side = 0
    side_scratch, side_params = [], {}
    if side is not None:
        assert qb == 1
        c_rows, w_ada, b_ada, col0, weights = side
        d_model, n_mod = w_ada.shape
        mcols = (n_mod - col0) // n_req
        assert mcols * n_req == n_mod - col0 and mcols % LANES == 0 and col0 % mcols == 0
        assert n_req >= MOD_RING
        in_specs += [
            _const_spec(c_rows.shape),
            pl.BlockSpec(memory_space=pl.ANY),
            pl.BlockSpec((1, mcols), lambda b, qi: (0, col0 // mcols + b)),
        ]
        side_scratch = [pltpu.VMEM((MOD_RING, d_model, mcols), F32), pltpu.SemaphoreType.DMA((MOD_RING,))]
        side_params = dict(n_req=n_req, mod_col0=col0, mod_cols=mcols)
        args += [c_rows, w_ada, b_ada]
        out_specs.append(pl.BlockSpec((MOD_ROWS, mcols), lambda b, qi: (0, b)))
        out_shape.append(jax.ShapeDtypeStruct((MOD_ROWS, n_mod - col0), F32))
        n_side = len(weights)
        assert n_side > 0
        for w in weights:
            rows = w.shape[0] // n_req
            assert rows * n_req == w.shape[0] and rows % 16 == 0
            in_specs.append(pl.BlockSpec((rows, w.shape[1]), lambda b, qi: (b, 0)))
            args.append(w)
            out_specs.append(pl.BlockSpec((rows, w.shape[1]), lambda b, qi: (b, 0)))
            out_shape.append(jax.ShapeDtypeStruct(w.shape, BF16))
    res = pl.pallas_call(
        functools.partial(_attn_kernel, n_own=n_own, n_cache=n_cache, heads_per_iter=heads_per_iter,
                          n_side=n_side, **side_params),
        grid=(n_req, qb),
        in_specs=in_specs,
        out_specs=out_specs,
        out_shape=out_shape,
        scratch_shapes=[
            pltpu.VMEM((N_HEADS, kn, HEAD_SLOT), BF16),
            pltpu.VMEM((N_HEADS, kn, D_V), BF16),
            pltpu.VMEM((kn, kv_lora), BF16),
            pltpu.VMEM((kn, ROPE_SLOT), BF16),
            pltpu.VMEM((N_HEADS, tq, D_V), BF16),
        ] + side_scratch,
        compiler_params=_params(2),
        name="mla_attn_cache" if n_cache else "mla_attn",
    )(*args)
    if side is None:
        return res[0]
    return res[0], res[1], res[2:]


def _oproj_kernel(xc_ref, xl_ref, w_ref, o_ref, *, n_ctx_tiles):
    i = pl.program_id(0)

    @pl.when(i < n_ctx_tiles)
    def _():
        o_ref[...] = _bdot(xc_ref[...], w_ref[...]).astype(o_ref.dtype)

    @pl.when(i >= n_ctx_tiles)
    def _():
        o_ref[...] = _bdot(xl_ref[...], w_ref[...]).astype(o_ref.dtype)


def _o_proj(xc, xl, w, *, tm):
    n_ctx, k = xc.shape
    t = n_ctx + xl.shape[0]
    n = w.shape[1]
    n_ctx_tiles = n_ctx // tm
    return pl.pallas_call(
        functools.partial(_oproj_kernel, n_ctx_tiles=n_ctx_tiles),
        grid=(t // tm,),
        in_specs=[
            pl.BlockSpec((tm, k), lambda i: (jnp.minimum(i, n_ctx_tiles - 1), 0)),
            pl.BlockSpec((tm, k), lambda i: (jnp.maximum(i - n_ctx_tiles, 0), 0)),
            _const_spec(w.shape),
        ],
        out_specs=pl.BlockSpec((tm, n), lambda i: (i, 0)),
        out_shape=jax.ShapeDtypeStruct((t, n), BF16),
        compiler_params=_params(1),
        name="o_proj",
    )(xc, xl, w)


def _sgu_merge_kernel(gu_ref, gv_ref, ga_ref, gb_ref, oa_ref, g_ref, ws_ref, bs_ref, wo_ref, m_ref,
                      z_even, z_odd, *, tm, gc):
    s = pl.program_id(0)

    @pl.when(s == 0)
    def _():
        z_odd[...] = jnp.zeros(z_odd.shape, BF16)

    def body(z_fill, z_drain):
        o_b = _bdot(z_drain[...], wo_ref[...])
        for r0 in range(0, tm, CHUNK):
            rows = slice(r0, r0 + CHUNK)
            oa = oa_ref[rows, :]
            ob = o_b[rows, :].astype(BF16)
            ta = _half_tanh_half(ga_ref[rows, :])
            tb = _half_tanh_half(gb_ref[rows, :])
            m_ref[rows, :] = 0.5 * ((oa + ob) + (ta * oa + tb * ob))

        g_bf = g_ref[...].astype(BF16)
        for r0 in range(0, tm, CHUNK):
            v = gv_ref[r0:r0 + CHUNK, :]
            v32 = v.astype(F32)
            inv = lax.rsqrt(jnp.mean(v32 * v32, axis=-1, keepdims=True) + EPS)
            vn = v * inv.astype(BF16) * g_bf
            for g in range(GM_GROUPS):
                c0 = g * gc
                mix = _bdot(ws_ref[g], vn[:, c0:c0 + gc]) + bs_ref[:, g:g + 1]
                u = gu_ref[r0:r0 + CHUNK, c0:c0 + gc]
                z_fill[r0:r0 + CHUNK, c0:c0 + gc] = u * mix.astype(BF16)

    @pl.when(s % 2 == 0)
    def _():
        body(z_even, z_odd)

    @pl.when(s % 2 == 1)
    def _():
        body(z_odd, z_even)


def _sgu_merge(big, o_a, g_sgu, w_s, b_s_t, w_o_gm, *, tm):
    t = big.shape[0]
    width = g_sgu.shape[1]
    d = w_o_gm.shape[1]
    gc = width // GM_GROUPS
    nt = t // tm

    def fill(s):
        return jnp.minimum(s, nt - 1)

    def drain(s):
        return jnp.maximum(s - 1, 0)

    return pl.pallas_call(
        functools.partial(_sgu_merge_kernel, tm=tm, gc=gc),
        grid=(nt + 1,),
        in_specs=[
            pl.BlockSpec((tm, width), lambda s: (fill(s), 0)),
            pl.BlockSpec((tm, width), lambda s: (fill(s), 1)),
            pl.BlockSpec((tm, d), lambda s: (drain(s), 2)),
            pl.BlockSpec((tm, d), lambda s: (drain(s), 3)),
            pl.BlockSpec((tm, d), lambda s: (drain(s), 0)),
            _const_spec((1, width)),
            _const_spec(w_s.shape),
            _const_spec(b_s_t.shape),
            _const_spec(w_o_gm.shape),
        ],
        out_specs=pl.BlockSpec((tm, d), lambda s: (drain(s), 0)),
        out_shape=jax.ShapeDtypeStruct((t, d), BF16),
        scratch_shapes=[pltpu.VMEM((tm, width), BF16), pltpu.VMEM((tm, width), BF16)],
        compiler_params=_params(1),
        name="sgu_merge",
    )(big, big, big, big, o_a, g_sgu, w_s, b_s_t, w_o_gm)


def _outproj_kernel(m_ref, xc_ref, xl_ref, gate_ref, sc_ref, sh_ref, g_ref, wo_ref, wr_ref,
                    x1_ref, h2_ref, lg_ref, *, tm, n_ctx_tiles):
    i = pl.program_id(0)

    def body(x_ref):
        half = tm // 2
        for b0 in range(0, tm, half):
            r = _bdot(m_ref[b0:b0 + half, :], wo_ref[...])
            for r0 in range(0, half, CHUNK):
                rows = slice(b0 + r0, b0 + r0 + CHUNK)
                x1 = x_ref[rows, :] + gate_ref[...] * r[r0:r0 + CHUNK, :]
                x1_ref[rows, :] = x1
                h2 = (_rms_rows(x1, g_ref[...]) * (1.0 + sc_ref[...]) + sh_ref[...]).astype(BF16)
                h2_ref[rows, :] = h2
                lg_ref[rows, :] = _bdot(h2, wr_ref[...])

    @pl.when(i < n_ctx_tiles)
    def _():
        body(xc_ref)

    @pl.when(i >= n_ctx_tiles)
    def _():
        body(xl_ref)


def _out_proj(merged, xc, xl, mod6, g_ffn, w_out, w_router_p, *, n_lat_seq, tm):
    n_ctx, d = xc.shape
    t = n_ctx + xl.shape[0]
    n_ctx_tiles = n_ctx // tm
    tiles_per_seq = n_lat_seq // tm

    def mod_row(i):
        return jnp.where(i < n_ctx_tiles, MOD_ROWS // 2, (i - n_ctx_tiles) // tiles_per_seq)

    return pl.pallas_call(
        functools.partial(_outproj_kernel, tm=tm, n_ctx_tiles=n_ctx_tiles),
        grid=(t // tm,),
        in_specs=[
            pl.BlockSpec((tm, d), lambda i: (i, 0)),
            pl.BlockSpec((tm, d), lambda i: (jnp.minimum(i, n_ctx_tiles - 1), 0)),
            pl.BlockSpec((tm, d), lambda i: (jnp.maximum(i - n_ctx_tiles, 0), 0)),
            pl.BlockSpec((None, None, 1, d), lambda i: (2, mod_row(i), 0, 0)),
            pl.BlockSpec((None, None, 1, d), lambda i: (4, mod_row(i), 0, 0)),
            pl.BlockSpec((None, None, 1, d), lambda i: (3, mod_row(i), 0, 0)),
            _const_spec((1, d)),
            _const_spec(w_out.shape),
            _const_spec(w_router_p.shape),
        ],
        out_specs=[
            pl.BlockSpec((tm, d), lambda i: (i, 0)),
            pl.BlockSpec((tm, d), lambda i: (i, 0)),
            pl.BlockSpec((tm, LOGIT_LANES), lambda i: (i, 0)),
        ],
        out_shape=[
            jax.ShapeDtypeStruct((t, d), F32),
            jax.ShapeDtypeStruct((t, d), BF16),
            jax.ShapeDtypeStruct((t, LOGIT_LANES), F32),
        ],
        compiler_params=_params(1),
        name="out_proj",
    )(merged, xc, xl, mod6, mod6, mod6, g_ffn, w_out, w_router_p)


BISECT_STEPS = 48
MIN_NORMAL_F32 = float(np.finfo(np.float32).tiny)


def _route_kernel(lg_ref, key_ref, aff_ref, tri_scr, *, n_sets, n, cap):
    for r0 in range(0, n, LANES):
        r = lax.broadcasted_iota(jnp.int32, (LANES, n), 0) + r0
        c = lax.broadcasted_iota(jnp.int32, (LANES, n), 1)
        tri_scr[r0:r0 + LANES, :] = jnp.where(r < c, 1.0, 0.0).astype(BF16)

    for s in range(n_sets):
        logits = lg_ref[s * n:(s + 1) * n, :].T[0:N_EXPERTS, :]
        e = jnp.exp(logits - jnp.max(logits, axis=0, keepdims=True))
        aff_ref[s * N_EXPERTS:(s + 1) * N_EXPERTS, :] = e / jnp.sum(e, axis=0, keepdims=True)
    aff = aff_ref[...]
    rows = n_sets * N_EXPERTS

    def count_ge(thr):
        return jnp.sum(jnp.where(aff >= thr, 1.0, 0.0), axis=1, keepdims=True)

    def bisect(_, carry):
        lo, hi = carry
        mid = jnp.sqrt(lo) * jnp.sqrt(hi)
        ok = count_ge(mid) >= cap
        return jnp.where(ok, mid, lo), jnp.where(ok, hi, mid)

    lo0 = jnp.full((rows, 1), MIN_NORMAL_F32, F32)
    hi0 = jnp.full((rows, 1), 2.0, F32)
    lo, hi = lax.fori_loop(0, BISECT_STEPS, bisect, (lo0, hi0))
    lo = jnp.where(count_ge(lo) >= cap, lo, 0.0)

    above = aff >= hi
    band = (aff >= lo) & jnp.logical_not(above)
    need = cap - jnp.sum(jnp.where(above, 1.0, 0.0), axis=1, keepdims=True)
    tri = tri_scr[...]
    band_before = _bdot(jnp.where(band, 1.0, 0.0).astype(BF16), tri)
    sel = above | (band & (band_before < need))
    pos = _bdot(jnp.where(sel, 1.0, 0.0).astype(BF16), tri)
    key_ref[...] = jnp.where(sel, pos, -1.0)


def _route(logits, *, row0, n_sets, n):
    cap = EC_FACTOR * n // N_EXPERTS
    rows = n_sets * N_EXPERTS
    blk = row0 // (n_sets * n)
    return pl.pallas_call(
        functools.partial(_route_kernel, n_sets=n_sets, n=n, cap=cap),
        grid=(1,),
        in_specs=[pl.BlockSpec((n_sets * n, LOGIT_LANES), lambda g: (blk, 0))],
        out_specs=[pl.BlockSpec((rows, n), lambda g: (0, 0)), pl.BlockSpec((rows, n), lambda g: (0, 0))],
        out_shape=[jax.ShapeDtypeStruct((rows, n), F32), jax.ShapeDtypeStruct((rows, n), F32)],
        scratch_shapes=[pltpu.VMEM((n, n), BF16)],
        compiler_params=_params(1),
        name=f"route_{n}",
    )(logits)


def _gather_kernel(key_ref, aff_ref, h2_ref, pt_ref, gate_ref, xg_ref, p_scr, *, sps, n, cap, d, nb):
    slot = lax.broadcasted_iota(jnp.int32, (cap, n), 0).astype(F32)
    per_group = LANES // cap
    for s in range(sps):
        key = key_ref[s * N_EXPERTS:(s + 1) * N_EXPERTS, :]
        aff = aff_ref[s * N_EXPERTS:(s + 1) * N_EXPERTS, :]
        for grp in range(N_EXPERTS // per_group):
            pieces = []
            for ex in range(grp * per_group, (grp + 1) * per_group):
                hit = slot == key[ex:ex + 1, :]
                gate_ref[ex, s] = jnp.sum(jnp.where(hit, aff[ex:ex + 1, :], 0.0), axis=1, keepdims=True)
                pieces.append(jnp.where(hit, 1.0, 0.0))
            hits = pieces[0] if per_group == 1 else jnp.concatenate(pieces, axis=0)
            rows = slice(grp * LANES, (grp + 1) * LANES)
            p_scr[s, rows, :] = hits.astype(BF16)
            pt_ref[s, :, rows] = hits.T.astype(BF16)

        p = p_scr[s]
        for c in range(d // nb):
            xg = _bdot(p, h2_ref[s * n:(s + 1) * n, c * nb:(c + 1) * nb]).astype(BF16)
            for ex in range(N_EXPERTS):
                xg_ref[ex, s, :, c * nb:(c + 1) * nb] = xg[ex * cap:(ex + 1) * cap, :]


def _dispatch_gather(key, aff, h2, *, row0, n_sets, n, sps):
    d = h2.shape[1]
    cap = EC_FACTOR * n // N_EXPERTS
    slots = N_EXPERTS * cap
    blk0 = row0 // (sps * n)
    return pl.pallas_call(
        functools.partial(_gather_kernel, sps=sps, n=n, cap=cap, d=d, nb=MXU_WIDTH),
        grid=(n_sets // sps,),
        in_specs=[
            pl.BlockSpec((sps * N_EXPERTS, n), lambda b: (b, 0)),
            pl.BlockSpec((sps * N_EXPERTS, n), lambda b: (b, 0)),
            pl.BlockSpec((sps * n, d), lambda b: (blk0 + b, 0)),
        ],
        out_specs=[
            pl.BlockSpec((sps, n, slots), lambda b: (b, 0, 0)),
            pl.BlockSpec((N_EXPERTS, sps, cap, 1), lambda b: (0, b, 0, 0)),
            pl.BlockSpec((N_EXPERTS, sps, cap, d), lambda b: (0, b, 0, 0)),
        ],
        out_shape=[
            jax.ShapeDtypeStruct((n_sets, n, slots), BF16),
            jax.ShapeDtypeStruct((N_EXPERTS, n_sets, cap, 1), F32),
            jax.ShapeDtypeStruct((N_EXPERTS, n_sets, cap, d), BF16),
        ],
        scratch_shapes=[pltpu.VMEM((sps, slots, n), BF16)],
        compiler_params=_params(1),
        name=f"dispatch_gather_{n}",
    )(key, aff, h2)


def _expert_kernel(xc_ref, xl_ref, gc_ref, gl_ref, w1_ref, w3_ref, w2_ref, yc_ref, yl_ref, hc_scr, hl_scr,
                   *, n_f, fc, rows_c, rows_l, d):
    k = pl.program_id(1)

    @pl.when(k < n_f)
    def _():
        w1 = w1_ref[...].astype(BF16)
        w3 = w3_ref[...].astype(BF16)
        for x_ref, h_scr, rows in ((xc_ref, hc_scr, rows_c), (xl_ref, hl_scr, rows_l)):
            x = x_ref[...].reshape(rows, d)
            a = _bdot(x, w1)
            g = _bdot(x, w3)
            h_scr[k] = (_silu(a) * g).astype(BF16)

    @pl.when(k >= n_f)
    def _():
        w2 = w2_ref[...].astype(BF16)
        for h_scr, g_ref, y_ref, rows in ((hc_scr, gc_ref, yc_ref, rows_c), (hl_scr, gl_ref, yl_ref, rows_l)):
            y = _bdot(h_scr[0], w2[0:fc, :])
            for kk in range(1, n_f):
                y = y + _bdot(h_scr[kk], w2[kk * fc:(kk + 1) * fc, :])
            y_ref[...] = (y * g_ref[...].reshape(rows, 1)).astype(BF16).reshape(y_ref.shape)


def _experts(xg_c, xg_l, gate_c, gate_l, w_e1, w_e3, w_e2, *, fc=512, nc=512):
    n_e, sets_c, cap_c, d = xg_c.shape
    _, sets_l, cap_l, _ = xg_l.shape
    d_ff = w_e1.shape[2]
    n_f = d_ff // fc
    n_c = d // nc
    rows_c, rows_l = sets_c * cap_c, sets_l * cap_l

    def e_in(e, k):
        return jnp.minimum(e + (k >= n_f).astype(jnp.int32), n_e - 1)

    def f_idx(k):
        return jnp.where(k < n_f, k, 0)

    def c_idx(k):
        return jnp.maximum(k - n_f, 0)

    kern = functools.partial(_expert_kernel, n_f=n_f, fc=fc, rows_c=rows_c, rows_l=rows_l, d=d)
    return pl.pallas_call(
        kern,
        grid=(n_e, n_f + n_c),
        in_specs=[
            pl.BlockSpec((None, sets_c, cap_c, d), lambda e, k: (e_in(e, k), 0, 0, 0)),
            pl.BlockSpec((None, sets_l, cap_l, d), lambda e, k: (e_in(e, k), 0, 0, 0)),
            pl.BlockSpec((None, sets_c, cap_c, 1), lambda e, k: (e, 0, 0, 0)),
            pl.BlockSpec((None, sets_l, cap_l, 1), lambda e, k: (e, 0, 0, 0)),
            pl.BlockSpec((None, d, fc), lambda e, k: (e_in(e, k), 0, f_idx(k))),
            pl.BlockSpec((None, d, fc), lambda e, k: (e_in(e, k), 0, f_idx(k))),
            pl.BlockSpec((None, d_ff, nc), lambda e, k: (e, 0, c_idx(k))),
        ],
        out_specs=[
            pl.BlockSpec((None, sets_c, cap_c, nc), lambda e, k: (e, 0, 0, c_idx(k))),
            pl.BlockSpec((None, sets_l, cap_l, nc), lambda e, k: (e, 0, 0, c_idx(k))),
        ],
        out_shape=[
            jax.ShapeDtypeStruct(xg_c.shape, BF16),
            jax.ShapeDtypeStruct(xg_l.shape, BF16),
        ],
        scratch_shapes=[pltpu.VMEM((n_f, rows_c, fc), BF16), pltpu.VMEM((n_f, rows_l, fc), BF16)],
        compiler_params=_params(2),
        name="experts",
    )(xg_c, xg_l, gate_c, gate_l, w_e1, w_e3, w_e2)


def _combine_kernel(pt_ref, y_ref, x1_ref, gate_ref, g_ref, o_ref, acc_scr, *, sps, tn, slots, d, nb):
    for s in range(sps):
        pt = pt_ref[s]
        rows = slice(s * tn, (s + 1) * tn)
        for c in range(d // nb):
            cols = slice(c * nb, (c + 1) * nb)
            moe = _bdot(pt, y_ref[:, s, :, cols].reshape(slots, nb))
            acc_scr[:, cols] = x1_ref[rows, cols] + gate_ref[:, cols] * moe
        o_ref[rows, :] = _rms_rows(acc_scr[...], g_ref[...])


def _combine(pt, y, x1, mod6, g_final, *, row0, n_sets, n, sps, mod_row_fn):
    d = x1.shape[1]
    cap = EC_FACTOR * n // N_EXPERTS
    slots = N_EXPERTS * cap
    tn = min(n, 512)
    nt = n // tn
    assert sps == 1 or nt == 1
    blk0 = row0 // (sps * tn)
    return pl.pallas_call(
        functools.partial(_combine_kernel, sps=sps, tn=tn, slots=slots, d=d, nb=512),
        grid=(n_sets // sps, nt),
        in_specs=[
            pl.BlockSpec((sps, tn, slots), lambda b, r: (b, r, 0)),
            pl.BlockSpec((N_EXPERTS, sps, cap, d), lambda b, r: (0, b, 0, 0)),
            pl.BlockSpec((sps * tn, d), lambda b, r: (blk0 + b * nt + r, 0)),
            pl.BlockSpec((None, None, 1, d), lambda b, r: (5, mod_row_fn(b), 0, 0)),
            _const_spec((1, d)),
        ],
        out_specs=pl.BlockSpec((sps * tn, d), lambda b, r: (b * nt + r, 0)),
        out_shape=jax.ShapeDtypeStruct((n_sets * n, d), F32),
        scratch_shapes=[pltpu.VMEM((tn, d), F32)],
        compiler_params=_params(2),
        name=f"combine_{n}",
    )(pt, y, x1, mod6, g_final)


def _rope_tables(n):
    tpos = jnp.arange(n, dtype=jnp.int32)
    row = (tpos // GRID_W).astype(F32)
    col = (tpos % GRID_W).astype(F32)
    inv = 1.0 / (ROPE_BASE ** (jnp.arange(ROPE_FREQS, dtype=F32) / ROPE_FREQS))
    ang = jnp.stack([row[:, None] * inv, col[:, None] * inv], axis=1)
    cos = jnp.cos(ang)[:, :, None, :]
    sin = jnp.sin(ang)[:, :, None, :]
    cos = jnp.broadcast_to(cos, (n, 2, 2, ROPE_FREQS)).reshape(n, D_ROPE)
    sin = jnp.concatenate([-sin, sin], axis=2).reshape(n, D_ROPE)
    pad = jnp.zeros((n, ROPE_SLOT - D_ROPE), F32)
    return jnp.concatenate([cos, pad], axis=1), jnp.concatenate([sin, pad], axis=1)


def kernel(x_prompt, x_sample, c, cache_ckv, cache_krope, c_ctx, g_attn, g_ffn, w_ada, b_ada, w_in, g_qa,
           w_qb, g_kv, w_uk, w_uv, w_o_mla, g_sgu, w_s, b_s, w_o_gm, w_out, w_router, w_e1, w_e3, w_e2,
           g_final):
    batch, seq, d = x_prompt.shape
    dec_batch, dec_seq, _ = x_sample.shape
    depth = g_attn.shape[0]
    assert depth == 1
    q_lora, kv_lora = g_qa.shape[1], g_kv.shape[1]
    gm_width = g_sgu.shape[1]
    n_ctx, n_lat = batch * seq, dec_batch * dec_seq
    assert dec_batch < MOD_ROWS // 2 + 1

    xc = x_prompt.reshape(n_ctx, d)
    xl = x_sample.reshape(n_lat, d)

    c_rows = jnp.zeros((MOD_ROWS, d), F32).at[:dec_batch].set(c).at[MOD_ROWS // 2].set(c_ctx)
    n_mod_early = 2 * d
    mod_a = _modulation(c_rows, w_ada[0], b_ada[0][None, :], n_cols=n_mod_early)
    mod6_a = mod_a.reshape(MOD_ROWS, 2, 1, d).transpose(1, 0, 2, 3)

    cos_t, sin_t = _rope_tables(dec_seq)

    w_in_t = w_in[0].T
    o_kr = q_lora + kv_lora
    w_small_t = jnp.pad(w_in_t[:o_kr + D_ROPE], ((0, ROPE_SLOT - D_ROPE), (0, 0))).astype(BF16)
    tm = ROW_TILE
    front = -(o_kr + D_ROPE) % WIDE_WEIGHT_BLOCK

    wq3 = w_qb[0].reshape(q_lora, N_HEADS, D_NOPE + D_ROPE)
    wq_nope, wq_rope = wq3[:, :, :D_NOPE], wq3[:, :, D_NOPE:]
    zq = jnp.zeros((q_lora, N_HEADS, HEAD_SLOT - D_NOPE - D_ROPE), F32)
    wq_p = jnp.concatenate([wq_nope, wq_rope, zq], axis=2).reshape(q_lora, N_HEADS * HEAD_SLOT).astype(BF16)

    h, ckv, kr, q_pad, w_all_t, ckv_ctx, kr_ctx = _pre_proj(
        xc, xl, g_attn, mod6_a, w_small_t, g_qa, g_kv, cos_t, sin_t, wq_p, w_in_t, front=front, tm=tm)
    big = _wide_proj(h, w_all_t, n_big=w_in_t.shape[0] - o_kr - D_ROPE, n_gelu=2 * gm_width,
                     tm=MATMUL_ROW_TILE, bn=WIDE_WEIGHT_BLOCK)

    w_uk2 = w_uk[0].reshape(kv_lora, N_HEADS // 2, 2 * D_NOPE).transpose(1, 0, 2).astype(BF16)
    w_uv2 = w_uv[0].reshape(kv_lora, N_HEADS // 2, 2 * D_V).transpose(1, 0, 2).astype(BF16)
    side = (c_rows, w_ada[0], b_ada[0][None, :], n_mod_early, (w_o_mla[0], w_o_gm[0], w_out[0]))
    o_ctx, mod_b, (w_o_mla_b, w_o_gm_b, w_out_b) = _attention(
        q_pad, ckv, kr, w_uk2, w_uv2, None, None,
        row0=0, n_req=batch, n_own=seq, tq=seq, heads_per_iter=N_HEADS, side=side)
    o_lat = _attention(q_pad, ckv, kr, w_uk2, w_uv2, cache_ckv[:, 0], cache_krope[:, 0],
                       row0=n_ctx, n_req=dec_batch, n_own=dec_seq, tq=ATTN_Q_TILE,
                       heads_per_iter=ATTN_HEADS_PER_STEP)
    mod6 = jnp.concatenate([mod6_a, mod_b.reshape(MOD_ROWS, 4, 1, d).transpose(1, 0, 2, 3)], axis=0)

    o_a = _o_proj(o_ctx, o_lat, w_o_mla_b, tm=MATMUL_ROW_TILE)
    merged = _sgu_merge(big, o_a, g_sgu, w_s[0].astype(BF16), b_s[0].T, w_o_gm_b, tm=tm)

    w_router_p = jnp.concatenate(
        [w_router[0], jnp.zeros((d, LOGIT_LANES - N_EXPERTS), F32)], axis=1).astype(BF16)
    x1, h2, logits = _out_proj(merged, xc, xl, mod6, g_ffn, w_out_b, w_router_p,
                               n_lat_seq=dec_seq, tm=tm)

    key_c, aff_c = _route(logits, row0=0, n_sets=batch, n=seq)
    key_l, aff_l = _route(logits, row0=n_ctx, n_sets=dec_batch, n=dec_seq)
    p_c, gate_c, xg_c = _dispatch_gather(key_c, aff_c, h2, row0=0, n_sets=batch, n=seq,
                                         sps=CTX_SETS_PER_GATHER_STEP)
    p_l, gate_l, xg_l = _dispatch_gather(key_l, aff_l, h2, row0=n_ctx, n_sets=dec_batch, n=dec_seq, sps=1)

    y_c, y_l = _experts(xg_c, xg_l, gate_c, gate_l, w_e1[0], w_e3[0], w_e2[0])

    y_prompt = _combine(p_c, y_c, x1, mod6, g_final[None, :], row0=0, n_sets=batch, n=seq,
                        sps=CTX_SETS_PER_COMBINE_STEP, mod_row_fn=lambda b: MOD_ROWS // 2)
    y_sample = _combine(p_l, y_l, x1, mod6, g_final[None, :], row0=n_ctx, n_sets=dec_batch, n=dec_seq, sps=1,
                        mod_row_fn=lambda b: b)

    new_ckv = ckv_ctx.reshape(batch, 1, seq, kv_lora)
    new_krope = kr_ctx.reshape(batch, 1, seq, D_ROPE)
    return (y_prompt.reshape(batch, seq, d), y_sample.reshape(dec_batch, dec_seq, d), new_ckv, new_krope)
```

```python
import functools

import numpy as np
import jax
import jax.numpy as jnp
from jax import lax
from jax.experimental import pallas as pl
from jax.experimental.pallas import tpu as pltpu

F32 = jnp.float32
BF16 = jnp.bfloat16

LANES = 128
MXU_WIDTH = 256

N_HEADS = 16
D_NOPE = 128
D_ROPE = 64
D_V = 128
ROPE_SLOT = LANES
HEAD_SLOT = D_NOPE + ROPE_SLOT
ROPE_FREQS = D_ROPE // 4
ROPE_BASE = 10000.0
GRID_W = 64
CHUNK = 128
GM_GROUPS = 8
N_EXPERTS = 16
EC_FACTOR = 2
EPS = 1e-6
MOD_ROWS = 8
LOGIT_LANES = LANES

VMEM_LIMIT_V7X = 56 * 1024 * 1024

ROW_TILE = 512
MATMUL_ROW_TILE = 1024
WIDE_WEIGHT_BLOCK = 1024
ATTN_Q_TILE = 512
ATTN_HEADS_PER_STEP = 4
CTX_SETS_PER_GATHER_STEP = 4
CTX_SETS_PER_COMBINE_STEP = 2


def _params(n_axes):
    return pltpu.CompilerParams(
        dimension_semantics=("arbitrary",) * n_axes, vmem_limit_bytes=VMEM_LIMIT_V7X)


def _const_spec(shape):
    nd = len(shape)
    return pl.BlockSpec(shape, lambda *_: (0,) * nd, pipeline_mode=pl.Buffered(1))


def _half_tanh_half(x):
    return jnp.tanh(0.5 * x)


def _silu(x):
    h = 0.5 * x
    return h * jnp.tanh(h) + h


def _gelu_tanh(x):
    assert x.dtype == F32
    c = float(np.sqrt(2.0 / np.pi))
    return x * (0.5 * (1.0 + jnp.tanh(c * (x + 0.044715 * (x * x * x)))))


def _rms_rows(x, g):
    return x * lax.rsqrt(jnp.mean(x * x, axis=-1, keepdims=True) + EPS) * g


def _bdot(a, b):
    return jnp.dot(a, b, preferred_element_type=F32)


def _bdot_t(a, bt):
    return lax.dot_general(a, bt, (((1,), (1,)), ((), ())), preferred_element_type=F32)


def _mod_kernel(c_ref, w_ref, b_ref, o_ref):
    c = c_ref[...]
    s = _silu(c).astype(BF16)
    o_ref[...] = _bdot(s, w_ref[...].astype(BF16)) + b_ref[...]


def _modulation(c_rows, w_ada, b_ada, *, n_cols, bn=1024):
    d, n = w_ada.shape[0], n_cols
    return pl.pallas_call(
        _mod_kernel,
        grid=(n // bn,),
        in_specs=[
            _const_spec((MOD_ROWS, d)),
            pl.BlockSpec((d, bn), lambda j: (0, j)),
            pl.BlockSpec((1, bn), lambda j: (0, j)),
        ],
        out_specs=pl.BlockSpec((MOD_ROWS, bn), lambda j: (0, j)),
        out_shape=jax.ShapeDtypeStruct((MOD_ROWS, n), F32),
        compiler_params=_params(1),
        name="adaln_mod",
    )(c_rows, w_ada, b_ada)


X_RING = 3


def _preproj_kernel(xc_hbm, xl_hbm, g_ref, sc_ref, sh_ref, ws_ref, gqa_ref, gkv_ref, cos_ref, sin_ref,
                    wq_ref, wlo_ref, whi_ref, h_ref, ckv_ref, kr_ref, q_ref, wcast_ref,
                    new_ckv_ref, new_kr_ref, x_buf, x_sem, *, n_tiles, n_ctx_tiles, q_lora, kv_lora, scale):
    i = pl.program_id(0)
    tm = x_buf.shape[1]

    def x_copy(t, src_hbm, row0):
        slot = t % X_RING
        rows = pl.ds(pl.multiple_of((t - row0) * tm, tm), tm)
        return pltpu.make_async_copy(src_hbm.at[rows, :], x_buf.at[slot], x_sem.at[slot])

    def for_tile(t, fn):
        @pl.when(t < n_ctx_tiles)
        def _():
            fn(x_copy(t, xc_hbm, 0))

        @pl.when(t >= n_ctx_tiles)
        def _():
            fn(x_copy(t, xl_hbm, n_ctx_tiles))

    @pl.when(i == 0)
    def _():
        for_tile(i, lambda cp: cp.start())
        for_tile(i + 1, lambda cp: cp.start())

    @pl.when(i + 2 < n_tiles)
    def _():
        for_tile(i + 2, lambda cp: cp.start())

    for_tile(i, lambda cp: cp.wait())
    x_ref = x_buf.at[i % X_RING]

    rb = wlo_ref.shape[0]
    lo = wlo_ref[...].astype(BF16)
    wcast_ref[0:rb, :] = jnp.where(i == 0, jnp.zeros_like(lo), lo)
    wcast_ref[rb:2 * rb, :] = whi_ref[...].astype(BF16)

    def body(x_ref, rotary):
        h = _rms_rows(x_ref[...], g_ref[...]) * (1.0 + sc_ref[...]) + sh_ref[...]
        hb = h.astype(BF16)
        h_ref[...] = hb
        small = _bdot_t(hb, ws_ref[...])
        qa = _rms_rows(small[:, :q_lora], gqa_ref[...]).astype(BF16)
        ckv = _rms_rows(small[:, q_lora:q_lora + kv_lora], gkv_ref[...])
        ckv_ref[...] = ckv
        o = q_lora + kv_lora
        kr = small[:, o:o + ROPE_SLOT]
        if not rotary:
            kr_ref[...] = kr
            new_ckv_ref[...] = ckv
            new_kr_ref[...] = kr[:, :D_ROPE]
            for hd in range(N_HEADS):
                q = _bdot(qa, wq_ref[:, hd * HEAD_SLOT:(hd + 1) * HEAD_SLOT])
                q_ref[hd] = (q * scale).astype(BF16)
            return
        cos = cos_ref[...]
        sin = sin_ref[...]
        lane = lax.broadcasted_iota(jnp.int32, kr.shape, 1)
        first_half = (lane & (2 * ROPE_FREQS - 1)) < ROPE_FREQS

        def rotate(r):
            partner = jnp.where(first_half, pltpu.roll(r, ROPE_SLOT - ROPE_FREQS, 1), pltpu.roll(r, ROPE_FREQS, 1))
            return r * cos + partner * sin

        kr_ref[...] = rotate(kr)
        for hd in range(N_HEADS):
            q = _bdot(qa, wq_ref[:, hd * HEAD_SLOT:(hd + 1) * HEAD_SLOT])
            q_ref[hd, :, 0:D_NOPE] = (q[:, 0:D_NOPE] * scale).astype(BF16)
            q_ref[hd, :, D_NOPE:HEAD_SLOT] = (rotate(q[:, D_NOPE:HEAD_SLOT]) * scale).astype(BF16)

    @pl.when(i < n_ctx_tiles)
    def _():
        body(x_ref, False)

    @pl.when(i >= n_ctx_tiles)
    def _():
        body(x_ref, True)


def _pre_proj(xc, xl, g_attn, mod6, w_small_t, g_qa, g_kv, cos_t, sin_t, wq_p, w_in_t, *, front, tm):
    n_ctx, d = xc.shape
    n_lat_seq = cos_t.shape[0]
    t = n_ctx + xl.shape[0]
    q_lora, kv_lora = g_qa.shape[1], g_kv.shape[1]
    n_ctx_tiles = n_ctx // tm
    tiles_per_seq = n_lat_seq // tm
    n_wblk = w_in_t.shape[0] // front
    assert w_in_t.shape[0] == n_wblk * front and front % 16 == 0
    assert n_wblk + 1 == 2 * (t // tm), "weight row blocks must pair up with the token steps"

    def mod_row(i):
        return jnp.where(i < n_ctx_tiles, MOD_ROWS // 2, (i - n_ctx_tiles) // tiles_per_seq)

    def rope_blk(i):
        return jnp.maximum(i - n_ctx_tiles, 0) % tiles_per_seq

    scale = np.float32((D_NOPE + D_ROPE) ** -0.5)
    assert t // tm >= X_RING
    kern = functools.partial(_preproj_kernel, n_tiles=t // tm, n_ctx_tiles=n_ctx_tiles, q_lora=q_lora,
                             kv_lora=kv_lora, scale=scale)
    return pl.pallas_call(
        kern,
        grid=(t // tm,),
        in_specs=[
            pl.BlockSpec(memory_space=pl.ANY),
            pl.BlockSpec(memory_space=pl.ANY),
            _const_spec((1, d)),
            pl.BlockSpec((None, None, 1, d), lambda i: (1, mod_row(i), 0, 0)),
            pl.BlockSpec((None, None, 1, d), lambda i: (0, mod_row(i), 0, 0)),
            _const_spec(w_small_t.shape),
            _const_spec((1, q_lora)),
            _const_spec((1, kv_lora)),
            pl.BlockSpec((tm, ROPE_SLOT), lambda i:(rope_blk(i), 0)),
            pl.BlockSpec((tm, ROPE_SLOT), lambda i:(rope_blk(i), 0)),
            _const_spec(wq_p.shape),
            pl.BlockSpec((front, d), lambda i: (jnp.maximum(2 * i - 1, 0), 0)),
            pl.BlockSpec((front, d), lambda i: (2 * i, 0)),
        ],
        out_specs=[
            pl.BlockSpec((tm, d), lambda i: (i, 0)),
            pl.BlockSpec((tm, kv_lora), lambda i: (i, 0)),
            pl.BlockSpec((tm, ROPE_SLOT), lambda i:(i, 0)),
            pl.BlockSpec((N_HEADS, tm, HEAD_SLOT), lambda i: (0, i, 0)),
            pl.BlockSpec((2 * front, d), lambda i: (i, 0)),
            pl.BlockSpec((tm, kv_lora), lambda i: (jnp.minimum(i, n_ctx_tiles - 1), 0)),
            pl.BlockSpec((tm, D_ROPE), lambda i: (jnp.minimum(i, n_ctx_tiles - 1), 0)),
        ],
        out_shape=[
            jax.ShapeDtypeStruct((t, d), BF16),
            jax.ShapeDtypeStruct((t, kv_lora), F32),
            jax.ShapeDtypeStruct((t, ROPE_SLOT), F32),
            jax.ShapeDtypeStruct((N_HEADS, t, HEAD_SLOT), BF16),
            jax.ShapeDtypeStruct(((n_wblk + 1) * front, d), BF16),
            jax.ShapeDtypeStruct((n_ctx, kv_lora), F32),
            jax.ShapeDtypeStruct((n_ctx, D_ROPE), F32),
        ],
        scratch_shapes=[pltpu.VMEM((X_RING, tm, d), F32), pltpu.SemaphoreType.DMA((X_RING,))],
        compiler_params=_params(1),
        name="pre_proj",
    )(xc, xl, g_attn, mod6, mod6, w_small_t, g_qa, g_kv, cos_t, sin_t, wq_p, w_in_t, w_in_t)


def _wideproj_kernel(h_ref, wa_ref, wb_ref, o_ref, *, tm, bn, n_gelu_steps):
    j = pl.program_id(0)
    h = h_ref[...]

    def run(act):
        for c, w_ref in enumerate((wa_ref, wb_ref)):
            y = _bdot_t(h, w_ref[...])
            for r0 in range(0, tm, 256):
                o_ref[r0:r0 + 256, c * bn:(c + 1) * bn] = act(y[r0:r0 + 256, :]).astype(BF16)

    @pl.when(j < n_gelu_steps)
    def _():
        run(_gelu_tanh)

    @pl.when(j >= n_gelu_steps)
    def _():
        run(lambda y: y)


def _wide_proj(h, w_all_t, *, n_big, n_gelu, tm, bn):
    t, d = h.shape
    blk0 = (w_all_t.shape[0] - n_big) // bn
    return pl.pallas_call(
        functools.partial(_wideproj_kernel, tm=tm, bn=bn, n_gelu_steps=n_gelu // (2 * bn)),
        grid=(n_big // (2 * bn), t // tm),
        in_specs=[
            pl.BlockSpec((tm, d), lambda j, i: (i, 0)),
            pl.BlockSpec((bn, d), lambda j, i: (blk0 + 2 * j, 0)),
            pl.BlockSpec((bn, d), lambda j, i: (blk0 + 2 * j + 1, 0)),
        ],
        out_specs=pl.BlockSpec((tm, 2 * bn), lambda j, i: (i, j)),
        out_shape=jax.ShapeDtypeStruct((t, n_big), BF16),
        compiler_params=_params(2),
        name="wide_proj",
    )(h, w_all_t, w_all_t)


MOD_RING = 3


def _attn_kernel(*refs, n_own, n_cache, heads_per_iter, n_side, n_req=0, mod_col0=0, mod_cols=0):
    refs = list(refs)
    q_ref, ckv_ref, kr_ref = refs[:3]
    del refs[:3]
    if n_cache:
        cckv_ref, ckr_ref = refs[:2]
        del refs[:2]
    wuk_ref, wuv_ref = refs[:2]
    del refs[:2]
    if n_side:
        c_ref, wada_hbm, bada_ref = refs[:3]
        side_in = refs[3:3 + n_side]
        del refs[:3 + n_side]
    o_ref = refs.pop(0)
    if n_side:
        mod_ref = refs.pop(0)
        side_out = refs[:n_side]
        del refs[:n_side]
        wada_buf, wada_sem = refs[-2:]
        del refs[-2:]
        b = pl.program_id(0)

        def wada_copy(step):
            slot = step % MOD_RING
            col = pl.multiple_of(mod_col0 + step * mod_cols, LANES)
            return pltpu.make_async_copy(
                wada_hbm.at[:, pl.ds(col, mod_cols)], wada_buf.at[slot], wada_sem.at[slot])

        @pl.when(b == 0)
        def _():
            wada_copy(0).start()
            wada_copy(1).start()

        @pl.when(b + 2 < n_req)
        def _():
            wada_copy(b + 2).start()

        wada_copy(b).wait()
        w_mod = wada_buf[b % MOD_RING].astype(BF16)
        mod_ref[...] = _bdot(_silu(c_ref[...]).astype(BF16), w_mod) + bada_ref[...]
        for src, dst in zip(side_in, side_out):
            dst[...] = src[...].astype(BF16)
    kpad, vexp, kall, krp, o_scr = refs
    qi = pl.program_id(1)

    @pl.when(qi == 0)
    def _():
        kall[0:n_own, :] = ckv_ref[...].astype(BF16)
        krp[0:n_own, :] = kr_ref[...].astype(BF16)
        if n_cache:
            kall[n_own:n_own + n_cache, :] = cckv_ref[...].astype(BF16)
            krp[n_own:n_own + n_cache, 0:D_ROPE] = ckr_ref[...].astype(BF16)
            krp[n_own:n_own + n_cache, D_ROPE:ROPE_SLOT] = jnp.zeros((n_cache, ROPE_SLOT - D_ROPE), BF16)

        def expand(hp, carry):
            kn = _bdot(kall[...], wuk_ref[hp]).astype(BF16)
            vv = _bdot(kall[...], wuv_ref[hp]).astype(BF16)
            for s in range(2):
                kpad[2 * hp + s, :, 0:D_NOPE] = kn[:, s * D_NOPE:(s + 1) * D_NOPE]
                kpad[2 * hp + s, :, D_NOPE:HEAD_SLOT] = krp[...]
                vexp[2 * hp + s] = vv[:, s * D_V:(s + 1) * D_V]
            return carry

        if heads_per_iter == N_HEADS:
            for hp in range(N_HEADS // 2):
                expand(hp, 0)
        else:
            lax.fori_loop(0, N_HEADS // 2, expand, 0)

    def one_head(h):
        s = _bdot_t(q_ref[h], kpad[h])
        p = jnp.exp(s - jnp.max(s, axis=-1, keepdims=True))
        l = jnp.sum(p, axis=-1, keepdims=True)
        o = _bdot(p.astype(BF16), vexp[h])
        return (o / l).astype(BF16)

    if heads_per_iter == N_HEADS:
        for h in range(N_HEADS):
            o_ref[:, h * D_V:(h + 1) * D_V] = one_head(h)
    else:
        def head_group(hg, carry):
            for g in range(heads_per_iter):
                h = hg * heads_per_iter + g
                o_scr[h] = one_head(h)
            return carry

        lax.fori_loop(0, N_HEADS // heads_per_iter, head_group, 0)
        for h in range(N_HEADS):
            o_ref[:, h * D_V:(h + 1) * D_V] = o_scr[h]


def _attention(q_pad, ckv, kr, w_uk, w_uv, cache_ckv, cache_kr, *, row0, n_req, n_own, tq, heads_per_iter,
               side=None):
    kv_lora = ckv.shape[1]
    n_cache = 0 if cache_ckv is None else cache_ckv.shape[1]
    kn = n_own + n_cache
    qb = n_own // tq
    in_specs = [
        pl.BlockSpec((N_HEADS, tq, HEAD_SLOT), lambda b, qi: (0, row0 // tq + b * qb + qi, 0)),
        pl.BlockSpec((n_own, kv_lora), lambda b, qi: (row0 // n_own + b, 0)),
        pl.BlockSpec((n_own, ROPE_SLOT), lambda b, qi: (row0 // n_own + b, 0)),
    ]
    args = [q_pad, ckv, kr]
    if n_cache:
        in_specs += [
            pl.BlockSpec((None, n_cache, kv_lora), lambda b, qi: (b, 0, 0)),
            pl.BlockSpec((None, n_cache, D_ROPE), lambda b, qi: (b, 0, 0)),
        ]
        args += [cache_ckv, cache_kr]
    in_specs += [_const_spec(w_uk.shape), _const_spec(w_uv.shape)]
    args += [w_uk, w_uv]
    out_specs = [pl.BlockSpec((tq, N_HEADS * D_V), lambda b, qi: (b * qb + qi, 0))]
    out_shape = [jax.ShapeDtypeStruct((n_req * n_own, N_HEADS * D_V), BF16)]
    n_side = 0
    side_scratch, side_params = [], {}
    if side is not None:
        assert qb == 1
        c_rows, w_ada, b_ada, col0, weights = side
        d_model, n_mod = w_ada.shape
        mcols = (n_mod - col0) // n_req
        assert mcols * n_req == n_mod - col0 and mcols % LANES == 0 and col0 % mcols == 0
        assert n_req >= MOD_RING
        in_specs += [
            _const_spec(c_rows.shape),
            pl.BlockSpec(memory_space=pl.ANY),
            pl.BlockSpec((1, mcols), lambda b, qi: (0, col0 // mcols + b)),
        ]
        side_scratch = [pltpu.VMEM((MOD_RING, d_model, mcols), F32), pltpu.SemaphoreType.DMA((MOD_RING,))]
        side_params = dict(n_req=n_req, mod_col0=col0, mod_cols=mcols)
        args += [c_rows, w_ada, b_ada]
        out_specs.append(pl.BlockSpec((MOD_ROWS, mcols), lambda b, qi: (0, b)))
        out_shape.append(jax.ShapeDtypeStruct((MOD_ROWS, n_mod - col0), F32))
        n_side = len(weights)
        assert n_side > 0
        for w in weights:
            rows = w.shape[0] // n_req
            assert rows * n_req == w.shape[0] and rows % 16 == 0
            in_specs.append(pl.BlockSpec((rows, w.shape[1]), lambda b, qi: (b, 0)))
            args.append(w)
            out_specs.append(pl.BlockSpec((rows, w.shape[1]), lambda b, qi: (b, 0)))
            out_shape.append(jax.ShapeDtypeStruct(w.shape, BF16))
    res = pl.pallas_call(
        functools.partial(_attn_kernel, n_own=n_own, n_cache=n_cache, heads_per_iter=heads_per_iter,
                          n_side=n_side, **side_params),
        grid=(n_req, qb),
        in_specs=in_specs,
        out_specs=out_specs,
        out_shape=out_shape,
        scratch_shapes=[
            pltpu.VMEM((N_HEADS, kn, HEAD_SLOT), BF16),
            pltpu.VMEM((N_HEADS, kn, D_V), BF16),
            pltpu.VMEM((kn, kv_lora), BF16),
            pltpu.VMEM((kn, ROPE_SLOT), BF16),
            pltpu.VMEM((N_HEADS, tq, D_V), BF16),
        ] + side_scratch,
        compiler_params=_params(2),
        name="mla_attn_cache" if n_cache else "mla_attn",
    )(*args)
    if side is None:
        return res[0]
    return res[0], res[1], res[2:]


def _oproj_kernel(xc_ref, xl_ref, w_ref, o_ref, *, n_ctx_tiles):
    i = pl.program_id(0)

    @pl.when(i < n_ctx_tiles)
    def _():
        o_ref[...] = _bdot(xc_ref[...], w_ref[...]).astype(o_ref.dtype)

    @pl.when(i >= n_ctx_tiles)
    def _():
        o_ref[...] = _bdot(xl_ref[...], w_ref[...]).astype(o_ref.dtype)


def _o_proj(xc, xl, w, *, tm):
    n_ctx, k = xc.shape
    t = n_ctx + xl.shape[0]
    n = w.shape[1]
    n_ctx_tiles = n_ctx // tm
    return pl.pallas_call(
        functools.partial(_oproj_kernel, n_ctx_tiles=n_ctx_tiles),
        grid=(t // tm,),
        in_specs=[
            pl.BlockSpec((tm, k), lambda i: (jnp.minimum(i, n_ctx_tiles - 1), 0)),
            pl.BlockSpec((tm, k), lambda i: (jnp.maximum(i - n_ctx_tiles, 0), 0)),
            _const_spec(w.shape),
        ],
        out_specs=pl.BlockSpec((tm, n), lambda i: (i, 0)),
        out_shape=jax.ShapeDtypeStruct((t, n), BF16),
        compiler_params=_params(1),
        name="o_proj",
    )(xc, xl, w)


def _sgu_merge_kernel(gu_ref, gv_ref, ga_ref, gb_ref, oa_ref, g_ref, ws_ref, bs_ref, wo_ref, m_ref,
                      z_even, z_odd, *, tm, gc):
    s = pl.program_id(0)

    @pl.when(s == 0)
    def _():
        z_odd[...] = jnp.zeros(z_odd.shape, BF16)

    def body(z_fill, z_drain):
        o_b = _bdot(z_drain[...], wo_ref[...])
        for r0 in range(0, tm, CHUNK):
            rows = slice(r0, r0 + CHUNK)
            oa = oa_ref[rows, :]
            ob = o_b[rows, :].astype(BF16)
            ta = _half_tanh_half(ga_ref[rows, :])
            tb = _half_tanh_half(gb_ref[rows, :])
            m_ref[rows, :] = 0.5 * ((oa + ob) + (ta * oa + tb * ob))

        g_bf = g_ref[...].astype(BF16)
        for r0 in range(0, tm, CHUNK):
            v = gv_ref[r0:r0 + CHUNK, :]
            v32 = v.astype(F32)
            inv = lax.rsqrt(jnp.mean(v32 * v32, axis=-1, keepdims=True) + EPS)
            vn = v * inv.astype(BF16) * g_bf
            for g in range(GM_GROUPS):
                c0 = g * gc
                mix = _bdot(ws_ref[g], vn[:, c0:c0 + gc]) + bs_ref[:, g:g + 1]
                u = gu_ref[r0:r0 + CHUNK, c0:c0 + gc]
                z_fill[r0:r0 + CHUNK, c0:c0 + gc] = u * mix.astype(BF16)

    @pl.when(s % 2 == 0)
    def _():
        body(z_even, z_odd)

    @pl.when(s % 2 == 1)
    def _():
        body(z_odd, z_even)


def _sgu_merge(big, o_a, g_sgu, w_s, b_s_t, w_o_gm, *, tm):
    t = big.shape[0]
    width = g_sgu.shape[1]
    d = w_o_gm.shape[1]
    gc = width // GM_GROUPS
    nt = t // tm

    def fill(s):
        return jnp.minimum(s, nt - 1)

    def drain(s):
        return jnp.maximum(s - 1, 0)

    return pl.pallas_call(
        functools.partial(_sgu_merge_kernel, tm=tm, gc=gc),
        grid=(nt + 1,),
        in_specs=[
            pl.BlockSpec((tm, width), lambda s: (fill(s), 0)),
            pl.BlockSpec((tm, width), lambda s: (fill(s), 1)),
            pl.BlockSpec((tm, d), lambda s: (drain(s), 2)),
            pl.BlockSpec((tm, d), lambda s: (drain(s), 3)),
            pl.BlockSpec((tm, d), lambda s: (drain(s), 0)),
            _const_spec((1, width)),
            _const_spec(w_s.shape),
            _const_spec(b_s_t.shape),
            _const_spec(w_o_gm.shape),
        ],
        out_specs=pl.BlockSpec((tm, d), lambda s: (drain(s), 0)),
        out_shape=jax.ShapeDtypeStruct((t, d), BF16),
        scratch_shapes=[pltpu.VMEM((tm, width), BF16), pltpu.VMEM((tm, width), BF16)],
        compiler_params=_params(1),
        name="sgu_merge",
    )(big, big, big, big, o_a, g_sgu, w_s, b_s_t, w_o_gm)


def _outproj_kernel(m_ref, xc_ref, xl_ref, gate_ref, sc_ref, sh_ref, g_ref, wo_ref, wr_ref,
                    x1_ref, h2_ref, lg_ref, *, tm, n_ctx_tiles):
    i = pl.program_id(0)

    def body(x_ref):
        half = tm // 2
        for b0 in range(0, tm, half):
            r = _bdot(m_ref[b0:b0 + half, :], wo_ref[...])
            for r0 in range(0, half, CHUNK):
                rows = slice(b0 + r0, b0 + r0 + CHUNK)
                x1 = x_ref[rows, :] + gate_ref[...] * r[r0:r0 + CHUNK, :]
                x1_ref[rows, :] = x1
                h2 = (_rms_rows(x1, g_ref[...]) * (1.0 + sc_ref[...]) + sh_ref[...]).astype(BF16)
                h2_ref[rows, :] = h2
                lg_ref[rows, :] = _bdot(h2, wr_ref[...])

    @pl.when(i < n_ctx_tiles)
    def _():
        body(xc_ref)

    @pl.when(i >= n_ctx_tiles)
    def _():
        body(xl_ref)


def _out_proj(merged, xc, xl, mod6, g_ffn, w_out, w_router_p, *, n_lat_seq, tm):
    n_ctx, d = xc.shape
    t = n_ctx + xl.shape[0]
    n_ctx_tiles = n_ctx // tm
    tiles_per_seq = n_lat_seq // tm

    def mod_row(i):
        return jnp.where(i < n_ctx_tiles, MOD_ROWS // 2, (i - n_ctx_tiles) // tiles_per_seq)

    return pl.pallas_call(
        functools.partial(_outproj_kernel, tm=tm, n_ctx_tiles=n_ctx_tiles),
        grid=(t // tm,),
        in_specs=[
            pl.BlockSpec((tm, d), lambda i: (i, 0)),
            pl.BlockSpec((tm, d), lambda i: (jnp.minimum(i, n_ctx_tiles - 1), 0)),
            pl.BlockSpec((tm, d), lambda i: (jnp.maximum(i - n_ctx_tiles, 0), 0)),
            pl.BlockSpec((None, None, 1, d), lambda i: (2, mod_row(i), 0, 0)),
            pl.BlockSpec((None, None, 1, d), lambda i: (4, mod_row(i), 0, 0)),
            pl.BlockSpec((None, None, 1, d), lambda i: (3, mod_row(i), 0, 0)),
            _const_spec((1, d)),
            _const_spec(w_out.shape),
            _const_spec(w_router_p.shape),
        ],
        out_specs=[
            pl.BlockSpec((tm, d), lambda i: (i, 0)),
            pl.BlockSpec((tm, d), lambda i: (i, 0)),
            pl.BlockSpec((tm, LOGIT_LANES), lambda i: (i, 0)),
        ],
        out_shape=[
            jax.ShapeDtypeStruct((t, d), F32),
            jax.ShapeDtypeStruct((t, d), BF16),
            jax.ShapeDtypeStruct((t, LOGIT_LANES), F32),
        ],
        compiler_params=_params(1),
        name="out_proj",
    )(merged, xc, xl, mod6, mod6, mod6, g_ffn, w_out, w_router_p)


BISECT_STEPS = 48
MIN_NORMAL_F32 = float(np.finfo(np.float32).tiny)


def _route_kernel(lg_ref, key_ref, aff_ref, tri_scr, *, n_sets, n, cap):
    for r0 in range(0, n, LANES):
        r = lax.broadcasted_iota(jnp.int32, (LANES, n), 0) + r0
        c = lax.broadcasted_iota(jnp.int32, (LANES, n), 1)
        tri_scr[r0:r0 + LANES, :] = jnp.where(r < c, 1.0, 0.0).astype(BF16)

    for s in range(n_sets):
        logits = lg_ref[s * n:(s + 1) * n, :].T[0:N_EXPERTS, :]
        e = jnp.exp(logits - jnp.max(logits, axis=0, keepdims=True))
        aff_ref[s * N_EXPERTS:(s + 1) * N_EXPERTS, :] = e / jnp.sum(e, axis=0, keepdims=True)
    aff = aff_ref[...]
    rows = n_sets * N_EXPERTS

    def count_ge(thr):
        return jnp.sum(jnp.where(aff >= thr, 1.0, 0.0), axis=1, keepdims=True)

    def bisect(_, carry):
        lo, hi = carry
        mid = jnp.sqrt(lo) * jnp.sqrt(hi)
        ok = count_ge(mid) >= cap
        return jnp.where(ok, mid, lo), jnp.where(ok, hi, mid)

    lo0 = jnp.full((rows, 1), MIN_NORMAL_F32, F32)
    hi0 = jnp.full((rows, 1), 2.0, F32)
    lo, hi = lax.fori_loop(0, BISECT_STEPS, bisect, (lo0, hi0))
    lo = jnp.where(count_ge(lo) >= cap, lo, 0.0)

    above = aff >= hi
    band = (aff >= lo) & jnp.logical_not(above)
    need = cap - jnp.sum(jnp.where(above, 1.0, 0.0), axis=1, keepdims=True)
    tri = tri_scr[...]
    band_before = _bdot(jnp.where(band, 1.0, 0.0).astype(BF16), tri)
    sel = above | (band & (band_before < need))
    pos = _bdot(jnp.where(sel, 1.0, 0.0).astype(BF16), tri)
    key_ref[...] = jnp.where(sel, pos, -1.0)


def _route(logits, *, row0, n_sets, n):
    cap = EC_FACTOR * n // N_EXPERTS
    rows = n_sets * N_EXPERTS
    blk = row0 // (n_sets * n)
    return pl.pallas_call(
        functools.partial(_route_kernel, n_sets=n_sets, n=n, cap=cap),
        grid=(1,),
        in_specs=[pl.BlockSpec((n_sets * n, LOGIT_LANES), lambda g: (blk, 0))],
        out_specs=[pl.BlockSpec((rows, n), lambda g: (0, 0)), pl.BlockSpec((rows, n), lambda g: (0, 0))],
        out_shape=[jax.ShapeDtypeStruct((rows, n), F32), jax.ShapeDtypeStruct((rows, n), F32)],
        scratch_shapes=[pltpu.VMEM((n, n), BF16)],
        compiler_params=_params(1),
        name=f"route_{n}",
    )(logits)


def _gather_kernel(key_ref, aff_ref, h2_ref, pt_ref, gate_ref, xg_ref, p_scr, *, sps, n, cap, d, nb):
    slot = lax.broadcasted_iota(jnp.int32, (cap, n), 0).astype(F32)
    per_group = LANES // cap
    for s in range(sps):
        key = key_ref[s * N_EXPERTS:(s + 1) * N_EXPERTS, :]
        aff = aff_ref[s * N_EXPERTS:(s + 1) * N_EXPERTS, :]
        for grp in range(N_EXPERTS // per_group):
            pieces = []
            for ex in range(grp * per_group, (grp + 1) * per_group):
                hit = slot == key[ex:ex + 1, :]
                gate_ref[ex, s] = jnp.sum(jnp.where(hit, aff[ex:ex + 1, :], 0.0), axis=1, keepdims=True)
                pieces.append(jnp.where(hit, 1.0, 0.0))
            hits = pieces[0] if per_group == 1 else jnp.concatenate(pieces, axis=0)
            rows = slice(grp * LANES, (grp + 1) * LANES)
            p_scr[s, rows, :] = hits.astype(BF16)
            pt_ref[s, :, rows] = hits.T.astype(BF16)

        p = p_scr[s]
        for c in range(d // nb):
            xg = _bdot(p, h2_ref[s * n:(s + 1) * n, c * nb:(c + 1) * nb]).astype(BF16)
            for ex in range(N_EXPERTS):
                xg_ref[ex, s, :, c * nb:(c + 1) * nb] = xg[ex * cap:(ex + 1) * cap, :]


def _dispatch_gather(key, aff, h2, *, row0, n_sets, n, sps):
    d = h2.shape[1]
    cap = EC_FACTOR * n // N_EXPERTS
    slots = N_EXPERTS * cap
    blk0 = row0 // (sps * n)
    return pl.pallas_call(
        functools.partial(_gather_kernel, sps=sps, n=n, cap=cap, d=d, nb=MXU_WIDTH),
        grid=(n_sets // sps,),
        in_specs=[
            pl.BlockSpec((sps * N_EXPERTS, n), lambda b: (b, 0)),
            pl.BlockSpec((sps * N_EXPERTS, n), lambda b: (b, 0)),
            pl.BlockSpec((sps * n, d), lambda b: (blk0 + b, 0)),
        ],
        out_specs=[
            pl.BlockSpec((sps, n, slots), lambda b: (b, 0, 0)),
            pl.BlockSpec((N_EXPERTS, sps, cap, 1), lambda b: (0, b, 0, 0)),
            pl.BlockSpec((N_EXPERTS, sps, cap, d), lambda b: (0, b, 0, 0)),
        ],
        out_shape=[
            jax.ShapeDtypeStruct((n_sets, n, slots), BF16),
            jax.ShapeDtypeStruct((N_EXPERTS, n_sets, cap, 1), F32),
            jax.ShapeDtypeStruct((N_EXPERTS, n_sets, cap, d), BF16),
        ],
        scratch_shapes=[pltpu.VMEM((sps, slots, n), BF16)],
        compiler_params=_params(1),
        name=f"dispatch_gather_{n}",
    )(key, aff, h2)


def _expert_kernel(xc_ref, xl_ref, gc_ref, gl_ref, w1_ref, w3_ref, w2_ref, yc_ref, yl_ref, hc_scr, hl_scr,
                   *, n_f, fc, rows_c, rows_l, d):
    k = pl.program_id(1)

    @pl.when(k < n_f)
    def _():
        w1 = w1_ref[...].astype(BF16)
        w3 = w3_ref[...].astype(BF16)
        for x_ref, h_scr, rows in ((xc_ref, hc_scr, rows_c), (xl_ref, hl_scr, rows_l)):
            x = x_ref[...].reshape(rows, d)
            a = _bdot(x, w1)
            g = _bdot(x, w3)
            h_scr[k] = (_silu(a) * g).astype(BF16)

    @pl.when(k >= n_f)
    def _():
        w2 = w2_ref[...].astype(BF16)
        for h_scr, g_ref, y_ref, rows in ((hc_scr, gc_ref, yc_ref, rows_c), (hl_scr, gl_ref, yl_ref, rows_l)):
            y = _bdot(h_scr[0], w2[0:fc, :])
            for kk in range(1, n_f):
                y = y + _bdot(h_scr[kk], w2[kk * fc:(kk + 1) * fc, :])
            y_ref[...] = (y * g_ref[...].reshape(rows, 1)).astype(BF16).reshape(y_ref.shape)


def _experts(xg_c, xg_l, gate_c, gate_l, w_e1, w_e3, w_e2, *, fc=512, nc=512):
    n_e, sets_c, cap_c, d = xg_c.shape
    _, sets_l, cap_l, _ = xg_l.shape
    d_ff = w_e1.shape[2]
    n_f = d_ff // fc
    n_c = d // nc
    rows_c, rows_l = sets_c * cap_c, sets_l * cap_l

    def e_in(e, k):
        return jnp.minimum(e + (k >= n_f).astype(jnp.int32), n_e - 1)

    def f_idx(k):
        return jnp.where(k < n_f, k, 0)

    def c_idx(k):
        return jnp.maximum(k - n_f, 0)

    kern = functools.partial(_expert_kernel, n_f=n_f, fc=fc, rows_c=rows_c, rows_l=rows_l, d=d)
    return pl.pallas_call(
        kern,
        grid=(n_e, n_f + n_c),
        in_specs=[
            pl.BlockSpec((None, sets_c, cap_c, d), lambda e, k: (e_in(e, k), 0, 0, 0)),
            pl.BlockSpec((None, sets_l, cap_l, d), lambda e, k: (e_in(e, k), 0, 0, 0)),
            pl.BlockSpec((None, sets_c, cap_c, 1), lambda e, k: (e, 0, 0, 0)),
            pl.BlockSpec((None, sets_l, cap_l, 1), lambda e, k: (e, 0, 0, 0)),
            pl.BlockSpec((None, d, fc), lambda e, k: (e_in(e, k), 0, f_idx(k))),
            pl.BlockSpec((None, d, fc), lambda e, k: (e_in(e, k), 0, f_idx(k))),
            pl.BlockSpec((None, d_ff, nc), lambda e, k: (e, 0, c_idx(k))),
        ],
        out_specs=[
            pl.BlockSpec((None, sets_c, cap_c, nc), lambda e, k: (e, 0, 0, c_idx(k))),
            pl.BlockSpec((None, sets_l, cap_l, nc), lambda e, k: (e, 0, 0, c_idx(k))),
        ],
        out_shape=[
            jax.ShapeDtypeStruct(xg_c.shape, BF16),
            jax.ShapeDtypeStruct(xg_l.shape, BF16),
        ],
        scratch_shapes=[pltpu.VMEM((n_f, rows_c, fc), BF16), pltpu.VMEM((n_f, rows_l, fc), BF16)],
        compiler_params=_params(2),
        name="experts",
    )(xg_c, xg_l, gate_c, gate_l, w_e1, w_e3, w_e2)


def _combine_kernel(pt_ref, y_ref, x1_ref, gate_ref, g_ref, o_ref, acc_scr, *, sps, tn, slots, d, nb):
    for s in range(sps):
        pt = pt_ref[s]
        rows = slice(s * tn, (s + 1) * tn)
        for c in range(d // nb):
            cols = slice(c * nb, (c + 1) * nb)
            moe = _bdot(pt, y_ref[:, s, :, cols].reshape(slots, nb))
            acc_scr[:, cols] = x1_ref[rows, cols] + gate_ref[:, cols] * moe
        o_ref[rows, :] = _rms_rows(acc_scr[...], g_ref[...])


def _combine(pt, y, x1, mod6, g_final, *, row0, n_sets, n, sps, mod_row_fn):
    d = x1.shape[1]
    cap = EC_FACTOR * n // N_EXPERTS
    slots = N_EXPERTS * cap
    tn = min(n, 512)
    nt = n // tn
    assert sps == 1 or nt == 1
    blk0 = row0 // (sps * tn)
    return pl.pallas_call(
        functools.partial(_combine_kernel, sps=sps, tn=tn, slots=slots, d=d, nb=512),
        grid=(n_sets // sps, nt),
        in_specs=[
            pl.BlockSpec((sps, tn, slots), lambda b, r: (b, r, 0)),
            pl.BlockSpec((N_EXPERTS, sps, cap, d), lambda b, r: (0, b, 0, 0)),
            pl.BlockSpec((sps * tn, d), lambda b, r: (blk0 + b * nt + r, 0)),
            pl.BlockSpec((None, None, 1, d), lambda b, r: (5, mod_row_fn(b), 0, 0)),
            _const_spec((1, d)),
        ],
        out_specs=pl.BlockSpec((sps * tn, d), lambda b, r: (b * nt + r, 0)),
        out_shape=jax.ShapeDtypeStruct((n_sets * n, d), F32),
        scratch_shapes=[pltpu.VMEM((tn, d), F32)],
        compiler_params=_params(2),
        name=f"combine_{n}",
    )(pt, y, x1, mod6, g_final)


def _rope_tables(n):
    tpos = jnp.arange(n, dtype=jnp.int32)
    row = (tpos // GRID_W).astype(F32)
    col = (tpos % GRID_W).astype(F32)
    inv = 1.0 / (ROPE_BASE ** (jnp.arange(ROPE_FREQS, dtype=F32) / ROPE_FREQS))
    ang = jnp.stack([row[:, None] * inv, col[:, None] * inv], axis=1)
    cos = jnp.cos(ang)[:, :, None, :]
    sin = jnp.sin(ang)[:, :, None, :]
    cos = jnp.broadcast_to(cos, (n, 2, 2, ROPE_FREQS)).reshape(n, D_ROPE)
    sin = jnp.concatenate([-sin, sin], axis=2).reshape(n, D_ROPE)
    pad = jnp.zeros((n, ROPE_SLOT - D_ROPE), F32)
    return jnp.concatenate([cos, pad], axis=1), jnp.concatenate([sin, pad], axis=1)


def kernel(x_prompt, x_sample, c, cache_ckv, cache_krope, c_ctx, g_attn, g_ffn, w_ada, b_ada, w_in, g_qa,
           w_qb, g_kv, w_uk, w_uv, w_o_mla, g_sgu, w_s, b_s, w_o_gm, w_out, w_router, w_e1, w_e3, w_e2,
           g_final):
    batch, seq, d = x_prompt.shape
    dec_batch, dec_seq, _ = x_sample.shape
    depth = g_attn.shape[0]
    assert depth == 1
    q_lora, kv_lora = g_qa.shape[1], g_kv.shape[1]
    gm_width = g_sgu.shape[1]
    n_ctx, n_lat = batch * seq, dec_batch * dec_seq
    assert dec_batch < MOD_ROWS // 2 + 1

    xc = x_prompt.reshape(n_ctx, d)
    xl = x_sample.reshape(n_lat, d)

    c_rows = jnp.zeros((MOD_ROWS, d), F32).at[:dec_batch].set(c).at[MOD_ROWS // 2].set(c_ctx)
    n_mod_early = 2 * d
    mod_a = _modulation(c_rows, w_ada[0], b_ada[0][None, :], n_cols=n_mod_early)
    mod6_a = mod_a.reshape(MOD_ROWS, 2, 1, d).transpose(1, 0, 2, 3)

    cos_t, sin_t = _rope_tables(dec_seq)

    w_in_t = w_in[0].T
    o_kr = q_lora + kv_lora
    w_small_t = jnp.pad(w_in_t[:o_kr + D_ROPE], ((0, ROPE_SLOT - D_ROPE), (0, 0))).astype(BF16)
    tm = ROW_TILE
    front = -(o_kr + D_ROPE) % WIDE_WEIGHT_BLOCK

    wq3 = w_qb[0].reshape(q_lora, N_HEADS, D_NOPE + D_ROPE)
    wq_nope, wq_rope = wq3[:, :, :D_NOPE], wq3[:, :, D_NOPE:]
    zq = jnp.zeros((q_lora, N_HEADS, HEAD_SLOT - D_NOPE - D_ROPE), F32)
    wq_p = jnp.concatenate([wq_nope, wq_rope, zq], axis=2).reshape(q_lora, N_HEADS * HEAD_SLOT).astype(BF16)

    h, ckv, kr, q_pad, w_all_t, ckv_ctx, kr_ctx = _pre_proj(
        xc, xl, g_attn, mod6_a, w_small_t, g_qa, g_kv, cos_t, sin_t, wq_p, w_in_t, front=front, tm=tm)
    big = _wide_proj(h, w_all_t, n_big=w_in_t.shape[0] - o_kr - D_ROPE, n_gelu=2 * gm_width,
                     tm=MATMUL_ROW_TILE, bn=WIDE_WEIGHT_BLOCK)

    w_uk2 = w_uk[0].reshape(kv_lora, N_HEADS // 2, 2 * D_NOPE).transpose(1, 0, 2).astype(BF16)
    w_uv2 = w_uv[0].reshape(kv_lora, N_HEADS // 2, 2 * D_V).transpose(1, 0, 2).astype(BF16)
    side = (c_rows, w_ada[0], b_ada[0][None, :], n_mod_early, (w_o_mla[0], w_o_gm[0], w_out[0]))
    o_ctx, mod_b, (w_o_mla_b, w_o_gm_b, w_out_b) = _attention(
        q_pad, ckv, kr, w_uk2, w_uv2, None, None,
        row0=0, n_req=batch, n_own=seq, tq=seq, heads_per_iter=N_HEADS, side=side)
    o_lat = _attention(q_pad, ckv, kr, w_uk2, w_uv2, cache_ckv[:, 0], cache_krope[:, 0],
                       row0=n_ctx, n_req=dec_batch, n_own=dec_seq, tq=ATTN_Q_TILE,
                       heads_per_iter=ATTN_HEADS_PER_STEP)
    mod6 = jnp.concatenate([mod6_a, mod_b.reshape(MOD_ROWS, 4, 1, d).transpose(1, 0, 2, 3)], axis=0)

    o_a = _o_proj(o_ctx, o_lat, w_o_mla_b, tm=MATMUL_ROW_TILE)
    merged = _sgu_merge(big, o_a, g_sgu, w_s[0].astype(BF16), b_s[0].T, w_o_gm_b, tm=tm)

    w_router_p = jnp.concatenate(
        [w_router[0], jnp.zeros((d, LOGIT_LANES - N_EXPERTS), F32)], axis=1).astype(BF16)
    x1, h2, logits = _out_proj(merged, xc, xl, mod6, g_ffn, w_out_b, w_router_p,
                               n_lat_seq=dec_seq, tm=tm)

    key_c, aff_c = _route(logits, row0=0, n_sets=batch, n=seq)
    key_l, aff_l = _route(logits, row0=n_ctx, n_sets=dec_batch, n=dec_seq)
    p_c, gate_c, xg_c = _dispatch_gather(key_c, aff_c, h2, row0=0, n_sets=batch, n=seq,
                                         sps=CTX_SETS_PER_GATHER_STEP)
    p_l, gate_l, xg_l = _dispatch_gather(key_l, aff_l, h2, row0=n_ctx, n_sets=dec_batch, n=dec_seq, sps=1)

    y_c, y_l = _experts(xg_c, xg_l, gate_c, gate_l, w_e1[0], w_e3[0], w_e2[0])

    y_prompt = _combine(p_c, y_c, x1, mod6, g_final[None, :], row0=0, n_sets=batch, n=seq,
                        sps=CTX_SETS_PER_COMBINE_STEP, mod_row_fn=lambda b: MOD_ROWS // 2)
    y_sample = _combine(p_l, y_l, x1, mod6, g_final[None, :], row0=n_ctx, n_sets=dec_batch, n=dec_seq, sps=1,
                        mod_row_fn=lambda b: b)

    new_ckv = ckv_ctx.reshape(batch, 1, seq, kv_lora)
    new_krope = kr_ctx.reshape(batch, 1, seq, D_ROPE)
    return (y_prompt.reshape(batch, seq, d), y_sample.reshape(dec_batch, dec_seq, d), new_ckv, new_krope)
```
